```python
import jax, jax.numpy as jnp
from jax import lax
import numpy as np

D_MODEL = 1024
BATCH = 2
SEQ = 8192
DEPTH = 2

PLE_DIM = 256
RET_HEADS = 6
RET_HEAD_DIM = 64
RET_WIDTH = RET_HEADS * RET_HEAD_DIM
POOL_WINDOWS = (2, 4, 8, 16)
POOL_GROUPS = len(POOL_WINDOWS)
POOL_GROUP_DIM = 64
POOL_WIDTH = POOL_GROUPS * POOL_GROUP_DIM
MLSTM_HEADS = 4
MLSTM_HEAD_DIM = 96
MLSTM_WIDTH = MLSTM_HEADS * MLSTM_HEAD_DIM
MLSTM_CONV = 4
MIX_WIDTH = RET_WIDTH + POOL_WIDTH + MLSTM_WIDTH
IN_WIDTH = 4 * RET_WIDTH + POOL_WIDTH + 4 * MLSTM_WIDTH + 2 * MLSTM_HEADS
CHUNK = 128
D_FF = -(-8 * D_MODEL // (3 * 256)) * 256
ROPE_BASE = 10000.0
EPS = 1e-6

kernel_name = "hybrid_retention_pool_mlstm_block"


def rmsnorm(x, g):
    xf = x.astype(jnp.float32)
    y = xf * lax.rsqrt(jnp.mean(xf * xf, axis=-1, keepdims=True) + EPS)
    return (y * g.astype(jnp.float32)).astype(x.dtype)


def head_norm(h, g):
    b, s, nh, d = h.shape
    mu = jnp.mean(h, axis=-1, keepdims=True)
    hc = h - mu
    var = jnp.mean(hc * hc, axis=-1, keepdims=True)
    y = (hc * lax.rsqrt(var + EPS)).reshape(b, s, nh * d)
    return y * g.astype(jnp.float32)


def rope(x, pos):
    half = x.shape[-1] // 2
    inv = ROPE_BASE ** (-jnp.arange(half, dtype=jnp.float32) / half)
    ang = pos.astype(jnp.float32)[..., None] * inv
    cos = jnp.cos(ang)[:, :, None, :]
    sin = jnp.sin(ang)[:, :, None, :]
    x1, x2 = x[..., :half], x[..., half:]
    return jnp.concatenate([x1 * cos - x2 * sin, x1 * sin + x2 * cos], axis=-1)


def to_chunks(x):
    b, s, nh, d = x.shape
    return x.reshape(b, s // CHUNK, CHUNK, nh, d).transpose(0, 3, 1, 2, 4)


def from_chunks(y):
    b, nh, nc, c, d = y.shape
    return y.transpose(0, 2, 3, 1, 4).reshape(b, nc * c, nh, d)


def causal_conv(x, w, bias):
    k, c = w.shape
    y = lax.conv_general_dilated(
        x, w[:, None, :].astype(x.dtype), window_strides=(1,),
        padding=((k - 1, 0),), dimension_numbers=("NWC", "WIO", "NWC"),
        feature_group_count=c)
    return y + bias.astype(x.dtype)


def retention(q, k, v, g, pos, gn_gain):
    f32 = jnp.float32
    b, s, _ = q.shape
    nh, d = RET_HEADS, RET_HEAD_DIM
    qh = rope(q.reshape(b, s, nh, d).astype(f32), pos)
    kh = rope(k.reshape(b, s, nh, d).astype(f32), pos) * (d ** -0.5)
    vh = v.reshape(b, s, nh, d).astype(f32)
    log_gamma = jnp.log1p(-(2.0 ** (-5.0 - jnp.arange(nh, dtype=f32))))
    qc, kc, vc = to_chunks(qh), to_chunks(kh), to_chunks(vh)
    idx = jnp.arange(CHUNK, dtype=f32)
    rel = idx[:, None] - idx[None, :]
    decay = jnp.where(rel >= 0, jnp.exp(log_gamma[:, None, None] * jnp.maximum(rel, 0.0)), 0.0)
    scores = jnp.einsum('bhncd,bhnsd->bhncs', qc, kc) * decay[None, :, None]
    y_intra = jnp.einsum('bhncs,bhnse->bhnce', scores, vc)
    zeta = jnp.exp(log_gamma[:, None] * (CHUNK - 1 - idx))
    kv = jnp.einsum('bhncd,bhnce->bhnde', kc * zeta[None, :, None, :, None], vc)
    chunk_decay = jnp.exp(log_gamma * CHUNK)[None, :, None, None]

    def step(r, kv_n):
        return chunk_decay * r + kv_n, r

    _, r_prev = lax.scan(step, jnp.zeros((b, nh, d, d), f32), jnp.moveaxis(kv, 2, 0))
    r_prev = jnp.moveaxis(r_prev, 0, 2)
    xi = jnp.exp(log_gamma[:, None] * (idx + 1.0))
    y_cross = jnp.einsum('bhncd,bhnde->bhnce', qc, r_prev) * xi[None, :, None, :, None]
    y = head_norm(from_chunks(y_intra + y_cross), gn_gain)
    return (jax.nn.silu(g.astype(f32)) * y).astype(q.dtype)


def pool_mix(u, w_pool, pool_scale):
    f32 = jnp.float32
    b, s, _ = u.shape
    uf = u.astype(f32)
    csum = jnp.pad(lax.cumsum(uf, axis=1), ((0, 0), (1, 0), (0, 0)))
    t1 = jnp.arange(1, s + 1, dtype=f32)
    outs = []
    for gi, w in enumerate(POOL_WINDOWS):
        lo, hi = gi * POOL_GROUP_DIM, (gi + 1) * POOL_GROUP_DIM
        cs = csum[:, :, lo:hi]
        upper = cs[:, 1:]
        lower = jnp.pad(cs[:, :s + 1 - w], ((0, 0), (w - 1, 0), (0, 0)))
        count = jnp.minimum(t1, float(w))[None, :, None]
        outs.append((upper - lower) / count - uf[:, :, lo:hi])
    pooled = jnp.stack(outs, axis=2)
    mixed = jnp.einsum('bsgc,gcd->bsgd', pooled, w_pool.astype(f32)).reshape(b, s, POOL_WIDTH)
    return (mixed * pool_scale.astype(f32)).astype(u.dtype)


def mlstm(q, k, v, o, i_pre, f_pre, conv_w, conv_b, b_i, b_f, gn_gain):
    f32 = jnp.float32
    b, s, _ = q.shape
    nh, d = MLSTM_HEADS, MLSTM_HEAD_DIM
    qk = jax.nn.silu(causal_conv(jnp.concatenate([q, k], axis=-1).astype(f32), conv_w.astype(f32), conv_b))
    qh = qk[..., :MLSTM_WIDTH].reshape(b, s, nh, d) * (d ** -0.5)
    kh = qk[..., MLSTM_WIDTH:].reshape(b, s, nh, d)
    vh = v.reshape(b, s, nh, d).astype(f32)
    log_i = i_pre.astype(f32) + b_i.astype(f32)
    log_f = jax.nn.log_sigmoid(f_pre.astype(f32) + b_f.astype(f32))
    nc = s // CHUNK
    qc, kc, vc = to_chunks(qh), to_chunks(kh), to_chunks(vh)
    li = log_i.reshape(b, nc, CHUNK, nh).transpose(0, 3, 1, 2)
    lf = log_f.reshape(b, nc, CHUNK, nh).transpose(0, 3, 1, 2)
    bcum = lax.cumsum(lf, axis=3)
    b_last = bcum[..., -1]
    causal = jnp.tril(jnp.ones((CHUNK, CHUNK), dtype=bool))
    log_d = jnp.where(causal, bcum[..., :, None] - bcum[..., None, :] + li[..., None, :], -jnp.inf)
    log_w = b_last[..., None] - bcum + li
    a = jnp.max(log_w, axis=-1)
    kw = kc * jnp.exp(log_w - a[..., None])[..., None]
    kv_loc = jnp.einsum('bhncd,bhnce->bhnde', kw, vc)
    n_loc = jnp.sum(kw, axis=3)

    def step(carry, xs):
        c_st, n_st, m_st = carry
        kv_n, n_n, a_n, bl_n = xs
        m_new = jnp.maximum(bl_n + m_st, a_n)
        s_old = jnp.exp(bl_n + m_st - m_new)
        s_new = jnp.exp(a_n - m_new)
        c_new = s_old[..., None, None] * c_st + s_new[..., None, None] * kv_n
        n_new = s_old[..., None] * n_st + s_new[..., None] * n_n
        return (c_new, n_new, m_new), (c_st, n_st, m_st)

    init = (jnp.zeros((b, nh, d, d), f32), jnp.zeros((b, nh, d), f32), jnp.zeros((b, nh), f32))
    xs = (jnp.moveaxis(kv_loc, 2, 0), jnp.moveaxis(n_loc, 2, 0), jnp.moveaxis(a, 2, 0), jnp.moveaxis(b_last, 2, 0))
    _, (c_prev, n_prev, m_prev) = lax.scan(step, init, xs)
    c_prev = jnp.moveaxis(c_prev, 0, 2)
    n_prev = jnp.moveaxis(n_prev, 0, 2)
    m_prev = jnp.moveaxis(m_prev, 0, 2)
    log_inter = bcum + m_prev[..., None]
    m_row = jnp.maximum(log_inter, jnp.max(log_d, axis=-1))
    sc = jnp.einsum('bhncd,bhnsd->bhncs', qc, kc) * jnp.exp(log_d - m_row[..., None])
    inter = jnp.exp(log_inter - m_row)
    num = jnp.einsum('bhncs,bhnse->bhnce', sc, vc) + inter[..., None] * jnp.einsum('bhncd,bhnde->bhnce', qc, c_prev)
    den = jnp.sum(sc, axis=-1) + inter * jnp.einsum('bhncd,bhnd->bhnc', qc, n_prev)
    den = jnp.maximum(jnp.abs(den), jnp.exp(-m_row))
    h = head_norm(from_chunks(num / den[..., None]), gn_gain)
    return (jax.nn.sigmoid(o.astype(f32)) * h).astype(q.dtype)


def hybrid_layer(x, p_i, positions, norm_mix, w_in, ret_gn, pool_w, pool_scale,
                 conv_w, conv_b, b_igate, b_fgate, mlstm_gn, w_out, norm_ffn,
                 w_gate_up, w_down, norm_ple, w_ple_gate, w_ple_proj):
    h = rmsnorm(x, norm_mix)
    z = h @ w_in
    sizes = [RET_WIDTH] * 4 + [POOL_WIDTH] + [MLSTM_WIDTH] * 4 + [MLSTM_HEADS] * 2
    offs = np.cumsum(sizes)[:-1].tolist()
    (rq, rk, rv, rg, pu, mq, mk, mv, mo, mi, mf) = jnp.split(z, offs, axis=-1)
    y_ret = retention(rq, rk, rv, rg, positions, ret_gn)
    y_pool = pool_mix(pu, pool_w, pool_scale)
    y_ml = mlstm(mq, mk, mv, mo, mi, mf, conv_w, conv_b, b_igate, b_fgate, mlstm_gn)
    x = x + jnp.concatenate([y_ret, y_pool, y_ml], axis=-1) @ w_out
    h = rmsnorm(x, norm_ffn)
    gu = h @ w_gate_up
    x = x + (jax.nn.silu(gu[..., :D_FF]) * gu[..., D_FF:]) @ w_down
    hp = rmsnorm(x, norm_ple)
    x = x + jax.nn.sigmoid(hp @ w_ple_gate) * (p_i @ w_ple_proj)
    return x


def setup_inputs(seed: int = 0) -> dict:
    key = jax.random.key(seed)
    ks = jax.random.split(key, 24)
    f32 = jnp.float32

    def nrm(k, shape, scale):
        return jax.random.normal(k, shape, f32) * scale

    def gain(k, shape):
        return 1.0 + 0.05 * jax.random.normal(k, shape, f32)

    return {
        "x": nrm(ks[0], (BATCH, SEQ, D_MODEL), 1.0),
        "p": nrm(ks[1], (DEPTH, BATCH, SEQ, PLE_DIM), 1.0),
        "positions": jnp.broadcast_to(jnp.arange(SEQ, dtype=jnp.int32), (BATCH, SEQ)),
        "norm_mix": gain(ks[2], (DEPTH, D_MODEL)),
        "w_in": nrm(ks[3], (DEPTH, D_MODEL, IN_WIDTH), D_MODEL ** -0.5),
        "ret_gn": gain(ks[4], (DEPTH, RET_WIDTH)),
        "pool_w": nrm(ks[5], (DEPTH, POOL_GROUPS, POOL_GROUP_DIM, POOL_GROUP_DIM), POOL_GROUP_DIM ** -0.5),
        "pool_scale": gain(ks[6], (DEPTH, POOL_WIDTH)),
        "conv_w": nrm(ks[7], (DEPTH, MLSTM_CONV, 2 * MLSTM_WIDTH), MLSTM_CONV ** -0.5),
        "conv_b": nrm(ks[8], (DEPTH, 2 * MLSTM_WIDTH), 0.02),
        "b_igate": nrm(ks[9], (DEPTH, MLSTM_HEADS), 0.1),
        "b_fgate": jnp.broadcast_to(jnp.linspace(3.0, 6.0, MLSTM_HEADS, dtype=f32), (DEPTH, MLSTM_HEADS)) + nrm(ks[10], (DEPTH, MLSTM_HEADS), 0.1),
        "mlstm_gn": gain(ks[11], (DEPTH, MLSTM_WIDTH)),
        "w_out": nrm(ks[12], (DEPTH, MIX_WIDTH, D_MODEL), MIX_WIDTH ** -0.5),
        "norm_ffn": gain(ks[13], (DEPTH, D_MODEL)),
        "w_gate_up": nrm(ks[14], (DEPTH, D_MODEL, 2 * D_FF), D_MODEL ** -0.5),
        "w_down": nrm(ks[15], (DEPTH, D_FF, D_MODEL), D_FF ** -0.5),
        "norm_ple": gain(ks[16], (DEPTH, D_MODEL)),
        "w_ple_gate": nrm(ks[17], (DEPTH, D_MODEL, D_MODEL), D_MODEL ** -0.5),
        "w_ple_proj": nrm(ks[18], (DEPTH, PLE_DIM, D_MODEL), PLE_DIM ** -0.5),
        "norm_final": gain(ks[19], (D_MODEL,)),
    }


def reference(x, p, positions, norm_mix, w_in, ret_gn, pool_w, pool_scale,
              conv_w, conv_b, b_igate, b_fgate, mlstm_gn, w_out, norm_ffn,
              w_gate_up, w_down, norm_ple, w_ple_gate, w_ple_proj, norm_final):
    for i in range(DEPTH):
        x = hybrid_layer(x, p[i], positions, norm_mix[i], w_in[i], ret_gn[i],
                         pool_w[i], pool_scale[i], conv_w[i], conv_b[i],
                         b_igate[i], b_fgate[i], mlstm_gn[i], w_out[i],
                         norm_ffn[i], w_gate_up[i], w_down[i], norm_ple[i],
                         w_ple_gate[i], w_ple_proj[i])
    return rmsnorm(x, norm_final)
```

```python
import functools

import numpy as np
import jax
import jax.numpy as jnp
from jax import lax
from jax.experimental import pallas as pl
from jax.experimental.pallas import tpu as pltpu

F32 = jnp.float32
BF16 = jnp.bfloat16

D_MODEL = 1024
PLE_DIM = 256
RET_HEADS = 6
RET_HEAD_DIM = 64
RET_WIDTH = RET_HEADS * RET_HEAD_DIM
RET_PAIRS = RET_HEADS // 2
POOL_WINDOWS = (2, 4, 8, 16)
POOL_GROUP_DIM = 64
POOL_WIDTH = len(POOL_WINDOWS) * POOL_GROUP_DIM
MLSTM_HEADS = 4
MLSTM_HEAD_DIM = 96
MLSTM_WIDTH = MLSTM_HEADS * MLSTM_HEAD_DIM
MLSTM_CONV = 4
CHUNK = 128
D_FF = 2816
ROPE_BASE = 10000.0
EPS = 1e-6

LANES = 128
HEAD_PAD = LANES
MLSTM_WIDTH_P = MLSTM_HEADS * HEAD_PAD

RQ = 0
RK = RQ + RET_WIDTH
RV = RK + RET_WIDTH
RG = RV + RET_WIDTH
PU = RG + RET_WIDTH
MQ = PU + POOL_WIDTH
MK = MQ + MLSTM_WIDTH_P
MV = MK + MLSTM_WIDTH_P
MO = MV + MLSTM_WIDTH_P
GT = MO + MLSTM_WIDTH_P
Z_WIDTH = GT + LANES
HIST_COLS = MV - PU
HIST_ROWS = 16
MIX_WIDTH_P = RET_WIDTH + POOL_WIDTH + MLSTM_WIDTH_P
NEG = -1e30

ROW_TILE = 512
SEQ_TILE = 256
VMEM_LIMIT = 56 * 1024 * 1024


def _rmsnorm(x, g):
    return x * lax.rsqrt(jnp.mean(x * x, axis=-1, keepdims=True) + EPS) * g


def _sigmoid(x):
    return 1.0 / (1.0 + jnp.exp(-x))


def _dot(a, b):
    return jnp.dot(a, b, preferred_element_type=F32)


def _dot_nt(a, b):
    return lax.dot_general(a, b, (((1,), (1,)), ((), ())), preferred_element_type=F32)


def _trig_kernel(pos_ref, inv_ref, sign_ref, cos_ref, sin_ref):
    ang = pos_ref[...] * inv_ref[...]
    cos_ref[...] = jnp.cos(ang)
    sin_ref[...] = jnp.sin(ang) * sign_ref[...]


def _trig_tables(pos_col):
    t = pos_col.shape[0]
    half = RET_HEAD_DIM // 2
    inv = ROPE_BASE ** (-jnp.arange(half, dtype=F32) / half)
    inv_row = jnp.tile(inv, LANES // half).reshape(1, LANES)
    sign = np.where((np.arange(LANES) % RET_HEAD_DIM) < half, -1.0, 1.0).astype(np.float32).reshape(1, LANES)
    tile = 2 * ROW_TILE
    row = pl.BlockSpec((1, LANES), lambda i: (0, 0))
    out = pl.BlockSpec((tile, LANES), lambda i: (i, 0))
    return pl.pallas_call(
        _trig_kernel,
        grid=(t // tile,),
        in_specs=[pl.BlockSpec((tile, 1), lambda i: (i, 0)), row, row],
        out_specs=[out, out],
        out_shape=[jax.ShapeDtypeStruct((t, LANES), F32)] * 2,
        compiler_params=pltpu.CompilerParams(dimension_semantics=("arbitrary",)),
        name="rope_tables",
    )(pos_col, inv_row, jnp.asarray(sign))


def _inproj_kernel(x_ref, g_ref, w_ref, z_ref):
    h = _rmsnorm(x_ref[...], g_ref[...]).astype(BF16)
    z_ref[...] = _dot(h, w_ref[...])


def _inproj(x, gain, w):
    t = x.shape[0]
    return pl.pallas_call(
        _inproj_kernel,
        grid=(t // ROW_TILE,),
        in_specs=[
            pl.BlockSpec((ROW_TILE, D_MODEL), lambda i: (i, 0)),
            pl.BlockSpec((1, D_MODEL), lambda i: (0, 0)),
            pl.BlockSpec((D_MODEL, Z_WIDTH), lambda i: (0, 0)),
        ],
        out_specs=pl.BlockSpec((ROW_TILE, Z_WIDTH), lambda i: (i, 0)),
        out_shape=jax.ShapeDtypeStruct((t, Z_WIDTH), F32),
        compiler_params=pltpu.CompilerParams(
            dimension_semantics=("arbitrary",), vmem_limit_bytes=VMEM_LIMIT),
        name="in_projection",
    )(x, gain, w)


def _swap_halves(x, lo_half):
    return jnp.where(lo_half, pltpu.roll(x, LANES - RET_HEAD_DIM // 2, 1), pltpu.roll(x, RET_HEAD_DIM // 2, 1))


def _lane_scan(x, op, fill):
    lane = lax.broadcasted_iota(jnp.int32, x.shape, 1)
    sh = 1
    while sh < LANES:
        x = op(x, jnp.where(lane >= sh, pltpu.roll(x, sh, 1), fill))
        sh *= 2
    return x


def _mixer_kernel(z_ref, cos_ref, sin_ref, dtab_ref, xi_ref, zeta_ref, cd_ref, bd_ref,
                  rgn_ref, wpool_ref, pscale_ref, convw_ref, convb_ref, gbias_ref, mgn_ref,
                  mix_ref, hbuf, rstate, cstate, mstate, *, seq_tile):
    j = pl.program_id(1)

    @pl.when(j == 0)
    def _():
        hbuf[0:HIST_ROWS, :] = jnp.zeros((HIST_ROWS, HIST_COLS), F32)
        rstate[...] = jnp.zeros_like(rstate)
        cstate[...] = jnp.zeros_like(cstate)
        mstate[...] = jnp.zeros_like(mstate)

    hbuf[HIST_ROWS:, :] = z_ref[:, PU:MV]

    lane = lax.broadcasted_iota(jnp.int32, (CHUNK, LANES), 1)
    row_t = lax.broadcasted_iota(jnp.int32, (CHUNK, LANES), 0)
    head_a = lane < RET_HEAD_DIM
    lo_half = (lane % RET_HEAD_DIM) < (RET_HEAD_DIM // 2)
    causal = lane <= row_t
    lane_p = lax.broadcasted_iota(jnp.int32, (CHUNK, POOL_WIDTH), 1)
    row_p = lax.broadcasted_iota(jnp.int32, (CHUNK, POOL_WIDTH), 0)

    for c in range(seq_tile // CHUNK):
        r0 = c * CHUNK
        rows = pl.ds(r0, CHUNK)

        def hist(k, c0, c1):
            return hbuf[pl.ds(HIST_ROWS + r0 - k, CHUNK), c0:c1]

        u = hist(0, 0, POOL_WIDTH)
        s2 = u + hist(1, 0, POOL_WIDTH)
        s4 = s2 + hist(2, 0, POOL_WIDTH) + hist(3, 0, POOL_WIDTH)
        s8 = s4
        for k in range(4, 8):
            s8 = s8 + hist(k, 0, POOL_WIDTH)
        s16 = s8
        for k in range(8, 16):
            s16 = s16 + hist(k, 0, POOL_WIDTH)
        g0, g1, g2 = (lane_p < POOL_GROUP_DIM, lane_p < 2 * POOL_GROUP_DIM, lane_p < 3 * POOL_GROUP_DIM)
        wsum = jnp.where(g0, s2, jnp.where(g1, s4, jnp.where(g2, s8, s16)))
        width = jnp.where(g0, 2, jnp.where(g1, 4, jnp.where(g2, 8, 16)))
        tpos = row_p + (j * seq_tile + r0 + 1)
        count = jnp.minimum(tpos, width).astype(F32)
        pooled = wsum / count - u
        y_pool = _dot(pooled.astype(BF16), wpool_ref[...]) * pscale_ref[...]
        mix_ref[rows, RET_WIDTH:RET_WIDTH + POOL_WIDTH] = y_pool.astype(BF16)

        cosv = cos_ref[rows, :]
        sinv = sin_ref[rows, :]
        for p in range(RET_PAIRS):
            cs = slice(p * LANES, (p + 1) * LANES)
            qp = z_ref[rows, RQ + p * LANES:RQ + (p + 1) * LANES]
            kp = z_ref[rows, RK + p * LANES:RK + (p + 1) * LANES]
            vp = z_ref[rows, RV + p * LANES:RV + (p + 1) * LANES]
            gp = z_ref[rows, RG + p * LANES:RG + (p + 1) * LANES]
            q = qp * cosv + _swap_halves(qp, lo_half) * sinv
            k = (kp * cosv + _swap_halves(kp, lo_half) * sinv) * (RET_HEAD_DIM ** -0.5)
            q2 = jnp.concatenate([jnp.where(head_a, q, 0.0), jnp.where(head_a, 0.0, q)], axis=0)
            scores = _dot_nt(q2.astype(BF16), k.astype(BF16)) * dtab_ref[p]
            pcat = jnp.concatenate([scores[:CHUNK], scores[CHUNK:]], axis=1)
            v2 = jnp.concatenate([jnp.where(head_a, vp, 0.0), jnp.where(head_a, 0.0, vp)], axis=0)
            y = _dot(pcat.astype(BF16), v2.astype(BF16))
            y = y + _dot((q * xi_ref[p]).astype(BF16), rstate[p].astype(BF16))
            kz_t = (k * zeta_ref[p]).T.astype(BF16)
            rstate[p] = rstate[p] * cd_ref[p] + _dot(kz_t, vp.astype(BF16)) * bd_ref[...]
            inv_d = 1.0 / RET_HEAD_DIM
            sa = jnp.sum(jnp.where(head_a, y, 0.0), axis=1, keepdims=True)
            sb = jnp.sum(jnp.where(head_a, 0.0, y), axis=1, keepdims=True)
            yc = y - jnp.where(head_a, sa, sb) * inv_d
            yc2 = yc * yc
            va = jnp.sum(jnp.where(head_a, yc2, 0.0), axis=1, keepdims=True)
            vb = jnp.sum(jnp.where(head_a, 0.0, yc2), axis=1, keepdims=True)
            var = jnp.where(head_a, va, vb) * inv_d
            yn = yc * lax.rsqrt(var + EPS) * rgn_ref[:, cs]
            mix_ref[rows, cs] = (gp * _sigmoid(gp) * yn).astype(BF16)

        conv = convb_ref[...]
        for kk in range(MLSTM_CONV):
            conv = conv + convw_ref[kk:kk + 1, :] * hist(MLSTM_CONV - 1 - kk, POOL_WIDTH, HIST_COLS)
        qk = conv * _sigmoid(conv)

        gates_t = z_ref[rows, GT:GT + LANES].T
        li = gates_t[0:8] + gbias_ref[0:8, :]
        fpre = gates_t[8:16] + gbias_ref[8:16, :]
        lf = jnp.minimum(fpre, 0.0) - jnp.log(1.0 + jnp.exp(-jnp.abs(fpre)))
        bcum = _lane_scan(lf, jnp.add, 0.0)
        g = li - bcum
        cmax = _lane_scan(g, jnp.maximum, NEG)
        m_prev = mstate[...]
        big_g = jnp.maximum(m_prev, cmax)
        inter = jnp.exp(m_prev - big_g)
        emr = jnp.exp(-(bcum + big_g))
        g_last = jnp.broadcast_to(big_g[:, LANES - 1:LANES], (8, LANES))
        b_last = jnp.broadcast_to(bcum[:, LANES - 1:LANES], (8, LANES))
        wk = jnp.exp(g - g_last)
        mstate[...] = b_last + g_last
        cols = jnp.concatenate(
            [big_g, inter, emr, wk, jnp.zeros((CHUNK - 32, LANES), F32)], axis=0).T

        for h in range(MLSTM_HEADS):
            hs = slice(h * HEAD_PAD, (h + 1) * HEAD_PAD)
            qh = qk[:, h * HEAD_PAD:(h + 1) * HEAD_PAD] * (MLSTM_HEAD_DIM ** -0.5)
            kh = qk[:, MLSTM_WIDTH_P + h * HEAD_PAD:MLSTM_WIDTH_P + (h + 1) * HEAD_PAD]
            vh = z_ref[rows, MV + h * HEAD_PAD:MV + (h + 1) * HEAD_PAD]
            oh = z_ref[rows, MO + h * HEAD_PAD:MO + (h + 1) * HEAD_PAD]
            v_aug = jnp.where(lane == MLSTM_HEAD_DIM, 1.0, vh)
            g_col = cols[:, h:h + 1]
            inter_col = cols[:, 8 + h:9 + h]
            emr_col = cols[:, 16 + h:17 + h]
            wk_col = cols[:, 24 + h:25 + h]
            dmat = jnp.exp(jnp.where(causal, g[h:h + 1, :] - g_col, NEG))
            sc = _dot_nt(qh.astype(BF16), kh.astype(BF16)) * dmat
            lhs = jnp.concatenate([sc, qh * inter_col], axis=1).astype(BF16)
            rhs = jnp.concatenate([v_aug, cstate[h]], axis=0).astype(BF16)
            nd = _dot(lhs, rhs)
            den = nd[:, MLSTM_HEAD_DIM:MLSTM_HEAD_DIM + 1]
            hval = nd / jnp.maximum(jnp.abs(den), emr_col)
            real = lane < MLSTM_HEAD_DIM
            mu = jnp.sum(jnp.where(real, hval, 0.0), axis=1, keepdims=True) / MLSTM_HEAD_DIM
            hc = jnp.where(real, hval - mu, 0.0)
            var = jnp.sum(hc * hc, axis=1, keepdims=True) / MLSTM_HEAD_DIM
            hn = hc * lax.rsqrt(var + EPS) * mgn_ref[:, hs]
            c0 = RET_WIDTH + POOL_WIDTH + h * HEAD_PAD
            mix_ref[rows, c0:c0 + HEAD_PAD] = (_sigmoid(oh) * hn).astype(BF16)
            kw_t = (kh * wk_col).T.astype(BF16)
            s_old = cols[CHUNK - 1:CHUNK, 8 + h:9 + h]
            cstate[h] = cstate[h] * s_old + _dot(kw_t, v_aug.astype(BF16))

    hbuf[0:HIST_ROWS, :] = hbuf[seq_tile:seq_tile + HIST_ROWS, :]


def _retention_tables():
    lg = np.log1p(-(2.0 ** (-5.0 - np.arange(RET_HEADS, dtype=np.float64))))
    idx = np.arange(CHUNK, dtype=np.float64)
    rel = idx[:, None] - idx[None, :]
    head_of_lane = np.arange(LANES) // RET_HEAD_DIM
    dtab, xi, zeta, cd = [], [], [], []
    same = (head_of_lane[:, None] == head_of_lane[None, :])
    for p in range(RET_PAIRS):
        hl = lg[2 * p + head_of_lane]
        dtab.append(np.concatenate(
            [np.where(rel >= 0, np.exp(lg[2 * p + a] * np.maximum(rel, 0.0)), 0.0) for a in range(2)], axis=0))
        xi.append(np.exp(hl[None, :] * (idx[:, None] + 1.0)))
        zeta.append(np.exp(hl[None, :] * (CHUNK - 1 - idx[:, None])))
        cd.append(np.where(same, np.exp(hl * CHUNK)[:, None], 0.0))
    f = lambda a: jnp.asarray(np.stack(a).astype(np.float32))
    return f(dtab), f(xi), f(zeta), f(cd), jnp.asarray(same.astype(np.float32))


def _mixers(z, cos_t, sin_t, tables, rgn, wpool, pscale, convw, convb, gbias, mgn, batch, seq):
    t = z.shape[0]
    n_tiles = seq // SEQ_TILE
    dtab, xi, zeta, cd, bd = tables
    tile = lambda w: pl.BlockSpec((SEQ_TILE, w), lambda b, j: (b * n_tiles + j, 0))
    full = lambda a: pl.BlockSpec(a.shape, lambda b, j: (0,) * a.ndim)
    consts = (dtab, xi, zeta, cd, bd, rgn, wpool, pscale, convw, convb, gbias, mgn)
    return pl.pallas_call(
        functools.partial(_mixer_kernel, seq_tile=SEQ_TILE),
        grid=(batch, n_tiles),
        in_specs=[tile(Z_WIDTH), tile(LANES), tile(LANES)] + [full(a) for a in consts],
        out_specs=tile(MIX_WIDTH_P),
        out_shape=jax.ShapeDtypeStruct((t, MIX_WIDTH_P), BF16),
        scratch_shapes=[
            pltpu.VMEM((SEQ_TILE + HIST_ROWS, HIST_COLS), F32),
            pltpu.VMEM((RET_PAIRS, LANES, LANES), F32),
            pltpu.VMEM((MLSTM_HEADS, LANES, LANES), F32),
            pltpu.VMEM((8, LANES), F32),
        ],
        compiler_params=pltpu.CompilerParams(
            dimension_semantics=("arbitrary", "arbitrary"), vmem_limit_bytes=VMEM_LIMIT),
        name="token_mixers",
    )(z, cos_t, sin_t, *consts)


FF_CHUNK = 256


def _channel_kernel(x_ref, mix_ref, p_ref, wout_ref, nffn_ref, wgu_ref, wdown_ref, nple_ref,
                    wpg_ref, wpp_ref, nfin_ref, o_ref, *, final):
    x1 = x_ref[...] + _dot(mix_ref[...], wout_ref[...])
    h = _rmsnorm(x1, nffn_ref[...]).astype(BF16)
    acc = x1
    for c in range(D_FF // FF_CHUNK):
        gate = _dot(h, wgu_ref[:, c * FF_CHUNK:(c + 1) * FF_CHUNK])
        up = _dot(h, wgu_ref[:, D_FF + c * FF_CHUNK:D_FF + (c + 1) * FF_CHUNK])
        act = (gate * _sigmoid(gate) * up).astype(BF16)
        acc = acc + _dot(act, wdown_ref[c * FF_CHUNK:(c + 1) * FF_CHUNK, :])
    hp = _rmsnorm(acc, nple_ref[...]).astype(BF16)
    emb = _dot(p_ref[...].astype(BF16), wpp_ref[...])
    x3 = acc + _sigmoid(_dot(hp, wpg_ref[...])) * emb
    if final:
        x3 = _rmsnorm(x3, nfin_ref[...])
    o_ref[...] = x3


def _channel(x, mix, p, wout, nffn, wgu, wdown, nple, wpg, wpp, nfin, final):
    t = x.shape[0]
    tile = lambda w: pl.BlockSpec((ROW_TILE, w), lambda i: (i, 0))
    full = lambda a: pl.BlockSpec(a.shape, lambda i: (0,) * a.ndim, pipeline_mode=pl.Buffered(1))
    consts = (wout, nffn, wgu, wdown, nple, wpg, wpp, nfin)
    return pl.pallas_call(
        functools.partial(_channel_kernel, final=final),
        grid=(t // ROW_TILE,),
        in_specs=[tile(D_MODEL), tile(MIX_WIDTH_P), tile(PLE_DIM)] + [full(a) for a in consts],
        out_specs=tile(D_MODEL),
        out_shape=jax.ShapeDtypeStruct((t, D_MODEL), F32),
        compiler_params=pltpu.CompilerParams(
            dimension_semantics=("arbitrary",), vmem_limit_bytes=VMEM_LIMIT),
        name="channel_mixing",
    )(x, mix, p, *consts)


def _pad_heads(w):
    lead = w.shape[:-1]
    w = w.reshape(lead + (MLSTM_HEADS, MLSTM_HEAD_DIM))
    w = jnp.pad(w, [(0, 0)] * len(lead) + [(0, 0), (0, HEAD_PAD - MLSTM_HEAD_DIM)])
    return w.reshape(lead + (MLSTM_WIDTH_P,))


def _pad_in_proj(w):
    o = 4 * RET_WIDTH + POOL_WIDTH
    parts = [w[:, :o]]
    for _ in range(4):
        parts.append(_pad_heads(w[:, o:o + MLSTM_WIDTH]))
        o += MLSTM_WIDTH
    zeros = lambda n: jnp.zeros((w.shape[0], n), w.dtype)
    parts += [w[:, o:o + MLSTM_HEADS], zeros(8 - MLSTM_HEADS),
              w[:, o + MLSTM_HEADS:o + 2 * MLSTM_HEADS], zeros(LANES - 8 - MLSTM_HEADS)]
    return jnp.concatenate(parts, axis=1)


def _pad_out_proj(w):
    o = RET_WIDTH + POOL_WIDTH
    tail = w[o:].reshape(MLSTM_HEADS, MLSTM_HEAD_DIM, w.shape[1])
    tail = jnp.pad(tail, [(0, 0), (0, HEAD_PAD - MLSTM_HEAD_DIM), (0, 0)]).reshape(MLSTM_WIDTH_P, w.shape[1])
    return jnp.concatenate([w[:o], tail], axis=0)


def _block_diag(w):
    g, d, _ = w.shape
    out = jnp.zeros((g * d, g * d), w.dtype)
    for i in range(g):
        out = lax.dynamic_update_slice(out, w[i], (i * d, i * d))
    return out


def kernel(x, p, positions, norm_mix, w_in, ret_gn, pool_w, pool_scale, conv_w, conv_b, b_igate, b_fgate,
           mlstm_gn, w_out, norm_ffn, w_gate_up, w_down, norm_ple, w_ple_gate, w_ple_proj, norm_final):
    batch, seq, d = x.shape
    depth = w_in.shape[0]
    t = batch * seq
    xf = x.reshape(t, d)
    cos_t, sin_t = _trig_tables(positions.astype(F32).reshape(t, 1))
    tables = _retention_tables()
    row = lambda v: v.reshape(1, -1)
    for i in range(depth):
        w_in_p = _pad_in_proj(w_in[i]).astype(BF16)
        z = _inproj(xf, row(norm_mix[i]), w_in_p)
        convw = jnp.concatenate([_pad_heads(conv_w[i][:, :MLSTM_WIDTH]), _pad_heads(conv_w[i][:, MLSTM_WIDTH:])], axis=1)
        convb = row(jnp.concatenate([_pad_heads(conv_b[i][:MLSTM_WIDTH]), _pad_heads(conv_b[i][MLSTM_WIDTH:])]))
        zeros4 = jnp.zeros((8 - MLSTM_HEADS,), F32)
        gbias = jnp.broadcast_to(
            jnp.concatenate([b_igate[i], zeros4, b_fgate[i], zeros4])[:, None], (16, LANES))
        mix = _mixers(z, cos_t, sin_t, tables, row(ret_gn[i]), _block_diag(pool_w[i]).astype(BF16),
                      row(pool_scale[i]), convw, convb, gbias, row(_pad_heads(mlstm_gn[i])), batch, seq)
        xf = _channel(xf, mix, p[i].reshape(t, PLE_DIM), _pad_out_proj(w_out[i]).astype(BF16), row(norm_ffn[i]),
                      w_gate_up[i].astype(BF16), w_down[i].astype(BF16), row(norm_ple[i]),
                      w_ple_gate[i].astype(BF16), w_ple_proj[i].astype(BF16), row(norm_final),
                      final=(i == depth - 1))
    return xf.reshape(batch, seq, d)
```

```python
import functools

import numpy as np
import jax
import jax.numpy as jnp
from jax import lax
from jax.experimental import pallas as pl
from jax.experimental.pallas import tpu as pltpu

F32 = jnp.float32
BF16 = jnp.bfloat16

D_MODEL = 1024
PLE_DIM = 256
RET_HEADS = 6
RET_HEAD_DIM = 64
RET_WIDTH = RET_HEADS * RET_HEAD_DIM
RET_PAIRS = RET_HEADS // 2
POOL_WINDOWS = (2, 4, 8, 16)
POOL_GROUP_DIM = 64
POOL_WIDTH = len(POOL_WINDOWS) * POOL_GROUP_DIM
MLSTM_HEADS = 4
MLSTM_HEAD_DIM = 96
MLSTM_WIDTH = MLSTM_HEADS * MLSTM_HEAD_DIM
MLSTM_CONV = 4
CHUNK = 128
D_FF = 2816
ROPE_BASE = 10000.0
EPS = 1e-6

LANES = 128
HEAD_PAD = LANES
MLSTM_WIDTH_P = MLSTM_HEADS * HEAD_PAD

RQ = 0
RK = RQ + RET_WIDTH
RV = RK + RET_WIDTH
RG = RV + RET_WIDTH
PU = RG + RET_WIDTH
MQ = PU + POOL_WIDTH
MK = MQ + MLSTM_WIDTH_P
MV = MK + MLSTM_WIDTH_P
MO = MV + MLSTM_WIDTH_P
GT = MO + MLSTM_WIDTH_P
Z_WIDTH = GT + LANES
HIST_COLS = MV - PU
HIST_ROWS = 16
MIX_WIDTH_P = RET_WIDTH + POOL_WIDTH + MLSTM_WIDTH_P
NEG = -1e30

ROW_TILE = 512
SEQ_TILE = 256
VMEM_LIMIT = 56 * 1024 * 1024

(R_NORM_MIX, R_RET_GN, R_POOL_SCALE, R_CONV_B, R_MLSTM_GN, R_GATE_BIAS, R_NORM_FFN, R_NORM_PLE,
 R_NORM_FINAL, R_CONV_W) = range(10)
N_ROWS = 16


def _rmsnorm(x, g):
    return x * lax.rsqrt(jnp.mean(x * x, axis=-1, keepdims=True) + EPS) * g


def _sigmoid(x):
    return 1.0 / (1.0 + jnp.exp(-x))


def _dot(a, b):
    return jnp.dot(a, b, preferred_element_type=F32)


def _dot_nt(a, b):
    return lax.dot_general(a, b, (((1,), (1,)), ((), ())), preferred_element_type=F32)


def _trig_kernel(pos_ref, inv_ref, sign_ref, cos_ref, sin_ref):
    ang = pos_ref[...] * inv_ref[...]
    cos_ref[...] = jnp.cos(ang)
    sin_ref[...] = jnp.sin(ang) * sign_ref[...]


def _trig_tables(pos_col):
    t = pos_col.shape[0]
    half = RET_HEAD_DIM // 2
    inv = ROPE_BASE ** (-jnp.arange(half, dtype=F32) / half)
    inv_row = jnp.tile(inv, LANES // half).reshape(1, LANES)
    sign = np.where((np.arange(LANES) % RET_HEAD_DIM) < half, -1.0, 1.0).astype(np.float32).reshape(1, LANES)
    tile = 2 * ROW_TILE
    row = pl.BlockSpec((1, LANES), lambda i: (0, 0))
    out = pl.BlockSpec((tile, LANES), lambda i: (i, 0))
    return pl.pallas_call(
        _trig_kernel,
        grid=(t // tile,),
        in_specs=[pl.BlockSpec((tile, 1), lambda i: (i, 0)), row, row],
        out_specs=[out, out],
        out_shape=[jax.ShapeDtypeStruct((t, LANES), F32)] * 2,
        compiler_params=pltpu.CompilerParams(dimension_semantics=("arbitrary",)),
        name="rope_tables",
    )(pos_col, inv_row, jnp.asarray(sign))


def _inproj_kernel(x_ref, rows_ref, w_ref, z_ref):
    h = _rmsnorm(x_ref[...], rows_ref[R_NORM_MIX:R_NORM_MIX + 1, :]).astype(BF16)
    z_ref[...] = _dot(h, w_ref[...])


def _layer_block(a, layer):
    return pl.BlockSpec((None,) + a.shape[1:], lambda *_: (layer,) + (0,) * (a.ndim - 1))


def _inproj(x, rows, w, layer):
    t = x.shape[0]
    return pl.pallas_call(
        _inproj_kernel,
        grid=(t // ROW_TILE,),
        in_specs=[
            pl.BlockSpec((ROW_TILE, D_MODEL), lambda i: (i, 0)),
            _layer_block(rows, layer),
            _layer_block(w, layer),
        ],
        out_specs=pl.BlockSpec((ROW_TILE, Z_WIDTH), lambda i: (i, 0)),
        out_shape=jax.ShapeDtypeStruct((t, Z_WIDTH), F32),
        compiler_params=pltpu.CompilerParams(
            dimension_semantics=("arbitrary",), vmem_limit_bytes=VMEM_LIMIT),
        name="in_projection",
    )(x, rows, w)


def _swap_halves(x, lo_half):
    return jnp.where(lo_half, pltpu.roll(x, LANES - RET_HEAD_DIM // 2, 1), pltpu.roll(x, RET_HEAD_DIM // 2, 1))


def _lane_scan(x, op, fill):
    lane = lax.broadcasted_iota(jnp.int32, x.shape, 1)
    sh = 1
    while sh < LANES:
        x = op(x, jnp.where(lane >= sh, pltpu.roll(x, sh, 1), fill))
        sh *= 2
    return x


def _mixer_kernel(z_ref, cos_ref, sin_ref, dtab_ref, xi_ref, zeta_ref, cd_ref, bd_ref,
                  rows_ref, wpool_ref,
                  mix_ref, hbuf, rstate, cstate, mstate, *, seq_tile):
    j = pl.program_id(1)
    vec = lambda r, n: rows_ref[r:r + 1, 0:n]

    @pl.when(j == 0)
    def _():
        hbuf[0:HIST_ROWS, :] = jnp.zeros((HIST_ROWS, HIST_COLS), F32)
        rstate[...] = jnp.zeros_like(rstate)
        cstate[...] = jnp.zeros_like(cstate)
        mstate[...] = jnp.zeros_like(mstate)

    hbuf[HIST_ROWS:, :] = z_ref[:, PU:MV]

    lane = lax.broadcasted_iota(jnp.int32, (CHUNK, LANES), 1)
    row_t = lax.broadcasted_iota(jnp.int32, (CHUNK, LANES), 0)
    head_a = lane < RET_HEAD_DIM
    lo_half = (lane % RET_HEAD_DIM) < (RET_HEAD_DIM // 2)
    causal = lane <= row_t
    lane_p = lax.broadcasted_iota(jnp.int32, (CHUNK, POOL_WIDTH), 1)
    row_p = lax.broadcasted_iota(jnp.int32, (CHUNK, POOL_WIDTH), 0)

    for c in range(seq_tile // CHUNK):
        r0 = c * CHUNK
        rows = pl.ds(r0, CHUNK)

        def hist(k, c0, c1):
            return hbuf[pl.ds(HIST_ROWS + r0 - k, CHUNK), c0:c1]

        u = hist(0, 0, POOL_WIDTH)
        s2 = u + hist(1, 0, POOL_WIDTH)
        s4 = s2 + hist(2, 0, POOL_WIDTH) + hist(3, 0, POOL_WIDTH)
        s8 = s4
        for k in range(4, 8):
            s8 = s8 + hist(k, 0, POOL_WIDTH)
        s16 = s8
        for k in range(8, 16):
            s16 = s16 + hist(k, 0, POOL_WIDTH)
        g0, g1, g2 = (lane_p < POOL_GROUP_DIM, lane_p < 2 * POOL_GROUP_DIM, lane_p < 3 * POOL_GROUP_DIM)
        wsum = jnp.where(g0, s2, jnp.where(g1, s4, jnp.where(g2, s8, s16)))
        width = jnp.where(g0, 2, jnp.where(g1, 4, jnp.where(g2, 8, 16)))
        tpos = row_p + (j * seq_tile + r0 + 1)
        count = jnp.minimum(tpos, width).astype(F32)
        pooled = wsum / count - u
        y_pool = _dot(pooled.astype(BF16), wpool_ref[...]) * vec(R_POOL_SCALE, POOL_WIDTH)
        mix_ref[rows, RET_WIDTH:RET_WIDTH + POOL_WIDTH] = y_pool.astype(BF16)

        cosv = cos_ref[rows, :]
        sinv = sin_ref[rows, :]
        for p in range(RET_PAIRS):
            cs = slice(p * LANES, (p + 1) * LANES)
            qp = z_ref[rows, RQ + p * LANES:RQ + (p + 1) * LANES]
            kp = z_ref[rows, RK + p * LANES:RK + (p + 1) * LANES]
            vp = z_ref[rows, RV + p * LANES:RV + (p + 1) * LANES]
            gp = z_ref[rows, RG + p * LANES:RG + (p + 1) * LANES]
            q = qp * cosv + _swap_halves(qp, lo_half) * sinv
            k = (kp * cosv + _swap_halves(kp, lo_half) * sinv) * (RET_HEAD_DIM ** -0.5)
            q2 = jnp.concatenate([jnp.where(head_a, q, 0.0), jnp.where(head_a, 0.0, q)], axis=0)
            scores = _dot_nt(q2.astype(BF16), k.astype(BF16)) * dtab_ref[p]
            pcat = jnp.concatenate([scores[:CHUNK], scores[CHUNK:]], axis=1)
            v2 = jnp.concatenate([jnp.where(head_a, vp, 0.0), jnp.where(head_a, 0.0, vp)], axis=0)
            y = _dot(pcat.astype(BF16), v2.astype(BF16))
            y = y + _dot((q * xi_ref[p]).astype(BF16), rstate[p].astype(BF16))
            kz_t = (k * zeta_ref[p]).T.astype(BF16)
            rstate[p] = rstate[p] * cd_ref[p] + _dot(kz_t, vp.astype(BF16)) * bd_ref[...]
            inv_d = 1.0 / RET_HEAD_DIM
            sa = jnp.sum(jnp.where(head_a, y, 0.0), axis=1, keepdims=True)
            sb = jnp.sum(jnp.where(head_a, 0.0, y), axis=1, keepdims=True)
            yc = y - jnp.where(head_a, sa, sb) * inv_d
            yc2 = yc * yc
            va = jnp.sum(jnp.where(head_a, yc2, 0.0), axis=1, keepdims=True)
            vb = jnp.sum(jnp.where(head_a, 0.0, yc2), axis=1, keepdims=True)
            var = jnp.where(head_a, va, vb) * inv_d
            yn = yc * lax.rsqrt(var + EPS) * rows_ref[R_RET_GN:R_RET_GN + 1, cs]
            mix_ref[rows, cs] = (gp * _sigmoid(gp) * yn).astype(BF16)

        conv = vec(R_CONV_B, 2 * MLSTM_WIDTH_P)
        for kk in range(MLSTM_CONV):
            conv = conv + vec(R_CONV_W + kk, 2 * MLSTM_WIDTH_P) * hist(MLSTM_CONV - 1 - kk, POOL_WIDTH, HIST_COLS)
        qk = conv * _sigmoid(conv)

        gates_t = (z_ref[rows, GT:GT + LANES] + vec(R_GATE_BIAS, LANES)).T
        li = gates_t[0:8]
        fpre = gates_t[8:16]
        lf = jnp.minimum(fpre, 0.0) - jnp.log(1.0 + jnp.exp(-jnp.abs(fpre)))
        bcum = _lane_scan(lf, jnp.add, 0.0)
        g = li - bcum
        cmax = _lane_scan(g, jnp.maximum, NEG)
        m_prev = mstate[...]
        big_g = jnp.maximum(m_prev, cmax)
        inter = jnp.exp(m_prev - big_g)
        emr = jnp.exp(-(bcum + big_g))
        g_last = jnp.broadcast_to(big_g[:, LANES - 1:LANES], (8, LANES))
        b_last = jnp.broadcast_to(bcum[:, LANES - 1:LANES], (8, LANES))
        wk = jnp.exp(g - g_last)
        mstate[...] = b_last + g_last
        cols = jnp.concatenate(
            [big_g, inter, emr, wk, jnp.zeros((CHUNK - 32, LANES), F32)], axis=0).T

        for h in range(MLSTM_HEADS):
            hs = slice(h * HEAD_PAD, (h + 1) * HEAD_PAD)
            qh = qk[:, h * HEAD_PAD:(h + 1) * HEAD_PAD] * (MLSTM_HEAD_DIM ** -0.5)
            kh = qk[:, MLSTM_WIDTH_P + h * HEAD_PAD:MLSTM_WIDTH_P + (h + 1) * HEAD_PAD]
            vh = z_ref[rows, MV + h * HEAD_PAD:MV + (h + 1) * HEAD_PAD]
            oh = z_ref[rows, MO + h * HEAD_PAD:MO + (h + 1) * HEAD_PAD]
            v_aug = jnp.where(lane == MLSTM_HEAD_DIM, 1.0, vh)
            g_col = cols[:, h:h + 1]
            inter_col = cols[:, 8 + h:9 + h]
            emr_col = cols[:, 16 + h:17 + h]
            wk_col = cols[:, 24 + h:25 + h]
            dmat = jnp.exp(jnp.where(causal, g[h:h + 1, :] - g_col, NEG))
            sc = _dot_nt(qh.astype(BF16), kh.astype(BF16)) * dmat
            lhs = jnp.concatenate([sc, qh * inter_col], axis=1).astype(BF16)
            rhs = jnp.concatenate([v_aug, cstate[h]], axis=0).astype(BF16)
            nd = _dot(lhs, rhs)
            den = nd[:, MLSTM_HEAD_DIM:MLSTM_HEAD_DIM + 1]
            hval = nd / jnp.maximum(jnp.abs(den), emr_col)
            real = lane < MLSTM_HEAD_DIM
            mu = jnp.sum(jnp.where(real, hval, 0.0), axis=1, keepdims=True) / MLSTM_HEAD_DIM
            hc = jnp.where(real, hval - mu, 0.0)
            var = jnp.sum(hc * hc, axis=1, keepdims=True) / MLSTM_HEAD_DIM
            hn = hc * lax.rsqrt(var + EPS) * rows_ref[R_MLSTM_GN:R_MLSTM_GN + 1, hs]
            c0 = RET_WIDTH + POOL_WIDTH + h * HEAD_PAD
            mix_ref[rows, c0:c0 + HEAD_PAD] = (_sigmoid(oh) * hn).astype(BF16)
            kw_t = (kh * wk_col).T.astype(BF16)
            s_old = cols[CHUNK - 1:CHUNK, 8 + h:9 + h]
            cstate[h] = cstate[h] * s_old + _dot(kw_t, v_aug.astype(BF16))

    hbuf[0:HIST_ROWS, :] = hbuf[seq_tile:seq_tile + HIST_ROWS, :]


def _retention_tables():
    lg = np.log1p(-(2.0 ** (-5.0 - np.arange(RET_HEADS, dtype=np.float64))))
    idx = np.arange(CHUNK, dtype=np.float64)
    rel = idx[:, None] - idx[None, :]
    head_of_lane = np.arange(LANES) // RET_HEAD_DIM
    dtab, xi, zeta, cd = [], [], [], []
    same = (head_of_lane[:, None] == head_of_lane[None, :])
    for p in range(RET_PAIRS):
        hl = lg[2 * p + head_of_lane]
        dtab.append(np.concatenate(
            [np.where(rel >= 0, np.exp(lg[2 * p + a] * np.maximum(rel, 0.0)), 0.0) for a in range(2)], axis=0))
        xi.append(np.exp(hl[None, :] * (idx[:, None] + 1.0)))
        zeta.append(np.exp(hl[None, :] * (CHUNK - 1 - idx[:, None])))
        cd.append(np.where(same, np.exp(hl * CHUNK)[:, None], 0.0))
    f = lambda a: jnp.asarray(np.stack(a).astype(np.float32))
    return f(dtab), f(xi), f(zeta), f(cd), jnp.asarray(same.astype(np.float32))


def _mixers(z, cos_t, sin_t, tables, rows, wpool, layer, batch, seq):
    t = z.shape[0]
    n_tiles = seq // SEQ_TILE
    tile = lambda w: pl.BlockSpec((SEQ_TILE, w), lambda b, j: (b * n_tiles + j, 0))
    full = lambda a: pl.BlockSpec(a.shape, lambda b, j: (0,) * a.ndim)
    consts = tuple(tables) + (rows, wpool)
    return pl.pallas_call(
        functools.partial(_mixer_kernel, seq_tile=SEQ_TILE),
        grid=(batch, n_tiles),
        in_specs=[tile(Z_WIDTH), tile(LANES), tile(LANES)] + [full(a) for a in tables]
        + [_layer_block(rows, layer), _layer_block(wpool, layer)],
        out_specs=tile(MIX_WIDTH_P),
        out_shape=jax.ShapeDtypeStruct((t, MIX_WIDTH_P), BF16),
        scratch_shapes=[
            pltpu.VMEM((SEQ_TILE + HIST_ROWS, HIST_COLS), F32),
            pltpu.VMEM((RET_PAIRS, LANES, LANES), F32),
            pltpu.VMEM((MLSTM_HEADS, LANES, LANES), F32),
            pltpu.VMEM((8, LANES), F32),
        ],
        compiler_params=pltpu.CompilerParams(
            dimension_semantics=("arbitrary", "arbitrary"), vmem_limit_bytes=VMEM_LIMIT),
        name="token_mixers",
    )(z, cos_t, sin_t, *consts)


FF_CHUNK = 256


def _channel_kernel(x_ref, mix_ref, p_ref, rows_ref, wout_ref, wgu_ref, wdown_ref,
                    wpg_ref, wpp_ref, o_ref, *, final):
    vec = lambda r: rows_ref[r:r + 1, :]
    x1 = x_ref[...] + _dot(mix_ref[...], wout_ref[...])
    h = _rmsnorm(x1, vec(R_NORM_FFN)).astype(BF16)
    acc = x1
    for c in range(D_FF // FF_CHUNK):
        gate = _dot(h, wgu_ref[:, c * FF_CHUNK:(c + 1) * FF_CHUNK])
        up = _dot(h, wgu_ref[:, D_FF + c * FF_CHUNK:D_FF + (c + 1) * FF_CHUNK])
        act = (gate * _sigmoid(gate) * up).astype(BF16)
        acc = acc + _dot(act, wdown_ref[c * FF_CHUNK:(c + 1) * FF_CHUNK, :])
    hp = _rmsnorm(acc, vec(R_NORM_PLE)).astype(BF16)
    emb = _dot(p_ref[...].astype(BF16), wpp_ref[...])
    x3 = acc + _sigmoid(_dot(hp, wpg_ref[...])) * emb
    if final:
        x3 = _rmsnorm(x3, vec(R_NORM_FINAL))
    o_ref[...] = x3


def _channel(x, mix, p, rows, wout, wgu, wdown, wpg, wpp, layer, final):
    t = x.shape[0]
    tile = lambda w: pl.BlockSpec((ROW_TILE, w), lambda i: (i, 0))
    resident = lambda a: pl.BlockSpec((None,) + a.shape[1:], lambda i: (layer,) + (0,) * (a.ndim - 1),
                                      pipeline_mode=pl.Buffered(1))
    consts = (rows, wout, wgu, wdown, wpg, wpp)
    return pl.pallas_call(
        functools.partial(_channel_kernel, final=final),
        grid=(t // ROW_TILE,),
        in_specs=[tile(D_MODEL), tile(MIX_WIDTH_P),
                  pl.BlockSpec((None, ROW_TILE, PLE_DIM), lambda i: (layer, i, 0))]
        + [resident(a) for a in consts],
        out_specs=tile(D_MODEL),
        out_shape=jax.ShapeDtypeStruct((t, D_MODEL), F32),
        compiler_params=pltpu.CompilerParams(
            dimension_semantics=("arbitrary",), vmem_limit_bytes=VMEM_LIMIT),
        name="channel_mixing",
    )(x, mix, p, *consts)


def _pad_heads(w):
    lead = w.shape[:-1]
    w = w.reshape(lead + (MLSTM_HEADS, MLSTM_HEAD_DIM))
    w = jnp.pad(w, [(0, 0)] * len(lead) + [(0, 0), (0, HEAD_PAD - MLSTM_HEAD_DIM)])
    return w.reshape(lead + (MLSTM_WIDTH_P,))


def _pad_in_proj(w):
    o = 4 * RET_WIDTH + POOL_WIDTH
    parts = [w[..., :o]]
    for _ in range(4):
        parts.append(_pad_heads(w[..., o:o + MLSTM_WIDTH]))
        o += MLSTM_WIDTH
    zeros = lambda n: jnp.zeros(w.shape[:-1] + (n,), w.dtype)
    parts += [w[..., o:o + MLSTM_HEADS], zeros(8 - MLSTM_HEADS),
              w[..., o + MLSTM_HEADS:o + 2 * MLSTM_HEADS], zeros(LANES - 8 - MLSTM_HEADS)]
    return jnp.concatenate(parts, axis=-1)


def _pad_out_proj(w):
    o = RET_WIDTH + POOL_WIDTH
    depth, _, d = w.shape
    tail = w[:, o:].reshape(depth, MLSTM_HEADS, MLSTM_HEAD_DIM, d)
    tail = jnp.pad(tail, [(0, 0), (0, 0), (0, HEAD_PAD - MLSTM_HEAD_DIM), (0, 0)]).reshape(depth, MLSTM_WIDTH_P, d)
    return jnp.concatenate([w[:, :o], tail], axis=1)


def _block_diag(w):
    depth, g, d, _ = w.shape
    eye = jnp.asarray(np.eye(g, dtype=np.float32))
    return (w[:, :, :, None, :] * eye[None, :, None, :, None]).reshape(depth, g * d, g * d)


def _pack_rows(depth, vectors):
    rows = [jnp.pad(v, [(0, 0), (0, D_MODEL - v.shape[-1])]) for v in vectors]
    rows += [jnp.zeros((depth, D_MODEL), F32)] * (N_ROWS - len(rows))
    return jnp.stack(rows, axis=1)


def kernel(x, p, positions, norm_mix, w_in, ret_gn, pool_w, pool_scale, conv_w, conv_b, b_igate, b_fgate,
           mlstm_gn, w_out, norm_ffn, w_gate_up, w_down, norm_ple, w_ple_gate, w_ple_proj, norm_final):
    batch, seq, d = x.shape
    depth = w_in.shape[0]
    t = batch * seq
    xf = x.reshape(t, d)
    cos_t, sin_t = _trig_tables(positions.astype(F32).reshape(t, 1))
    tables = _retention_tables()

    pad_qk = lambda v: jnp.concatenate([_pad_heads(v[..., :MLSTM_WIDTH]), _pad_heads(v[..., MLSTM_WIDTH:])], axis=-1)
    zeros4 = jnp.zeros((depth, 8 - MLSTM_HEADS), F32)
    convw = pad_qk(conv_w)
    rows = _pack_rows(depth, [
        norm_mix, ret_gn, pool_scale, pad_qk(conv_b), _pad_heads(mlstm_gn),
        jnp.concatenate([b_igate, zeros4, b_fgate, zeros4], axis=-1),
        norm_ffn, norm_ple, jnp.broadcast_to(norm_final, (depth, d)),
    ] + [convw[:, k] for k in range(MLSTM_CONV)])
    w_in_p = _pad_in_proj(w_in).astype(BF16)
    w_out_p = _pad_out_proj(w_out).astype(BF16)
    wpool = _block_diag(pool_w).astype(BF16)
    wgu, wdown = w_gate_up.astype(BF16), w_down.astype(BF16)
    wpg, wpp = w_ple_gate.astype(BF16), w_ple_proj.astype(BF16)
    pf = p.reshape(depth, t, PLE_DIM)

    for i in range(depth):
        z = _inproj(xf, rows, w_in_p, i)
        mix = _mixers(z, cos_t, sin_t, tables, rows, wpool, i, batch, seq)
        xf = _channel(xf, mix, pf, rows, w_out_p, wgu, wdown, wpg, wpp, i, final=(i == depth - 1))
    return xf.reshape(batch, seq, d)
```

```python
import functools

import numpy as np
import jax
import jax.numpy as jnp
from jax import lax
from jax.experimental import pallas as pl
from jax.experimental.pallas import tpu as pltpu

F32 = jnp.float32
BF16 = jnp.bfloat16

D_MODEL = 1024
PLE_DIM = 256
RET_HEADS = 6
RET_HEAD_DIM = 64
RET_WIDTH = RET_HEADS * RET_HEAD_DIM
RET_PAIRS = RET_HEADS // 2
POOL_WINDOWS = (2, 4, 8, 16)
POOL_GROUP_DIM = 64
POOL_WIDTH = len(POOL_WINDOWS) * POOL_GROUP_DIM
MLSTM_HEADS = 4
MLSTM_HEAD_DIM = 96
MLSTM_WIDTH = MLSTM_HEADS * MLSTM_HEAD_DIM
MLSTM_CONV = 4
CHUNK = 128
D_FF = 2816
ROPE_BASE = 10000.0
EPS = 1e-6

LANES = 128
HEAD_PAD = LANES
MLSTM_WIDTH_P = MLSTM_HEADS * HEAD_PAD

RQ = 0
RK = RQ + RET_WIDTH
RV = RK + RET_WIDTH
RG = RV + RET_WIDTH
PU = RG + RET_WIDTH
MQ = PU + POOL_WIDTH
MK = MQ + MLSTM_WIDTH_P
MV = MK + MLSTM_WIDTH_P
MO = MV + MLSTM_WIDTH_P
GT = MO + MLSTM_WIDTH_P
Z_WIDTH = GT + LANES
HIST_COLS = MV - PU
HIST_ROWS = 16
MIX_WIDTH_P = RET_WIDTH + POOL_WIDTH + MLSTM_WIDTH_P
NEG = -1e30

ROW_TILE = 512
SEQ_TILE = 256
VMEM_LIMIT = 56 * 1024 * 1024

(R_NORM_MIX, R_RET_GN, R_POOL_SCALE, R_CONV_B, R_MLSTM_GN, R_GATE_BIAS, R_NORM_FFN, R_NORM_PLE,
 R_NORM_FINAL, R_CONV_W) = range(10)
N_ROWS = 16


def _rmsnorm(x, g):
    return x * lax.rsqrt(jnp.mean(x * x, axis=-1, keepdims=True) + EPS) * g


def _sigmoid(x):
    return 1.0 / (1.0 + jnp.exp(-x))


def _dot(a, b):
    return jnp.dot(a, b, preferred_element_type=F32)


def _dot_nt(a, b):
    return lax.dot_general(a, b, (((1,), (1,)), ((), ())), preferred_element_type=F32)


def _trig_kernel(pos_ref, inv_ref, sign_ref, cos_ref, sin_ref):
    ang = pos_ref[...] * inv_ref[...]
    cos_ref[...] = jnp.cos(ang)
    sin_ref[...] = jnp.sin(ang) * sign_ref[...]


def _trig_tables(pos_col):
    t = pos_col.shape[0]
    half = RET_HEAD_DIM // 2
    inv = ROPE_BASE ** (-jnp.arange(half, dtype=F32) / half)
    inv_row = jnp.tile(inv, LANES // half).reshape(1, LANES)
    sign = np.where((np.arange(LANES) % RET_HEAD_DIM) < half, -1.0, 1.0).astype(np.float32).reshape(1, LANES)
    tile = 2 * ROW_TILE
    row = pl.BlockSpec((1, LANES), lambda i: (0, 0))
    out = pl.BlockSpec((tile, LANES), lambda i: (i, 0))
    return pl.pallas_call(
        _trig_kernel,
        grid=(t // tile,),
        in_specs=[pl.BlockSpec((tile, 1), lambda i: (i, 0)), row, row],
        out_specs=[out, out],
        out_shape=[jax.ShapeDtypeStruct((t, LANES), F32)] * 2,
        compiler_params=pltpu.CompilerParams(dimension_semantics=("arbitrary",)),
        name="rope_tables",
    )(pos_col, inv_row, jnp.asarray(sign))


def _inproj_kernel(x_ref, rows_ref, w_ref, z_ref):
    h = _rmsnorm(x_ref[...], rows_ref[R_NORM_MIX:R_NORM_MIX + 1, :]).astype(BF16)
    z_ref[...] = _dot(h, w_ref[...])


def _layer_block(a, layer):
    return pl.BlockSpec((None,) + a.shape[1:], lambda *_: (layer,) + (0,) * (a.ndim - 1))


def _inproj(x, rows, w, layer):
    t = x.shape[0]
    return pl.pallas_call(
        _inproj_kernel,
        grid=(t // ROW_TILE,),
        in_specs=[
            pl.BlockSpec((ROW_TILE, D_MODEL), lambda i: (i, 0)),
            _layer_block(rows, layer),
            _layer_block(w, layer),
        ],
        out_specs=pl.BlockSpec((ROW_TILE, Z_WIDTH), lambda i: (i, 0)),
        out_shape=jax.ShapeDtypeStruct((t, Z_WIDTH), F32),
        compiler_params=pltpu.CompilerParams(
            dimension_semantics=("arbitrary",), vmem_limit_bytes=VMEM_LIMIT),
        name="in_projection",
    )(x, rows, w)


def _swap_halves(x, lo_half):
    return jnp.where(lo_half, pltpu.roll(x, LANES - RET_HEAD_DIM // 2, 1), pltpu.roll(x, RET_HEAD_DIM // 2, 1))


def _lane_scan(x, op, fill):
    lane = lax.broadcasted_iota(jnp.int32, x.shape, 1)
    sh = 1
    while sh < LANES:
        x = op(x, jnp.where(lane >= sh, pltpu.roll(x, sh, 1), fill))
        sh *= 2
    return x


def _mixer_kernel(z_ref, cos_ref, sin_ref, dtab_ref, xi_ref, zeta_ref, cd_ref, bd_ref, avg_ref,
                  rows_ref, wpool_ref,
                  mix_ref, hbuf, rstate, cstate, mstate, *, seq_tile):
    j = pl.program_id(1)
    vec = lambda r, n: rows_ref[r:r + 1, 0:n]

    @pl.when(j == 0)
    def _():
        hbuf[0:HIST_ROWS, :] = jnp.zeros((HIST_ROWS, HIST_COLS), F32)
        rstate[...] = jnp.zeros_like(rstate)
        cstate[...] = jnp.zeros_like(cstate)
        mstate[...] = jnp.zeros_like(mstate)

    hbuf[HIST_ROWS:, :] = z_ref[:, PU:MV]

    lane = lax.broadcasted_iota(jnp.int32, (CHUNK, LANES), 1)
    row_i = lax.broadcasted_iota(jnp.int32, (CHUNK, LANES), 0)
    head_a = lane < RET_HEAD_DIM
    lo_half = (lane % RET_HEAD_DIM) < (RET_HEAD_DIM // 2)
    key_le_query = row_i <= lane
    lane_p = lax.broadcasted_iota(jnp.int32, (CHUNK, POOL_WIDTH), 1)
    row_p = lax.broadcasted_iota(jnp.int32, (CHUNK, POOL_WIDTH), 0)

    for c in range(seq_tile // CHUNK):
        r0 = c * CHUNK
        rows = pl.ds(r0, CHUNK)

        ext = hbuf[r0:r0 + HIST_ROWS + CHUNK, 0:POOL_WIDTH]
        s2 = ext + pltpu.roll(ext, 1, 0)
        s4 = s2 + pltpu.roll(s2, 2, 0)
        s8 = s4 + pltpu.roll(s4, 4, 0)
        s16 = s8 + pltpu.roll(s8, 8, 0)
        u, s2, s4, s8, s16 = (a[HIST_ROWS:] for a in (ext, s2, s4, s8, s16))
        g0, g1, g2 = (lane_p < POOL_GROUP_DIM, lane_p < 2 * POOL_GROUP_DIM, lane_p < 3 * POOL_GROUP_DIM)
        wsum = jnp.where(g0, s2, jnp.where(g1, s4, jnp.where(g2, s8, s16)))
        width = jnp.where(g0, 2, jnp.where(g1, 4, jnp.where(g2, 8, 16)))
        tpos = row_p + (j * seq_tile + r0 + 1)
        count = jnp.minimum(tpos, width).astype(F32)
        pooled = wsum / count - u
        y_pool = _dot(pooled.astype(BF16), wpool_ref[...]) * vec(R_POOL_SCALE, POOL_WIDTH)
        mix_ref[rows, RET_WIDTH:RET_WIDTH + POOL_WIDTH] = y_pool.astype(BF16)

        cosv = cos_ref[rows, :]
        sinv = sin_ref[rows, :]
        ys = []
        for p in range(RET_PAIRS):
            qp = z_ref[rows, RQ + p * LANES:RQ + (p + 1) * LANES]
            kp = z_ref[rows, RK + p * LANES:RK + (p + 1) * LANES]
            vp = z_ref[rows, RV + p * LANES:RV + (p + 1) * LANES]
            q = qp * cosv + _swap_halves(qp, lo_half) * sinv
            k = (kp * cosv + _swap_halves(kp, lo_half) * sinv) * (RET_HEAD_DIM ** -0.5)
            k2 = jnp.concatenate([jnp.where(head_a, k, 0.0), jnp.where(head_a, 0.0, k)], axis=0).astype(BF16)
            v2 = jnp.concatenate([jnp.where(head_a, vp, 0.0), jnp.where(head_a, 0.0, vp)], axis=0)
            probs = _dot_nt(q.astype(BF16), k2) * dtab_ref[p]
            lhs = jnp.concatenate([probs, q * xi_ref[p]], axis=1).astype(BF16)
            rhs = jnp.concatenate([v2, rstate[p]], axis=0).astype(BF16)
            ys.append(_dot(lhs, rhs))
            kz_t = (k * zeta_ref[p]).T.astype(BF16)
            rstate[p] = rstate[p] * cd_ref[p] + _dot(kz_t, vp.astype(BF16)) * bd_ref[...]
        y = jnp.concatenate(ys, axis=0)
        yc = y - _dot(y.astype(BF16), avg_ref[...])
        var = _dot((yc * yc).astype(BF16), avg_ref[...])
        yn = yc * lax.rsqrt(var + EPS)
        for p in range(RET_PAIRS):
            cs = slice(p * LANES, (p + 1) * LANES)
            gp = z_ref[rows, RG + p * LANES:RG + (p + 1) * LANES]
            gain = rows_ref[R_RET_GN:R_RET_GN + 1, cs]
            mix_ref[rows, cs] = (gp * _sigmoid(gp) * (yn[p * CHUNK:(p + 1) * CHUNK] * gain)).astype(BF16)

        ext = hbuf[r0 + HIST_ROWS - 8:r0 + HIST_ROWS + CHUNK, POOL_WIDTH:HIST_COLS]
        conv = vec(R_CONV_B, 2 * MLSTM_WIDTH_P) + vec(R_CONV_W + MLSTM_CONV - 1, 2 * MLSTM_WIDTH_P) * ext[8:]
        for d in range(1, MLSTM_CONV):
            conv = conv + vec(R_CONV_W + MLSTM_CONV - 1 - d, 2 * MLSTM_WIDTH_P) * pltpu.roll(ext, d, 0)[8:]
        qk = conv * _sigmoid(conv)

        gates_t = (z_ref[rows, GT:GT + LANES] + vec(R_GATE_BIAS, LANES)).T
        li = gates_t[0:8]
        fpre = gates_t[8:16]
        lf = jnp.minimum(fpre, 0.0) - jnp.log(1.0 + jnp.exp(-jnp.abs(fpre)))
        bcum = _lane_scan(lf, jnp.add, 0.0)
        g = li - bcum
        cmax = _lane_scan(g, jnp.maximum, NEG)
        m_prev = mstate[...]
        big_g = jnp.maximum(m_prev, cmax)
        inter = jnp.exp(m_prev - big_g)
        emr = jnp.exp(-(bcum + big_g))
        g_last = jnp.broadcast_to(big_g[:, LANES - 1:LANES], (8, LANES))
        b_last = jnp.broadcast_to(bcum[:, LANES - 1:LANES], (8, LANES))
        wk = jnp.exp(g - g_last)
        s_old = jnp.exp(m_prev - g_last)
        mstate[...] = b_last + g_last
        g_cols = jnp.concatenate([g, jnp.zeros((CHUNK - 8, LANES), F32)], axis=0).T

        for h in range(MLSTM_HEADS):
            hs = slice(h * HEAD_PAD, (h + 1) * HEAD_PAD)
            qh = (qk[:, h * HEAD_PAD:(h + 1) * HEAD_PAD] * (MLSTM_HEAD_DIM ** -0.5)).astype(BF16)
            kh = qk[:, MLSTM_WIDTH_P + h * HEAD_PAD:MLSTM_WIDTH_P + (h + 1) * HEAD_PAD].astype(BF16)
            vh = z_ref[rows, MV + h * HEAD_PAD:MV + (h + 1) * HEAD_PAD]
            oh = z_ref[rows, MO + h * HEAD_PAD:MO + (h + 1) * HEAD_PAD]
            v_t = jnp.where(lane == MLSTM_HEAD_DIM, 1.0, vh).T
            decay_t = jnp.exp(jnp.where(key_le_query, g_cols[:, h:h + 1] - big_g[h:h + 1, :], NEG))
            sc_t = _dot_nt(kh, qh) * decay_t
            c_t = cstate[h]
            nd_t = _dot(v_t.astype(BF16), sc_t.astype(BF16)) + _dot_nt(c_t.astype(BF16), qh) * inter[h:h + 1, :]
            den = nd_t[MLSTM_HEAD_DIM:MLSTM_HEAD_DIM + 1, :]
            h_t = nd_t[0:MLSTM_HEAD_DIM, :] * (1.0 / jnp.maximum(jnp.abs(den), emr[h:h + 1, :]))
            mu = jnp.sum(h_t, axis=0, keepdims=True) / MLSTM_HEAD_DIM
            hc = h_t - mu
            var = jnp.sum(hc * hc, axis=0, keepdims=True) / MLSTM_HEAD_DIM
            hn_t = hc * lax.rsqrt(var + EPS)
            hn = jnp.concatenate([hn_t, jnp.zeros((HEAD_PAD - MLSTM_HEAD_DIM, LANES), F32)], axis=0).T
            c0 = RET_WIDTH + POOL_WIDTH + h * HEAD_PAD
            mix_ref[rows, c0:c0 + HEAD_PAD] = (
                _sigmoid(oh) * (hn * rows_ref[R_MLSTM_GN:R_MLSTM_GN + 1, hs])).astype(BF16)
            cstate[h] = c_t * s_old[h:h + 1, :] + _dot((v_t * wk[h:h + 1, :]).astype(BF16), kh)

    hbuf[0:HIST_ROWS, :] = hbuf[seq_tile:seq_tile + HIST_ROWS, :]


def _retention_tables():
    lg = np.log1p(-(2.0 ** (-5.0 - np.arange(RET_HEADS, dtype=np.float64))))
    idx = np.arange(CHUNK, dtype=np.float64)
    rel = idx[:, None] - idx[None, :]
    head_of_lane = np.arange(LANES) // RET_HEAD_DIM
    dtab, xi, zeta, cd = [], [], [], []
    same = (head_of_lane[:, None] == head_of_lane[None, :])
    for p in range(RET_PAIRS):
        hl = lg[2 * p + head_of_lane]
        dtab.append(np.concatenate(
            [np.where(rel >= 0, np.exp(lg[2 * p + a] * np.maximum(rel, 0.0)), 0.0) for a in range(2)], axis=1))
        xi.append(np.exp(hl[None, :] * (idx[:, None] + 1.0)))
        zeta.append(np.exp(hl[None, :] * (CHUNK - 1 - idx[:, None])))
        cd.append(np.where(same, np.exp(hl * CHUNK)[:, None], 0.0))
    f = lambda a: jnp.asarray(np.stack(a).astype(np.float32))
    avg = jnp.asarray((same / RET_HEAD_DIM).astype(np.float32)).astype(BF16)
    return f(dtab), f(xi), f(zeta), f(cd), jnp.asarray(same.astype(np.float32)), avg


def _mixers(z, cos_t, sin_t, tables, rows, wpool, layer, batch, seq):
    t = z.shape[0]
    n_tiles = seq // SEQ_TILE
    tile = lambda w: pl.BlockSpec((SEQ_TILE, w), lambda b, j: (b * n_tiles + j, 0))
    full = lambda a: pl.BlockSpec(a.shape, lambda b, j: (0,) * a.ndim)
    consts = tuple(tables) + (rows, wpool)
    return pl.pallas_call(
        functools.partial(_mixer_kernel, seq_tile=SEQ_TILE),
        grid=(batch, n_tiles),
        in_specs=[tile(Z_WIDTH), tile(LANES), tile(LANES)] + [full(a) for a in tables]
        + [_layer_block(rows, layer), _layer_block(wpool, layer)],
        out_specs=tile(MIX_WIDTH_P),
        out_shape=jax.ShapeDtypeStruct((t, MIX_WIDTH_P), BF16),
        scratch_shapes=[
            pltpu.VMEM((SEQ_TILE + HIST_ROWS, HIST_COLS), F32),
            pltpu.VMEM((RET_PAIRS, LANES, LANES), F32),
            pltpu.VMEM((MLSTM_HEADS, LANES, LANES), F32),
            pltpu.VMEM((8, LANES), F32),
        ],
        compiler_params=pltpu.CompilerParams(
            dimension_semantics=("arbitrary", "arbitrary"), vmem_limit_bytes=VMEM_LIMIT),
        name="token_mixers",
    )(z, cos_t, sin_t, *consts)


FF_CHUNK = 256


def _channel_kernel(x_ref, mix_ref, p_ref, rows_ref, wout_ref, wgu_ref, wdown_ref,
                    wpg_ref, wpp_ref, o_ref, *, final):
    vec = lambda r: rows_ref[r:r + 1, :]
    x1 = x_ref[...] + _dot(mix_ref[...], wout_ref[...])
    h = _rmsnorm(x1, vec(R_NORM_FFN)).astype(BF16)
    acc = x1
    for c in range(D_FF // FF_CHUNK):
        gate = _dot(h, wgu_ref[:, c * FF_CHUNK:(c + 1) * FF_CHUNK])
        up = _dot(h, wgu_ref[:, D_FF + c * FF_CHUNK:D_FF + (c + 1) * FF_CHUNK])
        act = (gate * _sigmoid(gate) * up).astype(BF16)
        acc = acc + _dot(act, wdown_ref[c * FF_CHUNK:(c + 1) * FF_CHUNK, :])
    hp = _rmsnorm(acc, vec(R_NORM_PLE)).astype(BF16)
    emb = _dot(p_ref[...].astype(BF16), wpp_ref[...])
    x3 = acc + _sigmoid(_dot(hp, wpg_ref[...])) * emb
    if final:
        x3 = _rmsnorm(x3, vec(R_NORM_FINAL))
    o_ref[...] = x3


def _channel(x, mix, p, rows, wout, wgu, wdown, wpg, wpp, layer, final):
    t = x.shape[0]
    tile = lambda w: pl.BlockSpec((ROW_TILE, w), lambda i: (i, 0))
    resident = lambda a: pl.BlockSpec((None,) + a.shape[1:], lambda i: (layer,) + (0,) * (a.ndim - 1),
                                      pipeline_mode=pl.Buffered(1))
    consts = (rows, wout, wgu, wdown, wpg, wpp)
    return pl.pallas_call(
        functools.partial(_channel_kernel, final=final),
        grid=(t // ROW_TILE,),
        in_specs=[tile(D_MODEL), tile(MIX_WIDTH_P),
                  pl.BlockSpec((None, ROW_TILE, PLE_DIM), lambda i: (layer, i, 0))]
        + [resident(a) for a in consts],
        out_specs=tile(D_MODEL),
        out_shape=jax.ShapeDtypeStruct((t, D_MODEL), F32),
        compiler_params=pltpu.CompilerParams(
            dimension_semantics=("arbitrary",), vmem_limit_bytes=VMEM_LIMIT),
        name="channel_mixing",
    )(x, mix, p, *consts)


def _pad_heads(w):
    lead = w.shape[:-1]
    w = w.reshape(lead + (MLSTM_HEADS, MLSTM_HEAD_DIM))
    w = jnp.pad(w, [(0, 0)] * len(lead) + [(0, 0), (0, HEAD_PAD - MLSTM_HEAD_DIM)])
    return w.reshape(lead + (MLSTM_WIDTH_P,))


def _pad_in_proj(w):
    o = 4 * RET_WIDTH + POOL_WIDTH
    parts = [w[..., :o]]
    for _ in range(4):
        parts.append(_pad_heads(w[..., o:o + MLSTM_WIDTH]))
        o += MLSTM_WIDTH
    zeros = lambda n: jnp.zeros(w.shape[:-1] + (n,), w.dtype)
    parts += [w[..., o:o + MLSTM_HEADS], zeros(8 - MLSTM_HEADS),
              w[..., o + MLSTM_HEADS:o + 2 * MLSTM_HEADS], zeros(LANES - 8 - MLSTM_HEADS)]
    return jnp.concatenate(parts, axis=-1)


def _pad_out_proj(w):
    o = RET_WIDTH + POOL_WIDTH
    depth, _, d = w.shape
    tail = w[:, o:].reshape(depth, MLSTM_HEADS, MLSTM_HEAD_DIM, d)
    tail = jnp.pad(tail, [(0, 0), (0, 0), (0, HEAD_PAD - MLSTM_HEAD_DIM), (0, 0)]).reshape(depth, MLSTM_WIDTH_P, d)
    return jnp.concatenate([w[:, :o], tail], axis=1)


def _block_diag(w):
    depth, g, d, _ = w.shape
    eye = jnp.asarray(np.eye(g, dtype=np.float32))
    return (w[:, :, :, None, :] * eye[None, :, None, :, None]).reshape(depth, g * d, g * d)


def _pack_rows(depth, vectors):
    rows = [jnp.pad(v, [(0, 0), (0, D_MODEL - v.shape[-1])]) for v in vectors]
    rows += [jnp.zeros((depth, D_MODEL), F32)] * (N_ROWS - len(rows))
    return jnp.stack(rows, axis=1)


def kernel(x, p, positions, norm_mix, w_in, ret_gn, pool_w, pool_scale, conv_w, conv_b, b_igate, b_fgate,
           mlstm_gn, w_out, norm_ffn, w_gate_up, w_down, norm_ple, w_ple_gate, w_ple_proj, norm_final):
    batch, seq, d = x.shape
    depth = w_in.shape[0]
    t = batch * seq
    xf = x.reshape(t, d)
    cos_t, sin_t = _trig_tables(positions.astype(F32).reshape(t, 1))
    tables = _retention_tables()

    pad_qk = lambda v: jnp.concatenate([_pad_heads(v[..., :MLSTM_WIDTH]), _pad_heads(v[..., MLSTM_WIDTH:])], axis=-1)
    zeros4 = jnp.zeros((depth, 8 - MLSTM_HEADS), F32)
    convw = pad_qk(conv_w)
    rows = _pack_rows(depth, [
        norm_mix, ret_gn, pool_scale, pad_qk(conv_b), _pad_heads(mlstm_gn),
        jnp.concatenate([b_igate, zeros4, b_fgate, zeros4], axis=-1),
        norm_ffn, norm_ple, jnp.broadcast_to(norm_final, (depth, d)),
    ] + [convw[:, k] for k in range(MLSTM_CONV)])
    w_in_p = _pad_in_proj(w_in).astype(BF16)
    w_out_p = _pad_out_proj(w_out).astype(BF16)
    wpool = _block_diag(pool_w).astype(BF16)
    wgu, wdown = w_gate_up.astype(BF16), w_down.astype(BF16)
    wpg, wpp = w_ple_gate.astype(BF16), w_ple_proj.astype(BF16)
    pf = p.reshape(depth, t, PLE_DIM)

    for i in range(depth):
        z = _inproj(xf, rows, w_in_p, i)
        mix = _mixers(z, cos_t, sin_t, tables, rows, wpool, i, batch, seq)
        xf = _channel(xf, mix, pf, rows, w_out_p, wgu, wdown, wpg, wpp, i, final=(i == depth - 1))
    return xf.reshape(batch, seq, d)
```

```python
import functools

import numpy as np
import jax
import jax.numpy as jnp
from jax import lax
from jax.experimental import pallas as pl
from jax.experimental.pallas import tpu as pltpu

F32 = jnp.float32
BF16 = jnp.bfloat16

D_MODEL = 1024
PLE_DIM = 256
RET_HEADS = 6
RET_HEAD_DIM = 64
RET_WIDTH = RET_HEADS * RET_HEAD_DIM
RET_PAIRS = RET_HEADS // 2
POOL_WINDOWS = (2, 4, 8, 16)
POOL_GROUP_DIM = 64
POOL_WIDTH = len(POOL_WINDOWS) * POOL_GROUP_DIM
MLSTM_HEADS = 4
MLSTM_HEAD_DIM = 96
MLSTM_WIDTH = MLSTM_HEADS * MLSTM_HEAD_DIM
MLSTM_CONV = 4
CHUNK = 128
D_FF = 2816
ROPE_BASE = 10000.0
EPS = 1e-6

LANES = 128
HEAD_PAD = LANES
MLSTM_WIDTH_P = MLSTM_HEADS * HEAD_PAD

RQ = 0
RK = RQ + RET_WIDTH
RV = RK + RET_WIDTH
RG = RV + RET_WIDTH
PU = RG + RET_WIDTH
MQ = PU + POOL_WIDTH
MK = MQ + MLSTM_WIDTH_P
MV = MK + MLSTM_WIDTH_P
MO = MV + MLSTM_WIDTH_P
GT = MO + MLSTM_WIDTH_P
Z_WIDTH = GT + LANES
HIST_COLS = MV - PU
HIST_ROWS = 16
MIX_WIDTH_P = RET_WIDTH + POOL_WIDTH + MLSTM_WIDTH_P
NEG = -1e30

ROW_TILE = 512
SEQ_TILE = 256
VMEM_LIMIT = 56 * 1024 * 1024

(R_NORM_MIX, R_RET_GN, R_POOL_SCALE, R_CONV_B, R_MLSTM_GN, R_GATE_BIAS, R_NORM_FFN, R_NORM_PLE,
 R_NORM_FINAL, R_CONV_W) = range(10)
N_ROWS = 16


def _rmsnorm(x, g):
    return x * lax.rsqrt(jnp.mean(x * x, axis=-1, keepdims=True) + EPS) * g


def _sigmoid(x):
    return 1.0 / (1.0 + jnp.exp(-x))


def _dot(a, b):
    return jnp.dot(a, b, preferred_element_type=F32)


def _dot_nt(a, b):
    return lax.dot_general(a, b, (((1,), (1,)), ((), ())), preferred_element_type=F32)


def _trig_kernel(pos_ref, inv_ref, sign_ref, cos_ref, sin_ref):
    ang = pos_ref[...] * inv_ref[...]
    cos_ref[...] = jnp.cos(ang)
    sin_ref[...] = jnp.sin(ang) * sign_ref[...]


def _trig_tables(pos_col):
    t = pos_col.shape[0]
    half = RET_HEAD_DIM // 2
    inv = ROPE_BASE ** (-jnp.arange(half, dtype=F32) / half)
    inv_row = jnp.tile(inv, LANES // half).reshape(1, LANES)
    sign = np.where((np.arange(LANES) % RET_HEAD_DIM) < half, -1.0, 1.0).astype(np.float32).reshape(1, LANES)
    tile = 2 * ROW_TILE
    row = pl.BlockSpec((1, LANES), lambda i: (0, 0))
    out = pl.BlockSpec((tile, LANES), lambda i: (i, 0))
    return pl.pallas_call(
        _trig_kernel,
        grid=(t // tile,),
        in_specs=[pl.BlockSpec((tile, 1), lambda i: (i, 0)), row, row],
        out_specs=[out, out],
        out_shape=[jax.ShapeDtypeStruct((t, LANES), F32)] * 2,
        compiler_params=pltpu.CompilerParams(dimension_semantics=("arbitrary",)),
        name="rope_tables",
    )(pos_col, inv_row, jnp.asarray(sign))


def _layer_block(a, layer):
    return pl.BlockSpec((None,) + a.shape[1:], lambda *_: (layer,) + (0,) * (a.ndim - 1))


def _swap_halves(x, lo_half):
    return jnp.where(lo_half, pltpu.roll(x, LANES - RET_HEAD_DIM // 2, 1), pltpu.roll(x, RET_HEAD_DIM // 2, 1))


def _lane_scan(x, op, fill):
    lane = lax.broadcasted_iota(jnp.int32, x.shape, 1)
    sh = 1
    while sh < LANES:
        x = op(x, jnp.where(lane >= sh, pltpu.roll(x, sh, 1), fill))
        sh *= 2
    return x


def _mixer_kernel(x_ref, cos_ref, sin_ref, dtab_ref, xi_ref, zeta_ref, cd_ref, bd_ref, avg_ref,
                  rows_ref, wpool_ref, win_ref,
                  mix_ref, z_even, z_odd, hbuf, rstate, cstate, mstate, *, seq_tile):
    j = pl.program_id(1)

    @pl.when(j == 0)
    def _():
        z_odd[...] = jnp.zeros_like(z_odd)

    @pl.when(j <= 1)
    def _():
        hbuf[0:HIST_ROWS, :] = jnp.zeros((HIST_ROWS, HIST_COLS), F32)
        rstate[...] = jnp.zeros_like(rstate)
        cstate[...] = jnp.zeros_like(cstate)
        mstate[...] = jnp.zeros_like(mstate)

    def step(z_next, z_ref):
        h = _rmsnorm(x_ref[...], rows_ref[R_NORM_MIX:R_NORM_MIX + 1, :]).astype(BF16)

        def project(c0, c1):
            z_next[:, c0:c1] = _dot(h, win_ref[:, c0:c1])

        slabs = [functools.partial(project, c0, min(c0 + PROJ_SLAB, Z_WIDTH)) for c0 in range(0, Z_WIDTH, PROJ_SLAB)]
        _mixer_tile(z_ref, cos_ref, sin_ref, dtab_ref, xi_ref, zeta_ref, cd_ref, bd_ref, avg_ref, rows_ref,
                    wpool_ref, mix_ref, hbuf, rstate, cstate, mstate, jnp.maximum(j - 1, 0), seq_tile,
                    _Interleave(slabs, (seq_tile // CHUNK) * STAGES_PER_CHUNK))

    @pl.when(j % 2 == 0)
    def _():
        step(z_even, z_odd)

    @pl.when(j % 2 == 1)
    def _():
        step(z_odd, z_even)


PROJ_SLAB = 256
STAGES_PER_CHUNK = 11


class _Interleave:
    def __init__(self, thunks, n_points):
        self.thunks, self.n_points, self.point, self.done = thunks, n_points, 0, 0

    def tick(self):
        self.point += 1
        due = len(self.thunks) if self.point >= self.n_points else (self.point * len(self.thunks)) // self.n_points
        while self.done < due:
            self.thunks[self.done]()
            self.done += 1


def _mixer_tile(z_ref, cos_ref, sin_ref, dtab_ref, xi_ref, zeta_ref, cd_ref, bd_ref, avg_ref, rows_ref,
                wpool_ref, mix_ref, hbuf, rstate, cstate, mstate, j, seq_tile, other_work):
    vec = lambda r, n: rows_ref[r:r + 1, 0:n]
    hbuf[HIST_ROWS:, :] = z_ref[:, PU:MV]

    lane = lax.broadcasted_iota(jnp.int32, (CHUNK, LANES), 1)
    row_i = lax.broadcasted_iota(jnp.int32, (CHUNK, LANES), 0)
    head_a = lane < RET_HEAD_DIM
    lo_half = (lane % RET_HEAD_DIM) < (RET_HEAD_DIM // 2)
    key_le_query = row_i <= lane
    lane_p = lax.broadcasted_iota(jnp.int32, (CHUNK, POOL_WIDTH), 1)
    row_p = lax.broadcasted_iota(jnp.int32, (CHUNK, POOL_WIDTH), 0)

    for c in range(seq_tile // CHUNK):
        r0 = c * CHUNK
        rows = pl.ds(r0, CHUNK)

        ext = hbuf[r0:r0 + HIST_ROWS + CHUNK, 0:POOL_WIDTH]
        s2 = ext + pltpu.roll(ext, 1, 0)
        s4 = s2 + pltpu.roll(s2, 2, 0)
        s8 = s4 + pltpu.roll(s4, 4, 0)
        s16 = s8 + pltpu.roll(s8, 8, 0)
        u, s2, s4, s8, s16 = (a[HIST_ROWS:] for a in (ext, s2, s4, s8, s16))
        g0, g1, g2 = (lane_p < POOL_GROUP_DIM, lane_p < 2 * POOL_GROUP_DIM, lane_p < 3 * POOL_GROUP_DIM)
        wsum = jnp.where(g0, s2, jnp.where(g1, s4, jnp.where(g2, s8, s16)))
        width = jnp.where(g0, 2, jnp.where(g1, 4, jnp.where(g2, 8, 16)))
        tpos = row_p + (j * seq_tile + r0 + 1)
        count = jnp.minimum(tpos, width).astype(F32)
        pooled = wsum / count - u
        y_pool = _dot(pooled.astype(BF16), wpool_ref[...]) * vec(R_POOL_SCALE, POOL_WIDTH)
        mix_ref[rows, RET_WIDTH:RET_WIDTH + POOL_WIDTH] = y_pool.astype(BF16)
        other_work.tick()

        cosv = cos_ref[rows, :]
        sinv = sin_ref[rows, :]
        ys = []
        for p in range(RET_PAIRS):
            qp = z_ref[rows, RQ + p * LANES:RQ + (p + 1) * LANES]
            kp = z_ref[rows, RK + p * LANES:RK + (p + 1) * LANES]
            vp = z_ref[rows, RV + p * LANES:RV + (p + 1) * LANES]
            q = qp * cosv + _swap_halves(qp, lo_half) * sinv
            k = (kp * cosv + _swap_halves(kp, lo_half) * sinv) * (RET_HEAD_DIM ** -0.5)
            k2 = jnp.concatenate([jnp.where(head_a, k, 0.0), jnp.where(head_a, 0.0, k)], axis=0).astype(BF16)
            v2 = jnp.concatenate([jnp.where(head_a, vp, 0.0), jnp.where(head_a, 0.0, vp)], axis=0)
            probs = _dot_nt(q.astype(BF16), k2) * dtab_ref[p]
            lhs = jnp.concatenate([probs, q * xi_ref[p]], axis=1).astype(BF16)
            rhs = jnp.concatenate([v2, rstate[p]], axis=0).astype(BF16)
            ys.append(_dot(lhs, rhs))
            kz_t = (k * zeta_ref[p]).T.astype(BF16)
            rstate[p] = rstate[p] * cd_ref[p] + _dot(kz_t, vp.astype(BF16)) * bd_ref[...]
            other_work.tick()
        y = jnp.concatenate(ys, axis=0)
        yc = y - _dot(y.astype(BF16), avg_ref[...])
        var = _dot((yc * yc).astype(BF16), avg_ref[...])
        yn = yc * lax.rsqrt(var + EPS)
        for p in range(RET_PAIRS):
            cs = slice(p * LANES, (p + 1) * LANES)
            gp = z_ref[rows, RG + p * LANES:RG + (p + 1) * LANES]
            gain = rows_ref[R_RET_GN:R_RET_GN + 1, cs]
            mix_ref[rows, cs] = (gp * _sigmoid(gp) * (yn[p * CHUNK:(p + 1) * CHUNK] * gain)).astype(BF16)
        other_work.tick()

        ext = hbuf[r0 + HIST_ROWS - 8:r0 + HIST_ROWS + CHUNK, POOL_WIDTH:HIST_COLS]
        conv = vec(R_CONV_B, 2 * MLSTM_WIDTH_P) + vec(R_CONV_W + MLSTM_CONV - 1, 2 * MLSTM_WIDTH_P) * ext[8:]
        for d in range(1, MLSTM_CONV):
            conv = conv + vec(R_CONV_W + MLSTM_CONV - 1 - d, 2 * MLSTM_WIDTH_P) * pltpu.roll(ext, d, 0)[8:]
        qk = conv * _sigmoid(conv)
        other_work.tick()

        gates_t = (z_ref[rows, GT:GT + LANES] + vec(R_GATE_BIAS, LANES)).T
        li = gates_t[0:8]
        fpre = gates_t[8:16]
        lf = jnp.minimum(fpre, 0.0) - jnp.log(1.0 + jnp.exp(-jnp.abs(fpre)))
        bcum = _lane_scan(lf, jnp.add, 0.0)
        g = li - bcum
        cmax = _lane_scan(g, jnp.maximum, NEG)
        m_prev = mstate[...]
        big_g = jnp.maximum(m_prev, cmax)
        inter = jnp.exp(m_prev - big_g)
        emr = jnp.exp(-(bcum + big_g))
        g_last = jnp.broadcast_to(big_g[:, LANES - 1:LANES], (8, LANES))
        b_last = jnp.broadcast_to(bcum[:, LANES - 1:LANES], (8, LANES))
        wk = jnp.exp(g - g_last)
        s_old = jnp.exp(m_prev - g_last)
        mstate[...] = b_last + g_last
        g_cols = jnp.concatenate([g, jnp.zeros((CHUNK - 8, LANES), F32)], axis=0).T
        other_work.tick()

        for h in range(MLSTM_HEADS):
            hs = slice(h * HEAD_PAD, (h + 1) * HEAD_PAD)
            qh = (qk[:, h * HEAD_PAD:(h + 1) * HEAD_PAD] * (MLSTM_HEAD_DIM ** -0.5)).astype(BF16)
            kh = qk[:, MLSTM_WIDTH_P + h * HEAD_PAD:MLSTM_WIDTH_P + (h + 1) * HEAD_PAD].astype(BF16)
            vh = z_ref[rows, MV + h * HEAD_PAD:MV + (h + 1) * HEAD_PAD]
            oh = z_ref[rows, MO + h * HEAD_PAD:MO + (h + 1) * HEAD_PAD]
            v_t = jnp.where(lane == MLSTM_HEAD_DIM, 1.0, vh).T
            decay_t = jnp.exp(jnp.where(key_le_query, g_cols[:, h:h + 1] - big_g[h:h + 1, :], NEG))
            sc_t = _dot_nt(kh, qh) * decay_t
            c_t = cstate[h]
            nd_t = _dot(v_t.astype(BF16), sc_t.astype(BF16)) + _dot_nt(c_t.astype(BF16), qh) * inter[h:h + 1, :]
            den = nd_t[MLSTM_HEAD_DIM:MLSTM_HEAD_DIM + 1, :]
            h_t = nd_t[0:MLSTM_HEAD_DIM, :] * (1.0 / jnp.maximum(jnp.abs(den), emr[h:h + 1, :]))
            mu = jnp.sum(h_t, axis=0, keepdims=True) / MLSTM_HEAD_DIM
            hc = h_t - mu
            var = jnp.sum(hc * hc, axis=0, keepdims=True) / MLSTM_HEAD_DIM
            hn_t = hc * lax.rsqrt(var + EPS)
            hn = jnp.concatenate([hn_t, jnp.zeros((HEAD_PAD - MLSTM_HEAD_DIM, LANES), F32)], axis=0).T
            c0 = RET_WIDTH + POOL_WIDTH + h * HEAD_PAD
            mix_ref[rows, c0:c0 + HEAD_PAD] = (
                _sigmoid(oh) * (hn * rows_ref[R_MLSTM_GN:R_MLSTM_GN + 1, hs])).astype(BF16)
            cstate[h] = c_t * s_old[h:h + 1, :] + _dot((v_t * wk[h:h + 1, :]).astype(BF16), kh)
            other_work.tick()

    hbuf[0:HIST_ROWS, :] = hbuf[seq_tile:seq_tile + HIST_ROWS, :]


def _retention_tables():
    lg = np.log1p(-(2.0 ** (-5.0 - np.arange(RET_HEADS, dtype=np.float64))))
    idx = np.arange(CHUNK, dtype=np.float64)
    rel = idx[:, None] - idx[None, :]
    head_of_lane = np.arange(LANES) // RET_HEAD_DIM
    dtab, xi, zeta, cd = [], [], [], []
    same = (head_of_lane[:, None] == head_of_lane[None, :])
    for p in range(RET_PAIRS):
        hl = lg[2 * p + head_of_lane]
        dtab.append(np.concatenate(
            [np.where(rel >= 0, np.exp(lg[2 * p + a] * np.maximum(rel, 0.0)), 0.0) for a in range(2)], axis=1))
        xi.append(np.exp(hl[None, :] * (idx[:, None] + 1.0)))
        zeta.append(np.exp(hl[None, :] * (CHUNK - 1 - idx[:, None])))
        cd.append(np.where(same, np.exp(hl * CHUNK)[:, None], 0.0))
    f = lambda a: jnp.asarray(np.stack(a).astype(np.float32))
    avg = jnp.asarray((same / RET_HEAD_DIM).astype(np.float32)).astype(BF16)
    return f(dtab), f(xi), f(zeta), f(cd), jnp.asarray(same.astype(np.float32)), avg


def _mixers(x, cos_t, sin_t, tables, rows, wpool, w_in, layer, batch, seq):
    t = x.shape[0]
    n_tiles = seq // SEQ_TILE
    ahead = lambda w: pl.BlockSpec((SEQ_TILE, w), lambda b, j: (b * n_tiles + jnp.minimum(j, n_tiles - 1), 0))
    tile = lambda w: pl.BlockSpec((SEQ_TILE, w), lambda b, j: (b * n_tiles + jnp.maximum(j - 1, 0), 0))
    full = lambda a: pl.BlockSpec(a.shape, lambda b, j: (0,) * a.ndim)
    resident = lambda a: pl.BlockSpec((None,) + a.shape[1:], lambda b, j: (layer,) + (0,) * (a.ndim - 1),
                                      pipeline_mode=pl.Buffered(1))
    consts = tuple(tables) + (rows, wpool, w_in)
    return pl.pallas_call(
        functools.partial(_mixer_kernel, seq_tile=SEQ_TILE),
        grid=(batch, n_tiles + 1),
        in_specs=[ahead(D_MODEL), tile(LANES), tile(LANES)] + [full(a) for a in tables]
        + [_layer_block(rows, layer), _layer_block(wpool, layer), resident(w_in)],
        out_specs=tile(MIX_WIDTH_P),
        out_shape=jax.ShapeDtypeStruct((t, MIX_WIDTH_P), BF16),
        scratch_shapes=[
            pltpu.VMEM((SEQ_TILE, Z_WIDTH), F32),
            pltpu.VMEM((SEQ_TILE, Z_WIDTH), F32),
            pltpu.VMEM((SEQ_TILE + HIST_ROWS, HIST_COLS), F32),
            pltpu.VMEM((RET_PAIRS, LANES, LANES), F32),
            pltpu.VMEM((MLSTM_HEADS, LANES, LANES), F32),
            pltpu.VMEM((8, LANES), F32),
        ],
        compiler_params=pltpu.CompilerParams(
            dimension_semantics=("arbitrary", "arbitrary"), vmem_limit_bytes=VMEM_LIMIT),
        name="token_mixers",
    )(x, cos_t, sin_t, *consts)


FF_CHUNK = 256


def _channel_kernel(x_ref, mix_ref, p_ref, rows_ref, wout_ref, wgu_ref, wdown_ref,
                    wpg_ref, wpp_ref, o_ref, *, final):
    vec = lambda r: rows_ref[r:r + 1, :]
    x1 = x_ref[...] + _dot(mix_ref[...], wout_ref[...])
    h = _rmsnorm(x1, vec(R_NORM_FFN)).astype(BF16)
    acc = x1
    for c in range(D_FF // FF_CHUNK):
        gate = _dot(h, wgu_ref[:, c * FF_CHUNK:(c + 1) * FF_CHUNK])
        up = _dot(h, wgu_ref[:, D_FF + c * FF_CHUNK:D_FF + (c + 1) * FF_CHUNK])
        act = (gate * _sigmoid(gate) * up).astype(BF16)
        acc = acc + _dot(act, wdown_ref[c * FF_CHUNK:(c + 1) * FF_CHUNK, :])
    hp = _rmsnorm(acc, vec(R_NORM_PLE)).astype(BF16)
    emb = _dot(p_ref[...].astype(BF16), wpp_ref[...])
    x3 = acc + _sigmoid(_dot(hp, wpg_ref[...])) * emb
    if final:
        x3 = _rmsnorm(x3, vec(R_NORM_FINAL))
    o_ref[...] = x3


def _channel(x, mix, p, rows, wout, wgu, wdown, wpg, wpp, layer, final):
    t = x.shape[0]
    tile = lambda w: pl.BlockSpec((ROW_TILE, w), lambda i: (i, 0))
    resident = lambda a: pl.BlockSpec((None,) + a.shape[1:], lambda i: (layer,) + (0,) * (a.ndim - 1),
                                      pipeline_mode=pl.Buffered(1))
    consts = (rows, wout, wgu, wdown, wpg, wpp)
    return pl.pallas_call(
        functools.partial(_channel_kernel, final=final),
        grid=(t // ROW_TILE,),
        in_specs=[tile(D_MODEL), tile(MIX_WIDTH_P),
                  pl.BlockSpec((None, ROW_TILE, PLE_DIM), lambda i: (layer, i, 0))]
        + [resident(a) for a in consts],
        out_specs=tile(D_MODEL),
        out_shape=jax.ShapeDtypeStruct((t, D_MODEL), F32),
        compiler_params=pltpu.CompilerParams(
            dimension_semantics=("arbitrary",), vmem_limit_bytes=VMEM_LIMIT),
        name="channel_mixing",
    )(x, mix, p, *consts)


def _pad_heads(w):
    lead = w.shape[:-1]
    w = w.reshape(lead + (MLSTM_HEADS, MLSTM_HEAD_DIM))
    w = jnp.pad(w, [(0, 0)] * len(lead) + [(0, 0), (0, HEAD_PAD - MLSTM_HEAD_DIM)])
    return w.reshape(lead + (MLSTM_WIDTH_P,))


def _pad_in_proj(w):
    o = 4 * RET_WIDTH + POOL_WIDTH
    parts = [w[..., :o]]
    for _ in range(4):
        parts.append(_pad_heads(w[..., o:o + MLSTM_WIDTH]))
        o += MLSTM_WIDTH
    zeros = lambda n: jnp.zeros(w.shape[:-1] + (n,), w.dtype)
    parts += [w[..., o:o + MLSTM_HEADS], zeros(8 - MLSTM_HEADS),
              w[..., o + MLSTM_HEADS:o + 2 * MLSTM_HEADS], zeros(LANES - 8 - MLSTM_HEADS)]
    return jnp.concatenate(parts, axis=-1)


def _pad_out_proj(w):
    o = RET_WIDTH + POOL_WIDTH
    depth, _, d = w.shape
    tail = w[:, o:].reshape(depth, MLSTM_HEADS, MLSTM_HEAD_DIM, d)
    tail = jnp.pad(tail, [(0, 0), (0, 0), (0, HEAD_PAD - MLSTM_HEAD_DIM), (0, 0)]).reshape(depth, MLSTM_WIDTH_P, d)
    return jnp.concatenate([w[:, :o], tail], axis=1)


def _block_diag(w):
    depth, g, d, _ = w.shape
    eye = jnp.asarray(np.eye(g, dtype=np.float32))
    return (w[:, :, :, None, :] * eye[None, :, None, :, None]).reshape(depth, g * d, g * d)


def _pack_rows(depth, vectors):
    rows = [jnp.pad(v, [(0, 0), (0, D_MODEL - v.shape[-1])]) for v in vectors]
    rows += [jnp.zeros((depth, D_MODEL), F32)] * (N_ROWS - len(rows))
    return jnp.stack(rows, axis=1)


def kernel(x, p, positions, norm_mix, w_in, ret_gn, pool_w, pool_scale, conv_w, conv_b, b_igate, b_fgate,
           mlstm_gn, w_out, norm_ffn, w_gate_up, w_down, norm_ple, w_ple_gate, w_ple_proj, norm_final):
    batch, seq, d = x.shape
    depth = w_in.shape[0]
    t = batch * seq
    xf = x.reshape(t, d)
    cos_t, sin_t = _trig_tables(positions.astype(F32).reshape(t, 1))
    tables = _retention_tables()

    pad_qk = lambda v: jnp.concatenate([_pad_heads(v[..., :MLSTM_WIDTH]), _pad_heads(v[..., MLSTM_WIDTH:])], axis=-1)
    zeros4 = jnp.zeros((depth, 8 - MLSTM_HEADS), F32)
    convw = pad_qk(conv_w)
    rows = _pack_rows(depth, [
        norm_mix, ret_gn, pool_scale, pad_qk(conv_b), _pad_heads(mlstm_gn),
        jnp.concatenate([b_igate, zeros4, b_fgate, zeros4], axis=-1),
        norm_ffn, norm_ple, jnp.broadcast_to(norm_final, (depth, d)),
    ] + [convw[:, k] for k in range(MLSTM_CONV)])
    w_in_p = _pad_in_proj(w_in).astype(BF16)
    w_out_p = _pad_out_proj(w_out).astype(BF16)
    wpool = _block_diag(pool_w).astype(BF16)
    wgu, wdown = w_gate_up.astype(BF16), w_down.astype(BF16)
    wpg, wpp = w_ple_gate.astype(BF16), w_ple_proj.astype(BF16)
    pf = p.reshape(depth, t, PLE_DIM)

    for i in range(depth):
        mix = _mixers(xf, cos_t, sin_t, tables, rows, wpool, w_in_p, i, batch, seq)
        xf = _channel(xf, mix, pf, rows, w_out_p, wgu, wdown, wpg, wpp, i, final=(i == depth - 1))
    return xf.reshape(batch, seq, d)
```

```python
import functools

import numpy as np
import jax
import jax.numpy as jnp
from jax import lax
from jax.experimental import pallas as pl
from jax.experimental.pallas import tpu as pltpu

F32 = jnp.float32
BF16 = jnp.bfloat16

D_MODEL = 1024
PLE_DIM = 256
RET_HEADS = 6
RET_HEAD_DIM = 64
RET_WIDTH = RET_HEADS * RET_HEAD_DIM
RET_PAIRS = RET_HEADS // 2
POOL_WINDOWS = (2, 4, 8, 16)
POOL_GROUP_DIM = 64
POOL_WIDTH = len(POOL_WINDOWS) * POOL_GROUP_DIM
MLSTM_HEADS = 4
MLSTM_HEAD_DIM = 96
MLSTM_WIDTH = MLSTM_HEADS * MLSTM_HEAD_DIM
MLSTM_CONV = 4
CHUNK = 128
D_FF = 2816
ROPE_BASE = 10000.0
EPS = 1e-6

LANES = 128
HEAD_PAD = LANES
MLSTM_WIDTH_P = MLSTM_HEADS * HEAD_PAD

RQ = 0
RK = RQ + RET_WIDTH
RV = RK + RET_WIDTH
RG = RV + RET_WIDTH
PU = RG + RET_WIDTH
MQ = PU + POOL_WIDTH
MK = MQ + MLSTM_WIDTH_P
MV = MK + MLSTM_WIDTH_P
MO = MV + MLSTM_WIDTH_P
GT = MO + MLSTM_WIDTH_P
Z_WIDTH = GT + LANES
HIST_COLS = MV - PU
HIST_ROWS = 16
MIX_WIDTH_P = RET_WIDTH + POOL_WIDTH + MLSTM_WIDTH_P
NEG = -1e30

ROW_TILE = 512
SEQ_TILE = 256
VMEM_LIMIT = 56 * 1024 * 1024

(R_NORM_MIX, R_RET_GN, R_POOL_SCALE, R_CONV_B, R_MLSTM_GN, R_GATE_BIAS, R_NORM_FFN, R_NORM_PLE,
 R_NORM_FINAL, R_CONV_W) = range(10)
N_ROWS = 16


def _rmsnorm(x, g):
    return x * lax.rsqrt(jnp.mean(x * x, axis=-1, keepdims=True) + EPS) * g


def _sigmoid(x):
    return 1.0 / (1.0 + jnp.exp(-x))


def _silu(x):
    half = 0.5 * x
    return half + half * jnp.tanh(half)


def _dot(a, b):
    return jnp.dot(a, b, preferred_element_type=F32)


def _dot_nt(a, b):
    return lax.dot_general(a, b, (((1,), (1,)), ((), ())), preferred_element_type=F32)


def _trig_kernel(pos_ref, inv_ref, sign_ref, cos_ref, sin_ref):
    ang = pos_ref[...] * inv_ref[...]
    cos_ref[...] = jnp.cos(ang)
    sin_ref[...] = jnp.sin(ang) * sign_ref[...]


def _trig_tables(pos_col):
    t = pos_col.shape[0]
    half = RET_HEAD_DIM // 2
    inv = ROPE_BASE ** (-jnp.arange(half, dtype=F32) / half)
    inv_row = jnp.tile(inv, LANES // half).reshape(1, LANES)
    sign = np.where((np.arange(LANES) % RET_HEAD_DIM) < half, -1.0, 1.0).astype(np.float32).reshape(1, LANES)
    tile = 2 * ROW_TILE
    row = pl.BlockSpec((1, LANES), lambda i: (0, 0))
    out = pl.BlockSpec((tile, LANES), lambda i: (i, 0))
    return pl.pallas_call(
        _trig_kernel,
        grid=(t // tile,),
        in_specs=[pl.BlockSpec((tile, 1), lambda i: (i, 0)), row, row],
        out_specs=[out, out],
        out_shape=[jax.ShapeDtypeStruct((t, LANES), F32)] * 2,
        compiler_params=pltpu.CompilerParams(dimension_semantics=("arbitrary",)),
        name="rope_tables",
    )(pos_col, inv_row, jnp.asarray(sign))


def _layer_block(a, layer):
    return pl.BlockSpec((None,) + a.shape[1:], lambda *_: (layer,) + (0,) * (a.ndim - 1))


def _swap_halves(x, lo_half):
    return jnp.where(lo_half, pltpu.roll(x, LANES - RET_HEAD_DIM // 2, 1), pltpu.roll(x, RET_HEAD_DIM // 2, 1))


def _lane_scan(x, op, fill):
    lane = lax.broadcasted_iota(jnp.int32, x.shape, 1)
    sh = 1
    while sh < LANES:
        x = op(x, jnp.where(lane >= sh, pltpu.roll(x, sh, 1), fill))
        sh *= 2
    return x


def _mixer_kernel(x_ref, cos_ref, sin_ref, dtab_ref, xi_ref, zeta_ref, cd_ref, bd_ref, avg_ref,
                  rows_ref, wpool_ref, wraw_ref,
                  mix_ref, win_ref, z_even, z_odd, hbuf, rstate, cstate, mstate, *, seq_tile):
    j = pl.program_id(1)

    @pl.when((pl.program_id(0) == 0) & (j == 0))
    def _():
        o = 4 * RET_WIDTH + POOL_WIDTH
        win_ref[:, 0:o] = wraw_ref[:, 0:o].astype(BF16)
        pad = jnp.zeros((D_MODEL, HEAD_PAD - MLSTM_HEAD_DIM), F32)
        for k in range(4 * MLSTM_HEADS):
            src = o + k * MLSTM_HEAD_DIM
            head = jnp.concatenate([wraw_ref[:, src:src + MLSTM_HEAD_DIM], pad], axis=1)
            win_ref[:, o + k * HEAD_PAD:o + (k + 1) * HEAD_PAD] = head.astype(BF16)
        src = o + 4 * MLSTM_WIDTH
        gates = jnp.concatenate(
            [wraw_ref[:, src:src + 2 * MLSTM_HEADS], jnp.zeros((D_MODEL, LANES - 2 * MLSTM_HEADS), F32)], axis=1)
        win_ref[:, GT:GT + LANES] = gates.astype(BF16)

    @pl.when(j == 0)
    def _():
        z_odd[...] = jnp.zeros_like(z_odd)

    @pl.when(j <= 1)
    def _():
        hbuf[0:HIST_ROWS, :] = jnp.zeros((HIST_ROWS, HIST_COLS), F32)
        rstate[...] = jnp.zeros_like(rstate)
        cstate[...] = jnp.zeros_like(cstate)
        mstate[...] = jnp.zeros_like(mstate)

    def step(z_next, z_ref):
        h = _rmsnorm(x_ref[...], rows_ref[R_NORM_MIX:R_NORM_MIX + 1, :]).astype(BF16)

        def project(c0, c1):
            z_next[:, c0:c1] = _dot(h, win_ref[:, c0:c1])

        slabs = [functools.partial(project, c0, min(c0 + PROJ_SLAB, Z_WIDTH)) for c0 in range(0, Z_WIDTH, PROJ_SLAB)]
        _mixer_tile(z_ref, cos_ref, sin_ref, dtab_ref, xi_ref, zeta_ref, cd_ref, bd_ref, avg_ref, rows_ref,
                    wpool_ref, mix_ref, hbuf, rstate, cstate, mstate, jnp.maximum(j - 1, 0), seq_tile,
                    _Interleave(slabs, (seq_tile // CHUNK) * STAGES_PER_CHUNK))

    @pl.when(j % 2 == 0)
    def _():
        step(z_even, z_odd)

    @pl.when(j % 2 == 1)
    def _():
        step(z_odd, z_even)


PROJ_SLAB = 256
STAGES_PER_CHUNK = 11


class _Interleave:
    def __init__(self, thunks, n_points):
        self.thunks, self.n_points, self.point, self.done = thunks, n_points, 0, 0

    def tick(self):
        self.point += 1
        due = len(self.thunks) if self.point >= self.n_points else (self.point * len(self.thunks)) // self.n_points
        while self.done < due:
            self.thunks[self.done]()
            self.done += 1


def _mixer_tile(z_ref, cos_ref, sin_ref, dtab_ref, xi_ref, zeta_ref, cd_ref, bd_ref, avg_ref, rows_ref,
                wpool_ref, mix_ref, hbuf, rstate, cstate, mstate, j, seq_tile, other_work):
    vec = lambda r, n: rows_ref[r:r + 1, 0:n]
    hbuf[HIST_ROWS:, :] = z_ref[:, PU:MV]

    lane = lax.broadcasted_iota(jnp.int32, (CHUNK, LANES), 1)
    row_i = lax.broadcasted_iota(jnp.int32, (CHUNK, LANES), 0)
    lane_row = lax.broadcasted_iota(jnp.int32, (1, LANES), 1)
    in_a = (lane_row < RET_HEAD_DIM).astype(BF16)
    in_b = (lane_row >= RET_HEAD_DIM).astype(BF16)
    lo_half = (lane % RET_HEAD_DIM) < (RET_HEAD_DIM // 2)
    key_le_query = row_i <= lane
    lane_p = lax.broadcasted_iota(jnp.int32, (CHUNK, POOL_WIDTH), 1)
    row_p = lax.broadcasted_iota(jnp.int32, (CHUNK, POOL_WIDTH), 0)

    for c in range(seq_tile // CHUNK):
        r0 = c * CHUNK
        rows = pl.ds(r0, CHUNK)

        ext = hbuf[r0:r0 + HIST_ROWS + CHUNK, 0:POOL_WIDTH]
        s2 = ext + pltpu.roll(ext, 1, 0)
        s4 = s2 + pltpu.roll(s2, 2, 0)
        s8 = s4 + pltpu.roll(s4, 4, 0)
        s16 = s8 + pltpu.roll(s8, 8, 0)
        u, s2, s4, s8, s16 = (a[HIST_ROWS:] for a in (ext, s2, s4, s8, s16))
        g0, g1, g2 = (lane_p < POOL_GROUP_DIM, lane_p < 2 * POOL_GROUP_DIM, lane_p < 3 * POOL_GROUP_DIM)
        wsum = jnp.where(g0, s2, jnp.where(g1, s4, jnp.where(g2, s8, s16)))
        width = jnp.where(g0, 2, jnp.where(g1, 4, jnp.where(g2, 8, 16)))
        tpos = row_p + (j * seq_tile + r0 + 1)
        count = jnp.minimum(tpos, width).astype(F32)
        pooled = wsum / count - u
        y_pool = _dot(pooled.astype(BF16), wpool_ref[...]) * vec(R_POOL_SCALE, POOL_WIDTH)
        mix_ref[rows, RET_WIDTH:RET_WIDTH + POOL_WIDTH] = y_pool.astype(BF16)
        other_work.tick()

        cosv = cos_ref[rows, :]
        sinv = sin_ref[rows, :]
        ys = []
        for p in range(RET_PAIRS):
            qp = z_ref[rows, RQ + p * LANES:RQ + (p + 1) * LANES]
            kp = z_ref[rows, RK + p * LANES:RK + (p + 1) * LANES]
            vp = z_ref[rows, RV + p * LANES:RV + (p + 1) * LANES]
            q = qp * cosv + _swap_halves(qp, lo_half) * sinv
            k = kp * cosv + _swap_halves(kp, lo_half) * sinv
            kb, vb = k.astype(BF16), vp.astype(BF16)
            k2 = jnp.concatenate([kb * in_a, kb * in_b], axis=0)
            v2 = jnp.concatenate([vb * in_a, vb * in_b], axis=0)
            probs = _dot_nt(q.astype(BF16), k2).astype(BF16) * dtab_ref[p]
            lhs = jnp.concatenate([probs, (q * xi_ref[p]).astype(BF16)], axis=1)
            rhs = jnp.concatenate([v2, rstate[p].astype(BF16)], axis=0)
            ys.append(_dot(lhs, rhs))
            kz_t = (k * zeta_ref[p]).T.astype(BF16)
            rstate[p] = rstate[p] * cd_ref[p] + _dot(kz_t, vb) * bd_ref[...]
            other_work.tick()
        y = jnp.concatenate(ys, axis=0)
        yc = y - _dot(y.astype(BF16), avg_ref[...])
        var = _dot((yc * yc).astype(BF16), avg_ref[...])
        yn = yc * lax.rsqrt(var + EPS)
        for p in range(RET_PAIRS):
            cs = slice(p * LANES, (p + 1) * LANES)
            gp = z_ref[rows, RG + p * LANES:RG + (p + 1) * LANES]
            gain = rows_ref[R_RET_GN:R_RET_GN + 1, cs]
            mix_ref[rows, cs] = (_silu(gp) * (yn[p * CHUNK:(p + 1) * CHUNK] * gain)).astype(BF16)
        other_work.tick()

        ext = hbuf[r0 + HIST_ROWS - 8:r0 + HIST_ROWS + CHUNK, POOL_WIDTH:HIST_COLS]
        tap = lambda kk: vec(R_CONV_W + kk, 2 * MLSTM_WIDTH_P)
        prev = pltpu.roll(ext, 1, 0)
        older = pltpu.roll(tap(1) * ext + tap(0) * prev, 2, 0)
        conv = vec(R_CONV_B, 2 * MLSTM_WIDTH_P) + tap(3) * ext[8:] + tap(2) * prev[8:] + older[8:]
        qk = _silu(conv)
        other_work.tick()

        gates_t = (z_ref[rows, GT:GT + LANES] + vec(R_GATE_BIAS, LANES)).T
        li = gates_t[0:8]
        fpre = pltpu.roll(li, MLSTM_HEADS, 0)
        lf = jnp.minimum(fpre, 0.0) - jnp.log(1.0 + jnp.exp(-jnp.abs(fpre)))
        bcum = _lane_scan(lf, jnp.add, 0.0)
        g = li - bcum
        cmax = _lane_scan(g, jnp.maximum, NEG)
        m_prev = mstate[...]
        big_g = jnp.maximum(m_prev, cmax)
        inter = jnp.exp(m_prev - big_g)
        emr = jnp.exp(-(bcum + big_g))
        g_last = jnp.broadcast_to(big_g[:, LANES - 1:LANES], (8, LANES))
        b_last = jnp.broadcast_to(bcum[:, LANES - 1:LANES], (8, LANES))
        wk = jnp.exp(g - g_last)
        s_old = jnp.exp(m_prev - g_last)
        mstate[...] = b_last + g_last
        g_cols = jnp.concatenate([g, jnp.zeros((CHUNK - 8, LANES), F32)], axis=0).T
        other_work.tick()

        for h in range(MLSTM_HEADS):
            hs = slice(h * HEAD_PAD, (h + 1) * HEAD_PAD)
            qh = (qk[:, h * HEAD_PAD:(h + 1) * HEAD_PAD] * (MLSTM_HEAD_DIM ** -0.5)).astype(BF16)
            kh = qk[:, MLSTM_WIDTH_P + h * HEAD_PAD:MLSTM_WIDTH_P + (h + 1) * HEAD_PAD].astype(BF16)
            vh = z_ref[rows, MV + h * HEAD_PAD:MV + (h + 1) * HEAD_PAD]
            oh = z_ref[rows, MO + h * HEAD_PAD:MO + (h + 1) * HEAD_PAD]
            v_t = jnp.where(lane == MLSTM_HEAD_DIM, 1.0, vh).T
            decay_t = jnp.exp(jnp.where(key_le_query, g_cols[:, h:h + 1] - big_g[h:h + 1, :], NEG))
            sc_t = _dot_nt(kh, qh) * decay_t
            c_t = cstate[h]
            nd_t = _dot(v_t.astype(BF16), sc_t.astype(BF16)) + _dot_nt(c_t.astype(BF16), qh) * inter[h:h + 1, :]
            den = nd_t[MLSTM_HEAD_DIM:MLSTM_HEAD_DIM + 1, :]
            h_t = nd_t[0:MLSTM_HEAD_DIM, :] * (1.0 / jnp.maximum(jnp.abs(den), emr[h:h + 1, :]))
            mu = jnp.sum(h_t, axis=0, keepdims=True) / MLSTM_HEAD_DIM
            hc = h_t - mu
            var = jnp.sum(hc * hc, axis=0, keepdims=True) / MLSTM_HEAD_DIM
            hn_t = hc * lax.rsqrt(var + EPS)
            hn = jnp.concatenate([hn_t, jnp.zeros((HEAD_PAD - MLSTM_HEAD_DIM, LANES), F32)], axis=0).T
            c0 = RET_WIDTH + POOL_WIDTH + h * HEAD_PAD
            mix_ref[rows, c0:c0 + HEAD_PAD] = (
                _sigmoid(oh) * (hn * rows_ref[R_MLSTM_GN:R_MLSTM_GN + 1, hs])).astype(BF16)
            cstate[h] = c_t * s_old[h:h + 1, :] + _dot((v_t * wk[h:h + 1, :]).astype(BF16), kh)
            other_work.tick()

    hbuf[0:HIST_ROWS, :] = hbuf[seq_tile:seq_tile + HIST_ROWS, :]


def _retention_tables():
    lg = np.log1p(-(2.0 ** (-5.0 - np.arange(RET_HEADS, dtype=np.float64))))
    idx = np.arange(CHUNK, dtype=np.float64)
    rel = idx[:, None] - idx[None, :]
    head_of_lane = np.arange(LANES) // RET_HEAD_DIM
    dtab, xi, zeta, cd = [], [], [], []
    same = (head_of_lane[:, None] == head_of_lane[None, :])
    key_scale = RET_HEAD_DIM ** -0.5
    for p in range(RET_PAIRS):
        hl = lg[2 * p + head_of_lane]
        dtab.append(key_scale * np.concatenate(
            [np.where(rel >= 0, np.exp(lg[2 * p + a] * np.maximum(rel, 0.0)), 0.0) for a in range(2)], axis=1))
        xi.append(np.exp(hl[None, :] * (idx[:, None] + 1.0)))
        zeta.append(key_scale * np.exp(hl[None, :] * (CHUNK - 1 - idx[:, None])))
        cd.append(np.where(same, np.exp(hl * CHUNK)[:, None], 0.0))
    f = lambda a: jnp.asarray(np.stack(a).astype(np.float32))
    avg = jnp.asarray((same / RET_HEAD_DIM).astype(np.float32)).astype(BF16)
    return f(dtab).astype(BF16), f(xi), f(zeta), f(cd), jnp.asarray(same.astype(np.float32)), avg


def _mixers(x, cos_t, sin_t, tables, rows, wpool, w_in, layer, batch, seq):
    t = x.shape[0]
    n_tiles = seq // SEQ_TILE
    ahead = lambda w: pl.BlockSpec((SEQ_TILE, w), lambda b, j: (b * n_tiles + jnp.minimum(j, n_tiles - 1), 0))
    tile = lambda w: pl.BlockSpec((SEQ_TILE, w), lambda b, j: (b * n_tiles + jnp.maximum(j - 1, 0), 0))
    full = lambda a: pl.BlockSpec(a.shape, lambda b, j: (0,) * a.ndim)
    resident = lambda a: pl.BlockSpec((None,) + a.shape[1:], lambda b, j: (layer,) + (0,) * (a.ndim - 1),
                                      pipeline_mode=pl.Buffered(1))
    consts = tuple(tables) + (rows, wpool, w_in)
    return pl.pallas_call(
        functools.partial(_mixer_kernel, seq_tile=SEQ_TILE),
        grid=(batch, n_tiles + 1),
        in_specs=[ahead(D_MODEL), tile(LANES), tile(LANES)] + [full(a) for a in tables]
        + [_layer_block(rows, layer), _layer_block(wpool, layer), resident(w_in)],
        out_specs=tile(MIX_WIDTH_P),
        out_shape=jax.ShapeDtypeStruct((t, MIX_WIDTH_P), BF16),
        scratch_shapes=[
            pltpu.VMEM((D_MODEL, Z_WIDTH), BF16),
            pltpu.VMEM((SEQ_TILE, Z_WIDTH), F32),
            pltpu.VMEM((SEQ_TILE, Z_WIDTH), F32),
            pltpu.VMEM((SEQ_TILE + HIST_ROWS, HIST_COLS), F32),
            pltpu.VMEM((RET_PAIRS, LANES, LANES), F32),
            pltpu.VMEM((MLSTM_HEADS, LANES, LANES), F32),
            pltpu.VMEM((8, LANES), F32),
        ],
        compiler_params=pltpu.CompilerParams(
            dimension_semantics=("arbitrary", "arbitrary"), vmem_limit_bytes=VMEM_LIMIT),
        name="token_mixers",
    )(x, cos_t, sin_t, *consts)


FF_CHUNK = 256


def _channel_kernel(x_ref, mix_ref, p_ref, rows_ref, wout_ref, wgu_ref, wdown_ref,
                    wpg_ref, wpp_ref, o_ref, *, final):
    vec = lambda r: rows_ref[r:r + 1, :]
    o = RET_WIDTH + POOL_WIDTH
    pad = jnp.zeros((HEAD_PAD - MLSTM_HEAD_DIM, D_MODEL), BF16)
    w_rows = [wout_ref[0:o, :]]
    for hd in range(MLSTM_HEADS):
        w_rows += [wout_ref[o + hd * MLSTM_HEAD_DIM:o + (hd + 1) * MLSTM_HEAD_DIM, :], pad]
    x1 = x_ref[...] + _dot(mix_ref[...], jnp.concatenate(w_rows, axis=0))
    h = _rmsnorm(x1, vec(R_NORM_FFN)).astype(BF16)
    acc = x1
    for c in range(D_FF // FF_CHUNK):
        gate = _dot(h, wgu_ref[:, c * FF_CHUNK:(c + 1) * FF_CHUNK])
        up = _dot(h, wgu_ref[:, D_FF + c * FF_CHUNK:D_FF + (c + 1) * FF_CHUNK])
        act = (gate * _sigmoid(gate) * up).astype(BF16)
        acc = acc + _dot(act, wdown_ref[c * FF_CHUNK:(c + 1) * FF_CHUNK, :])
    hp = _rmsnorm(acc, vec(R_NORM_PLE)).astype(BF16)
    emb = _dot(p_ref[...].astype(BF16), wpp_ref[...])
    x3 = acc + _sigmoid(_dot(hp, wpg_ref[...])) * emb
    if final:
        x3 = _rmsnorm(x3, vec(R_NORM_FINAL))
    o_ref[...] = x3


def _channel(x, mix, p, rows, wout, wgu, wdown, wpg, wpp, layer, final):
    t = x.shape[0]
    tile = lambda w: pl.BlockSpec((ROW_TILE, w), lambda i: (i, 0))
    resident = lambda a: pl.BlockSpec((None,) + a.shape[1:], lambda i: (layer,) + (0,) * (a.ndim - 1),
                                      pipeline_mode=pl.Buffered(1))
    consts = (rows, wout, wgu, wdown, wpg, wpp)
    return pl.pallas_call(
        functools.partial(_channel_kernel, final=final),
        grid=(t // ROW_TILE,),
        in_specs=[tile(D_MODEL), tile(MIX_WIDTH_P),
                  pl.BlockSpec((None, ROW_TILE, PLE_DIM), lambda i: (layer, i, 0))]
        + [resident(a) for a in consts],
        out_specs=tile(D_MODEL),
        out_shape=jax.ShapeDtypeStruct((t, D_MODEL), F32),
        compiler_params=pltpu.CompilerParams(
            dimension_semantics=("arbitrary",), vmem_limit_bytes=VMEM_LIMIT),
        name="channel_mixing",
    )(x, mix, p, *consts)


def _head_segments(v):
    zeros = jnp.zeros(v.shape[:-1] + (HEAD_PAD - MLSTM_HEAD_DIM,), v.dtype)
    out = []
    for h in range(v.shape[-1] // MLSTM_HEAD_DIM):
        out += [v[..., h * MLSTM_HEAD_DIM:(h + 1) * MLSTM_HEAD_DIM], zeros]
    return out


def _block_diag(w):
    depth, g, d, _ = w.shape
    eye = jnp.asarray(np.eye(g, dtype=np.float32))
    return (w[:, :, :, None, :] * eye[None, :, None, :, None]).reshape(depth, g * d, g * d)


def _pack_rows(depth, rows):
    segs = []
    for row in rows:
        width = sum(s.shape[-1] for s in row)
        segs += list(row) + [jnp.zeros((depth, D_MODEL - width), F32)]
    segs.append(jnp.zeros((depth, (N_ROWS - len(rows)) * D_MODEL), F32))
    return jnp.concatenate(segs, axis=-1).reshape(depth, N_ROWS, D_MODEL)


def kernel(x, p, positions, norm_mix, w_in, ret_gn, pool_w, pool_scale, conv_w, conv_b, b_igate, b_fgate,
           mlstm_gn, w_out, norm_ffn, w_gate_up, w_down, norm_ple, w_ple_gate, w_ple_proj, norm_final):
    batch, seq, d = x.shape
    depth = w_in.shape[0]
    t = batch * seq
    xf = x.reshape(t, d)
    cos_t, sin_t = _trig_tables(positions.astype(F32).reshape(t, 1))
    tables = _retention_tables()

    rows = _pack_rows(depth, [
        [norm_mix], [ret_gn], [pool_scale], _head_segments(conv_b), _head_segments(mlstm_gn),
        [b_igate, b_fgate], [norm_ffn], [norm_ple], [jnp.broadcast_to(norm_final, (depth, d))],
    ] + [_head_segments(conv_w[:, k]) for k in range(MLSTM_CONV)])
    wpool = _block_diag(pool_w).astype(BF16)
    wout = w_out.astype(BF16)
    wgu, wdown = w_gate_up.astype(BF16), w_down.astype(BF16)
    wpg, wpp = w_ple_gate.astype(BF16), w_ple_proj.astype(BF16)
    pf = p.reshape(depth, t, PLE_DIM)

    for i in range(depth):
        mix = _mixers(xf, cos_t, sin_t, tables, rows, wpool, w_in, i, batch, seq)
        xf = _channel(xf, mix, pf, rows, wout, wgu, wdown, wpg, wpp, i, final=(i == depth - 1))
    return xf.reshape(batch, seq, d)
```

```python
import functools

import numpy as np
import jax
import jax.numpy as jnp
from jax import lax
from jax.experimental import pallas as pl
from jax.experimental.pallas import tpu as pltpu

F32 = jnp.float32
BF16 = jnp.bfloat16

D_MODEL = 1024
PLE_DIM = 256
RET_HEADS = 6
RET_HEAD_DIM = 64
RET_WIDTH = RET_HEADS * RET_HEAD_DIM
RET_PAIRS = RET_HEADS // 2
POOL_WINDOWS = (2, 4, 8, 16)
POOL_GROUP_DIM = 64
POOL_WIDTH = len(POOL_WINDOWS) * POOL_GROUP_DIM
MLSTM_HEADS = 4
MLSTM_HEAD_DIM = 96
MLSTM_WIDTH = MLSTM_HEADS * MLSTM_HEAD_DIM
MLSTM_CONV = 4
CHUNK = 128
D_FF = 2816
ROPE_BASE = 10000.0
EPS = 1e-6

LANES = 128
HEAD_PAD = LANES
MLSTM_WIDTH_P = MLSTM_HEADS * HEAD_PAD

RQ = 0
RK = RQ + RET_WIDTH
RV = RK + RET_WIDTH
RG = RV + RET_WIDTH
PU = RG + RET_WIDTH
MQ = PU + POOL_WIDTH
MK = MQ + MLSTM_WIDTH_P
MV = MK + MLSTM_WIDTH_P
MO = MV + MLSTM_WIDTH
GT = MO + MLSTM_WIDTH
Z_WIDTH = GT + LANES
HIST_COLS = MV - PU
HIST_ROWS = 16
MIX_WIDTH = RET_WIDTH + POOL_WIDTH + MLSTM_WIDTH
NEG = -1e30

ROW_TILE = 512
SEQ_TILE = 256
VMEM_LIMIT = 56 * 1024 * 1024

(R_NORM_MIX, R_RET_GN, R_POOL_SCALE, R_CONV_B, R_MLSTM_GN, R_GATE_BIAS, R_NORM_FFN, R_NORM_PLE,
 R_NORM_FINAL, R_CONV_W) = range(10)
N_ROWS = 16


def _rmsnorm(x, g):
    return x * lax.rsqrt(jnp.mean(x * x, axis=-1, keepdims=True) + EPS) * g


def _sigmoid(x):
    return 1.0 / (1.0 + jnp.exp(-x))


def _silu(x):
    half = 0.5 * x
    return half + half * jnp.tanh(half)


def _dot(a, b):
    return jnp.dot(a, b, preferred_element_type=F32)


def _dot_nt(a, b):
    return lax.dot_general(a, b, (((1,), (1,)), ((), ())), preferred_element_type=F32)


def _trig_kernel(pos_ref, inv_ref, sign_ref, cos_ref, sin_ref):
    ang = pos_ref[...] * inv_ref[...]
    cos_ref[...] = jnp.cos(ang)
    sin_ref[...] = jnp.sin(ang) * sign_ref[...]


def _trig_tables(pos_col):
    t = pos_col.shape[0]
    half = RET_HEAD_DIM // 2
    inv = ROPE_BASE ** (-jnp.arange(half, dtype=F32) / half)
    inv_row = jnp.tile(inv, LANES // half).reshape(1, LANES)
    sign = np.where((np.arange(LANES) % RET_HEAD_DIM) < half, -1.0, 1.0).astype(np.float32).reshape(1, LANES)
    tile = 2 * ROW_TILE
    row = pl.BlockSpec((1, LANES), lambda i: (0, 0))
    out = pl.BlockSpec((tile, LANES), lambda i: (i, 0))
    return pl.pallas_call(
        _trig_kernel,
        grid=(t // tile,),
        in_specs=[pl.BlockSpec((tile, 1), lambda i: (i, 0)), row, row],
        out_specs=[out, out],
        out_shape=[jax.ShapeDtypeStruct((t, LANES), F32)] * 2,
        compiler_params=pltpu.CompilerParams(dimension_semantics=("arbitrary",)),
        name="rope_tables",
    )(pos_col, inv_row, jnp.asarray(sign))


def _layer_block(a, layer):
    return pl.BlockSpec((None,) + a.shape[1:], lambda *_: (layer,) + (0,) * (a.ndim - 1))


def _swap_halves(x, lo_half):
    return jnp.where(lo_half, pltpu.roll(x, LANES - RET_HEAD_DIM // 2, 1), pltpu.roll(x, RET_HEAD_DIM // 2, 1))


def _lane_scan(x, op, fill):
    lane = lax.broadcasted_iota(jnp.int32, x.shape, 1)
    sh = 1
    while sh < LANES:
        x = op(x, jnp.where(lane >= sh, pltpu.roll(x, sh, 1), fill))
        sh *= 2
    return x


def _mixer_kernel(x_ref, cos_ref, sin_ref, dtab_ref, xi_ref, zeta_ref, cd_ref, bd_ref, avg_ref,
                  rows_ref, wpool_ref, wraw_ref,
                  mix_ref, win_ref, z_even, z_odd, hbuf, rstate, cstate, mstate, *, seq_tile, tiles_per_seq):
    j = pl.program_id(0)
    tile_in_seq = jnp.maximum(j - 1, 0) % tiles_per_seq

    @pl.when(j == 0)
    def _():
        o = 4 * RET_WIDTH + POOL_WIDTH
        win_ref[:, 0:o] = wraw_ref[:, 0:o].astype(BF16)
        pad = jnp.zeros((D_MODEL, HEAD_PAD - MLSTM_HEAD_DIM), F32)
        for k in range(2 * MLSTM_HEADS):
            src = o + k * MLSTM_HEAD_DIM
            head = jnp.concatenate([wraw_ref[:, src:src + MLSTM_HEAD_DIM], pad], axis=1)
            win_ref[:, o + k * HEAD_PAD:o + (k + 1) * HEAD_PAD] = head.astype(BF16)
        src = o + 2 * MLSTM_WIDTH
        win_ref[:, MV:GT] = wraw_ref[:, src:src + 2 * MLSTM_WIDTH].astype(BF16)
        src += 2 * MLSTM_WIDTH
        gates = jnp.concatenate(
            [wraw_ref[:, src:src + 2 * MLSTM_HEADS], jnp.zeros((D_MODEL, LANES - 2 * MLSTM_HEADS), F32)], axis=1)
        win_ref[:, GT:GT + LANES] = gates.astype(BF16)
        z_odd[...] = jnp.zeros_like(z_odd)

    @pl.when(tile_in_seq == 0)
    def _():
        hbuf[0:HIST_ROWS, :] = jnp.zeros((HIST_ROWS, HIST_COLS), F32)
        rstate[...] = jnp.zeros_like(rstate)
        cstate[...] = jnp.zeros_like(cstate)
        mstate[...] = jnp.zeros_like(mstate)

    def step(z_next, z_ref):
        h = _rmsnorm(x_ref[...], rows_ref[R_NORM_MIX:R_NORM_MIX + 1, :]).astype(BF16)

        def project(c0, c1):
            z_next[:, c0:c1] = _dot(h, win_ref[:, c0:c1])

        slabs = [functools.partial(project, c0, min(c0 + PROJ_SLAB, Z_WIDTH)) for c0 in range(0, Z_WIDTH, PROJ_SLAB)]
        _mixer_tile(z_ref, cos_ref, sin_ref, dtab_ref, xi_ref, zeta_ref, cd_ref, bd_ref, avg_ref, rows_ref,
                    wpool_ref, mix_ref, hbuf, rstate, cstate, mstate, tile_in_seq, seq_tile,
                    _Interleave(slabs, (seq_tile // CHUNK) * STAGES_PER_CHUNK))

    @pl.when(j % 2 == 0)
    def _():
        step(z_even, z_odd)

    @pl.when(j % 2 == 1)
    def _():
        step(z_odd, z_even)


PROJ_SLAB = 256
STAGES_PER_CHUNK = 11


class _Interleave:
    def __init__(self, thunks, n_points):
        self.thunks, self.n_points, self.point, self.done = thunks, n_points, 0, 0

    def tick(self):
        self.point += 1
        due = len(self.thunks) if self.point >= self.n_points else (self.point * len(self.thunks)) // self.n_points
        while self.done < due:
            self.thunks[self.done]()
            self.done += 1


def _mixer_tile(z_ref, cos_ref, sin_ref, dtab_ref, xi_ref, zeta_ref, cd_ref, bd_ref, avg_ref, rows_ref,
                wpool_ref, mix_ref, hbuf, rstate, cstate, mstate, j, seq_tile, other_work):
    vec = lambda r, n: rows_ref[r:r + 1, 0:n]
    hbuf[HIST_ROWS:, :] = z_ref[:, PU:MV]

    lane = lax.broadcasted_iota(jnp.int32, (CHUNK, LANES), 1)
    row_i = lax.broadcasted_iota(jnp.int32, (CHUNK, LANES), 0)
    lane_row = lax.broadcasted_iota(jnp.int32, (1, LANES), 1)
    in_a = (lane_row < RET_HEAD_DIM).astype(BF16)
    in_b = (lane_row >= RET_HEAD_DIM).astype(BF16)
    lo_half = (lane % RET_HEAD_DIM) < (RET_HEAD_DIM // 2)
    key_le_query = row_i <= lane
    lane_p = lax.broadcasted_iota(jnp.int32, (CHUNK, POOL_WIDTH), 1)
    row_p = lax.broadcasted_iota(jnp.int32, (CHUNK, POOL_WIDTH), 0)

    for c in range(seq_tile // CHUNK):
        r0 = c * CHUNK
        rows = pl.ds(r0, CHUNK)

        ext = hbuf[r0:r0 + HIST_ROWS + CHUNK, 0:POOL_WIDTH]
        s2 = ext + pltpu.roll(ext, 1, 0)
        s4 = s2 + pltpu.roll(s2, 2, 0)
        s8 = s4 + pltpu.roll(s4, 4, 0)
        s16 = s8 + pltpu.roll(s8, 8, 0)
        u, s2, s4, s8, s16 = (a[HIST_ROWS:] for a in (ext, s2, s4, s8, s16))
        g0, g1, g2 = (lane_p < POOL_GROUP_DIM, lane_p < 2 * POOL_GROUP_DIM, lane_p < 3 * POOL_GROUP_DIM)
        wsum = jnp.where(g0, s2, jnp.where(g1, s4, jnp.where(g2, s8, s16)))
        width = jnp.where(g0, 2, jnp.where(g1, 4, jnp.where(g2, 8, 16)))
        tpos = row_p + (j * seq_tile + r0 + 1)
        count = jnp.minimum(tpos, width).astype(F32)
        pooled = wsum / count - u
        y_pool = _dot(pooled.astype(BF16), wpool_ref[...]) * vec(R_POOL_SCALE, POOL_WIDTH)
        mix_ref[rows, RET_WIDTH:RET_WIDTH + POOL_WIDTH] = y_pool.astype(BF16)
        other_work.tick()

        cosv = cos_ref[rows, :]
        sinv = sin_ref[rows, :]
        ys = []
        for p in range(RET_PAIRS):
            qp = z_ref[rows, RQ + p * LANES:RQ + (p + 1) * LANES]
            kp = z_ref[rows, RK + p * LANES:RK + (p + 1) * LANES]
            vp = z_ref[rows, RV + p * LANES:RV + (p + 1) * LANES]
            q = qp * cosv + _swap_halves(qp, lo_half) * sinv
            k = kp * cosv + _swap_halves(kp, lo_half) * sinv
            kb, vb = k.astype(BF16), vp.astype(BF16)
            k2 = jnp.concatenate([kb * in_a, kb * in_b], axis=0)
            v2 = jnp.concatenate([vb * in_a, vb * in_b], axis=0)
            probs = _dot_nt(q.astype(BF16), k2).astype(BF16) * dtab_ref[p]
            lhs = jnp.concatenate([probs, (q * xi_ref[p]).astype(BF16)], axis=1)
            rhs = jnp.concatenate([v2, rstate[p].astype(BF16)], axis=0)
            ys.append(_dot(lhs, rhs))
            kz_t = (k * zeta_ref[p]).T.astype(BF16)
            rstate[p] = rstate[p] * cd_ref[p] + _dot(kz_t, vb) * bd_ref[...]
            other_work.tick()
        y = jnp.concatenate(ys, axis=0)
        yc = y - _dot(y.astype(BF16), avg_ref[...])
        var = _dot((yc * yc).astype(BF16), avg_ref[...])
        yn = yc * lax.rsqrt(var + EPS)
        for p in range(RET_PAIRS):
            cs = slice(p * LANES, (p + 1) * LANES)
            gp = z_ref[rows, RG + p * LANES:RG + (p + 1) * LANES]
            gain = rows_ref[R_RET_GN:R_RET_GN + 1, cs]
            mix_ref[rows, cs] = (_silu(gp) * (yn[p * CHUNK:(p + 1) * CHUNK] * gain)).astype(BF16)
        other_work.tick()

        ext = hbuf[r0 + HIST_ROWS - 8:r0 + HIST_ROWS + CHUNK, POOL_WIDTH:HIST_COLS]
        tap = lambda kk: vec(R_CONV_W + kk, 2 * MLSTM_WIDTH_P)
        prev = pltpu.roll(ext, 1, 0)
        older = pltpu.roll(tap(1) * ext + tap(0) * prev, 2, 0)
        conv = vec(R_CONV_B, 2 * MLSTM_WIDTH_P) + tap(3) * ext[8:] + tap(2) * prev[8:] + older[8:]
        qk = _silu(conv)
        other_work.tick()

        gates_t = (z_ref[rows, GT:GT + LANES] + vec(R_GATE_BIAS, LANES)).T
        li = gates_t[0:8]
        fpre = pltpu.roll(li, MLSTM_HEADS, 0)
        lf = jnp.minimum(fpre, 0.0) - jnp.log(1.0 + jnp.exp(-jnp.abs(fpre)))
        bcum = _lane_scan(lf, jnp.add, 0.0)
        g = li - bcum
        cmax = _lane_scan(g, jnp.maximum, NEG)
        m_prev = mstate[...]
        big_g = jnp.maximum(m_prev, cmax)
        inter = jnp.exp(m_prev - big_g)
        emr = jnp.exp(-(bcum + big_g))
        g_last = jnp.broadcast_to(big_g[:, LANES - 1:LANES], (8, LANES))
        b_last = jnp.broadcast_to(bcum[:, LANES - 1:LANES], (8, LANES))
        wk = jnp.exp(g - g_last)
        s_old = jnp.exp(m_prev - g_last)
        mstate[...] = b_last + g_last
        g_cols = jnp.concatenate([g, jnp.zeros((CHUNK - 8, LANES), F32)], axis=0).T
        other_work.tick()

        v_all_t = z_ref[rows, MV:MV + MLSTM_WIDTH].T
        ones_row = jnp.where(lax.broadcasted_iota(jnp.int32, (8, LANES), 0) == 0, 1.0, 0.0)
        v_tail = jnp.concatenate([ones_row, jnp.zeros((HEAD_PAD - MLSTM_HEAD_DIM - 8, LANES), F32)], axis=0)
        hn_ts = []
        for h in range(MLSTM_HEADS):
            qh = (qk[:, h * HEAD_PAD:(h + 1) * HEAD_PAD] * (MLSTM_HEAD_DIM ** -0.5)).astype(BF16)
            kh = qk[:, MLSTM_WIDTH_P + h * HEAD_PAD:MLSTM_WIDTH_P + (h + 1) * HEAD_PAD].astype(BF16)
            v_t = jnp.concatenate([v_all_t[h * MLSTM_HEAD_DIM:(h + 1) * MLSTM_HEAD_DIM], v_tail], axis=0)
            decay_t = jnp.exp(jnp.where(key_le_query, g_cols[:, h:h + 1] - big_g[h:h + 1, :], NEG))
            sc_t = _dot_nt(kh, qh) * decay_t
            c_t = cstate[h]
            nd_t = _dot(v_t.astype(BF16), sc_t.astype(BF16)) + _dot_nt(c_t.astype(BF16), qh) * inter[h:h + 1, :]
            den = nd_t[MLSTM_HEAD_DIM:MLSTM_HEAD_DIM + 1, :]
            h_t = nd_t[0:MLSTM_HEAD_DIM, :] * (1.0 / jnp.maximum(jnp.abs(den), emr[h:h + 1, :]))
            mu = jnp.sum(h_t, axis=0, keepdims=True) / MLSTM_HEAD_DIM
            hc = h_t - mu
            var = jnp.sum(hc * hc, axis=0, keepdims=True) / MLSTM_HEAD_DIM
            hn_ts.append(hc * lax.rsqrt(var + EPS))
            cstate[h] = c_t * s_old[h:h + 1, :] + _dot((v_t * wk[h:h + 1, :]).astype(BF16), kh)
            other_work.tick()
        hn = jnp.concatenate(hn_ts, axis=0).T
        o_gate = _sigmoid(z_ref[rows, MO:MO + MLSTM_WIDTH])
        mix_ref[rows, RET_WIDTH + POOL_WIDTH:MIX_WIDTH] = (
            o_gate * (hn * vec(R_MLSTM_GN, MLSTM_WIDTH))).astype(BF16)

    hbuf[0:HIST_ROWS, :] = hbuf[seq_tile:seq_tile + HIST_ROWS, :]


def _retention_tables():
    lg = np.log1p(-(2.0 ** (-5.0 - np.arange(RET_HEADS, dtype=np.float64))))
    idx = np.arange(CHUNK, dtype=np.float64)
    rel = idx[:, None] - idx[None, :]
    head_of_lane = np.arange(LANES) // RET_HEAD_DIM
    dtab, xi, zeta, cd = [], [], [], []
    same = (head_of_lane[:, None] == head_of_lane[None, :])
    key_scale = RET_HEAD_DIM ** -0.5
    for p in range(RET_PAIRS):
        hl = lg[2 * p + head_of_lane]
        dtab.append(key_scale * np.concatenate(
            [np.where(rel >= 0, np.exp(lg[2 * p + a] * np.maximum(rel, 0.0)), 0.0) for a in range(2)], axis=1))
        xi.append(np.exp(hl[None, :] * (idx[:, None] + 1.0)))
        zeta.append(key_scale * np.exp(hl[None, :] * (CHUNK - 1 - idx[:, None])))
        cd.append(np.where(same, np.exp(hl * CHUNK)[:, None], 0.0))
    f = lambda a: jnp.asarray(np.stack(a).astype(np.float32))
    avg = jnp.asarray((same / RET_HEAD_DIM).astype(np.float32)).astype(BF16)
    return f(dtab).astype(BF16), f(xi), f(zeta), f(cd), jnp.asarray(same.astype(np.float32)), avg


def _mixers(x, cos_t, sin_t, tables, rows, wpool, w_in, layer, seq):
    t = x.shape[0]
    n_tiles = t // SEQ_TILE
    ahead = lambda w: pl.BlockSpec((SEQ_TILE, w), lambda j: (jnp.minimum(j, n_tiles - 1), 0))
    tile = lambda w: pl.BlockSpec((SEQ_TILE, w), lambda j: (jnp.maximum(j - 1, 0), 0))
    full = lambda a: pl.BlockSpec(a.shape, lambda j: (0,) * a.ndim)
    resident = lambda a: pl.BlockSpec((None,) + a.shape[1:], lambda j: (layer,) + (0,) * (a.ndim - 1),
                                      pipeline_mode=pl.Buffered(1))
    consts = tuple(tables) + (rows, wpool, w_in)
    return pl.pallas_call(
        functools.partial(_mixer_kernel, seq_tile=SEQ_TILE, tiles_per_seq=seq // SEQ_TILE),
        grid=(n_tiles + 1,),
        in_specs=[ahead(D_MODEL), tile(LANES), tile(LANES)] + [full(a) for a in tables]
        + [_layer_block(rows, layer), _layer_block(wpool, layer), resident(w_in)],
        out_specs=tile(MIX_WIDTH),
        out_shape=jax.ShapeDtypeStruct((t, MIX_WIDTH), BF16),
        scratch_shapes=[
            pltpu.VMEM((D_MODEL, Z_WIDTH), BF16),
            pltpu.VMEM((SEQ_TILE, Z_WIDTH), F32),
            pltpu.VMEM((SEQ_TILE, Z_WIDTH), F32),
            pltpu.VMEM((SEQ_TILE + HIST_ROWS, HIST_COLS), F32),
            pltpu.VMEM((RET_PAIRS, LANES, LANES), F32),
            pltpu.VMEM((MLSTM_HEADS, LANES, LANES), F32),
            pltpu.VMEM((8, LANES), F32),
        ],
        compiler_params=pltpu.CompilerParams(
            dimension_semantics=("arbitrary",), vmem_limit_bytes=VMEM_LIMIT),
        name="token_mixers",
    )(x, cos_t, sin_t, *consts)


FF_CHUNK = 512


def _channel_kernel(x_ref, mix_ref, p_ref, rows_ref, wout_ref, wgu_ref, wdown_ref,
                    wpg_ref, wpp_ref, o_ref, *, final):
    vec = lambda r: rows_ref[r:r + 1, :]
    x1 = x_ref[...] + _dot(mix_ref[...], wout_ref[...])
    h = _rmsnorm(x1, vec(R_NORM_FFN)).astype(BF16)
    acc = x1
    for c0 in range(0, D_FF, FF_CHUNK):
        c1 = min(c0 + FF_CHUNK, D_FF)
        gate = _dot(h, wgu_ref[:, c0:c1])
        up = _dot(h, wgu_ref[:, D_FF + c0:D_FF + c1])
        act = (_silu(gate) * up).astype(BF16)
        acc = acc + _dot(act, wdown_ref[c0:c1, :])
    hp = _rmsnorm(acc, vec(R_NORM_PLE)).astype(BF16)
    emb = _dot(p_ref[...].astype(BF16), wpp_ref[...])
    x3 = acc + _sigmoid(_dot(hp, wpg_ref[...])) * emb
    if final:
        x3 = _rmsnorm(x3, vec(R_NORM_FINAL))
    o_ref[...] = x3


def _channel(x, mix, p, rows, wout, wgu, wdown, wpg, wpp, layer, final):
    t = x.shape[0]
    tile = lambda w: pl.BlockSpec((ROW_TILE, w), lambda i: (i, 0))
    resident = lambda a: pl.BlockSpec((None,) + a.shape[1:], lambda i: (layer,) + (0,) * (a.ndim - 1),
                                      pipeline_mode=pl.Buffered(1))
    consts = (rows, wout, wgu, wdown, wpg, wpp)
    return pl.pallas_call(
        functools.partial(_channel_kernel, final=final),
        grid=(t // ROW_TILE,),
        in_specs=[tile(D_MODEL), tile(MIX_WIDTH),
                  pl.BlockSpec((None, ROW_TILE, PLE_DIM), lambda i: (layer, i, 0))]
        + [resident(a) for a in consts],
        out_specs=tile(D_MODEL),
        out_shape=jax.ShapeDtypeStruct((t, D_MODEL), F32),
        compiler_params=pltpu.CompilerParams(
            dimension_semantics=("arbitrary",), vmem_limit_bytes=VMEM_LIMIT),
        name="channel_mixing",
    )(x, mix, p, *consts)


def _head_segments(v):
    zeros = jnp.zeros(v.shape[:-1] + (HEAD_PAD - MLSTM_HEAD_DIM,), v.dtype)
    out = []
    for h in range(v.shape[-1] // MLSTM_HEAD_DIM):
        out += [v[..., h * MLSTM_HEAD_DIM:(h + 1) * MLSTM_HEAD_DIM], zeros]
    return out


def _block_diag(w):
    depth, g, d, _ = w.shape
    eye = jnp.asarray(np.eye(g, dtype=np.float32))
    return (w[:, :, :, None, :] * eye[None, :, None, :, None]).reshape(depth, g * d, g * d)


def _pack_rows(depth, rows):
    segs = []
    for row in rows:
        width = sum(s.shape[-1] for s in row)
        segs += list(row) + [jnp.zeros((depth, D_MODEL - width), F32)]
    segs.append(jnp.zeros((depth, (N_ROWS - len(rows)) * D_MODEL), F32))
    return jnp.concatenate(segs, axis=-1).reshape(depth, N_ROWS, D_MODEL)


def kernel(x, p, positions, norm_mix, w_in, ret_gn, pool_w, pool_scale, conv_w, conv_b, b_igate, b_fgate,
           mlstm_gn, w_out, norm_ffn, w_gate_up, w_down, norm_ple, w_ple_gate, w_ple_proj, norm_final):
    batch, seq, d = x.shape
    depth = w_in.shape[0]
    t = batch * seq
    xf = x.reshape(t, d)
    cos_t, sin_t = _trig_tables(positions.astype(F32).reshape(t, 1))
    tables = _retention_tables()

    rows = _pack_rows(depth, [
        [norm_mix], [ret_gn], [pool_scale], _head_segments(conv_b), [mlstm_gn],
        [b_igate, b_fgate], [norm_ffn], [norm_ple], [jnp.broadcast_to(norm_final, (depth, d))],
    ] + [_head_segments(conv_w[:, k]) for k in range(MLSTM_CONV)])
    wpool = _block_diag(pool_w).astype(BF16)
    wout = w_out.astype(BF16)
    wgu, wdown = w_gate_up.astype(BF16), w_down.astype(BF16)
    wpg, wpp = w_ple_gate.astype(BF16), w_ple_proj.astype(BF16)
    pf = p.reshape(depth, t, PLE_DIM)

    for i in range(depth):
        mix = _mixers(xf, cos_t, sin_t, tables, rows, wpool, w_in, i, seq)
        xf = _channel(xf, mix, pf, rows, wout, wgu, wdown, wpg, wpp, i, final=(i == depth - 1))
    return xf.reshape(batch, seq, d)
```

```python
import functools

import numpy as np
import jax
import jax.numpy as jnp
from jax import lax
from jax.experimental import pallas as pl
from jax.experimental.pallas import tpu as pltpu

F32 = jnp.float32
BF16 = jnp.bfloat16

D_MODEL = 1024
PLE_DIM = 256
RET_HEADS = 6
RET_HEAD_DIM = 64
RET_WIDTH = RET_HEADS * RET_HEAD_DIM
RET_PAIRS = RET_HEADS // 2
POOL_WINDOWS = (2, 4, 8, 16)
POOL_GROUP_DIM = 64
POOL_WIDTH = len(POOL_WINDOWS) * POOL_GROUP_DIM
MLSTM_HEADS = 4
MLSTM_HEAD_DIM = 96
MLSTM_WIDTH = MLSTM_HEADS * MLSTM_HEAD_DIM
MLSTM_CONV = 4
CHUNK = 128
D_FF = 2816
ROPE_BASE = 10000.0
EPS = 1e-6

LANES = 128
HEAD_PAD = LANES
MLSTM_WIDTH_P = MLSTM_HEADS * HEAD_PAD

RQ = 0
RK = RQ + RET_WIDTH
RV = RK + RET_WIDTH
RG = RV + RET_WIDTH
PU = RG + RET_WIDTH
MQ = PU + POOL_WIDTH
MK = MQ + MLSTM_WIDTH_P
MV = MK + MLSTM_WIDTH_P
MO = MV + MLSTM_WIDTH
GT = MO + MLSTM_WIDTH
Z_WIDTH = GT + LANES
HIST_COLS = MV - PU
HIST_ROWS = 16
MIX_WIDTH = RET_WIDTH + POOL_WIDTH + MLSTM_WIDTH
NEG = -1e30

ROW_TILE = 512
SEQ_TILE = 256
VMEM_LIMIT = 56 * 1024 * 1024

(R_NORM_MIX, R_RET_GN, R_POOL_SCALE, R_CONV_B, R_MLSTM_GN, R_GATE_BIAS, R_NORM_FFN, R_NORM_PLE,
 R_NORM_FINAL, R_CONV_W) = range(10)
N_ROWS = 16


def _rmsnorm(x, g):
    return x * lax.rsqrt(jnp.mean(x * x, axis=-1, keepdims=True) + EPS) * g


def _sigmoid(x):
    return 1.0 / (1.0 + jnp.exp(-x))


def _silu(x):
    half = 0.5 * x
    return half + half * jnp.tanh(half)


def _dot(a, b):
    return jnp.dot(a, b, preferred_element_type=F32)


def _dot_nt(a, b):
    return lax.dot_general(a, b, (((1,), (1,)), ((), ())), preferred_element_type=F32)


def _trig_kernel(pos_ref, inv_ref, sign_ref, cos_ref, sin_ref):
    ang = pos_ref[...] * inv_ref[...]
    cos_ref[...] = jnp.cos(ang)
    sin_ref[...] = jnp.sin(ang) * sign_ref[...]


def _trig_tables(pos_col):
    t = pos_col.shape[0]
    half = RET_HEAD_DIM // 2
    inv = ROPE_BASE ** (-jnp.arange(half, dtype=F32) / half)
    inv_row = jnp.tile(inv, LANES // half).reshape(1, LANES)
    sign = np.where((np.arange(LANES) % RET_HEAD_DIM) < half, -1.0, 1.0).astype(np.float32).reshape(1, LANES)
    tile = 2 * ROW_TILE
    row = pl.BlockSpec((1, LANES), lambda i: (0, 0))
    out = pl.BlockSpec((tile, LANES), lambda i: (i, 0))
    return pl.pallas_call(
        _trig_kernel,
        grid=(t // tile,),
        in_specs=[pl.BlockSpec((tile, 1), lambda i: (i, 0)), row, row],
        out_specs=[out, out],
        out_shape=[jax.ShapeDtypeStruct((t, LANES), F32)] * 2,
        compiler_params=pltpu.CompilerParams(dimension_semantics=("arbitrary",)),
        name="rope_tables",
    )(pos_col, inv_row, jnp.asarray(sign))


def _layer_block(a, layer):
    return pl.BlockSpec((None,) + a.shape[1:], lambda *_: (layer,) + (0,) * (a.ndim - 1))


def _swap_halves(x, lo_half):
    return jnp.where(lo_half, pltpu.roll(x, LANES - RET_HEAD_DIM // 2, 1), pltpu.roll(x, RET_HEAD_DIM // 2, 1))


def _lane_scan(x, op, fill):
    lane = lax.broadcasted_iota(jnp.int32, x.shape, 1)
    sh = 1
    while sh < LANES:
        x = op(x, jnp.where(lane >= sh, pltpu.roll(x, sh, 1), fill))
        sh *= 2
    return x


def _mixer_kernel(x_ref, xnext_ref, cos_ref, sin_ref, dtab_ref, xi_ref, zeta_ref, cd_ref, bd_ref, avg_ref,
                  rows_ref, wpool_ref, wraw_ref,
                  mix_ref, win_ref, h_ref, z_even, z_odd, hbuf, rstate, cstate, mstate, *, seq_tile, tiles_per_seq):
    j = pl.program_id(0)
    tile_in_seq = jnp.maximum(j - 1, 0) % tiles_per_seq
    norm_in = lambda ref: _rmsnorm(ref[...], rows_ref[R_NORM_MIX:R_NORM_MIX + 1, :]).astype(BF16)

    @pl.when(j == 0)
    def _():
        h_ref[...] = norm_in(x_ref)
        o = 4 * RET_WIDTH + POOL_WIDTH
        win_ref[:, 0:o] = wraw_ref[:, 0:o].astype(BF16)
        pad = jnp.zeros((D_MODEL, HEAD_PAD - MLSTM_HEAD_DIM), F32)
        for k in range(2 * MLSTM_HEADS):
            src = o + k * MLSTM_HEAD_DIM
            head = jnp.concatenate([wraw_ref[:, src:src + MLSTM_HEAD_DIM], pad], axis=1)
            win_ref[:, o + k * HEAD_PAD:o + (k + 1) * HEAD_PAD] = head.astype(BF16)
        src = o + 2 * MLSTM_WIDTH
        win_ref[:, MV:GT] = wraw_ref[:, src:src + 2 * MLSTM_WIDTH].astype(BF16)
        src += 2 * MLSTM_WIDTH
        gates = jnp.concatenate(
            [wraw_ref[:, src:src + 2 * MLSTM_HEADS], jnp.zeros((D_MODEL, LANES - 2 * MLSTM_HEADS), F32)], axis=1)
        win_ref[:, GT:GT + LANES] = gates.astype(BF16)
        z_odd[...] = jnp.zeros_like(z_odd)

    @pl.when(tile_in_seq == 0)
    def _():
        hbuf[0:HIST_ROWS, :] = jnp.zeros((HIST_ROWS, HIST_COLS), F32)
        rstate[...] = jnp.zeros_like(rstate)
        cstate[...] = jnp.zeros_like(cstate)
        mstate[...] = jnp.zeros_like(mstate)

    def step(z_next, z_ref):
        def project(c0, c1):
            z_next[:, c0:c1] = _dot(h_ref[...], win_ref[:, c0:c1])

        slabs = [functools.partial(project, c0, min(c0 + PROJ_SLAB, Z_WIDTH)) for c0 in range(0, Z_WIDTH, PROJ_SLAB)]
        _mixer_tile(z_ref, cos_ref, sin_ref, dtab_ref, xi_ref, zeta_ref, cd_ref, bd_ref, avg_ref, rows_ref,
                    wpool_ref, mix_ref, hbuf, rstate, cstate, mstate, tile_in_seq, seq_tile,
                    _Interleave(slabs, (seq_tile // CHUNK) * STAGES_PER_CHUNK))
        h_ref[...] = norm_in(xnext_ref)

    @pl.when(j % 2 == 0)
    def _():
        step(z_even, z_odd)

    @pl.when(j % 2 == 1)
    def _():
        step(z_odd, z_even)


PROJ_SLAB = 256
STAGES_PER_CHUNK = 11


class _Interleave:
    def __init__(self, thunks, n_points):
        self.thunks, self.n_points, self.point, self.done = thunks, n_points, 0, 0

    def tick(self):
        self.point += 1
        due = len(self.thunks) if self.point >= self.n_points else (self.point * len(self.thunks)) // self.n_points
        while self.done < due:
            self.thunks[self.done]()
            self.done += 1


def _mixer_tile(z_ref, cos_ref, sin_ref, dtab_ref, xi_ref, zeta_ref, cd_ref, bd_ref, avg_ref, rows_ref,
                wpool_ref, mix_ref, hbuf, rstate, cstate, mstate, j, seq_tile, other_work):
    vec = lambda r, n: rows_ref[r:r + 1, 0:n]
    hbuf[HIST_ROWS:, :] = z_ref[:, PU:MV]

    lane = lax.broadcasted_iota(jnp.int32, (CHUNK, LANES), 1)
    row_i = lax.broadcasted_iota(jnp.int32, (CHUNK, LANES), 0)
    lane_row = lax.broadcasted_iota(jnp.int32, (1, LANES), 1)
    in_a = (lane_row < RET_HEAD_DIM).astype(BF16)
    in_b = (lane_row >= RET_HEAD_DIM).astype(BF16)
    lo_half = (lane % RET_HEAD_DIM) < (RET_HEAD_DIM // 2)
    key_le_query = row_i <= lane
    lane_p = lax.broadcasted_iota(jnp.int32, (CHUNK, POOL_WIDTH), 1)
    row_p = lax.broadcasted_iota(jnp.int32, (CHUNK, POOL_WIDTH), 0)

    for c in range(seq_tile // CHUNK):
        r0 = c * CHUNK
        rows = pl.ds(r0, CHUNK)

        ext = hbuf[r0:r0 + HIST_ROWS + CHUNK, 0:POOL_WIDTH]
        s2 = ext + pltpu.roll(ext, 1, 0)
        s4 = s2 + pltpu.roll(s2, 2, 0)
        s8 = s4 + pltpu.roll(s4, 4, 0)
        s16 = s8 + pltpu.roll(s8, 8, 0)
        u, s2, s4, s8, s16 = (a[HIST_ROWS:] for a in (ext, s2, s4, s8, s16))
        g0, g1, g2 = (lane_p < POOL_GROUP_DIM, lane_p < 2 * POOL_GROUP_DIM, lane_p < 3 * POOL_GROUP_DIM)
        wsum = jnp.where(g0, s2, jnp.where(g1, s4, jnp.where(g2, s8, s16)))
        width = jnp.where(g0, 2, jnp.where(g1, 4, jnp.where(g2, 8, 16)))
        tpos = row_p + (j * seq_tile + r0 + 1)
        count = jnp.minimum(tpos, width).astype(F32)
        pooled = wsum / count - u
        y_pool = _dot(pooled.astype(BF16), wpool_ref[...]) * vec(R_POOL_SCALE, POOL_WIDTH)
        mix_ref[rows, RET_WIDTH:RET_WIDTH + POOL_WIDTH] = y_pool.astype(BF16)
        other_work.tick()

        cosv = cos_ref[rows, :]
        sinv = sin_ref[rows, :]
        pairs = range(RET_PAIRS)
        col = lambda base, p: z_ref[rows, base + p * LANES:base + (p + 1) * LANES]
        rope = lambda a: a * cosv + _swap_halves(a, lo_half) * sinv
        q = [rope(col(RQ, p)) for p in pairs]
        k = [rope(col(RK, p)) for p in pairs]
        other_work.tick()
        kb = [k[p].astype(BF16) for p in pairs]
        vb = [col(RV, p).astype(BF16) for p in pairs]
        k2 = [jnp.concatenate([kb[p] * in_a, kb[p] * in_b], axis=0) for p in pairs]
        v2 = [jnp.concatenate([vb[p] * in_a, vb[p] * in_b], axis=0) for p in pairs]
        scores = [_dot_nt(q[p].astype(BF16), k2[p]) for p in pairs]
        kz_t = [(k[p] * zeta_ref[p]).T.astype(BF16) for p in pairs]
        other_work.tick()
        lhs = [jnp.concatenate([scores[p].astype(BF16) * dtab_ref[p], (q[p] * xi_ref[p]).astype(BF16)], axis=1)
               for p in pairs]
        rhs = [jnp.concatenate([v2[p], rstate[p].astype(BF16)], axis=0) for p in pairs]
        ys = [_dot(lhs[p], rhs[p]) for p in pairs]
        kv = [_dot(kz_t[p], vb[p]) for p in pairs]
        for p in pairs:
            rstate[p] = rstate[p] * cd_ref[p] + kv[p] * bd_ref[...]
        other_work.tick()
        y = jnp.concatenate(ys, axis=0)
        yc = y - _dot(y.astype(BF16), avg_ref[...])
        var = _dot((yc * yc).astype(BF16), avg_ref[...])
        yn = yc * lax.rsqrt(var + EPS)
        for p in range(RET_PAIRS):
            cs = slice(p * LANES, (p + 1) * LANES)
            gp = z_ref[rows, RG + p * LANES:RG + (p + 1) * LANES]
            gain = rows_ref[R_RET_GN:R_RET_GN + 1, cs]
            mix_ref[rows, cs] = (_silu(gp) * (yn[p * CHUNK:(p + 1) * CHUNK] * gain)).astype(BF16)
        other_work.tick()

        ext = hbuf[r0 + HIST_ROWS - 8:r0 + HIST_ROWS + CHUNK, POOL_WIDTH:HIST_COLS]
        tap = lambda kk: vec(R_CONV_W + kk, 2 * MLSTM_WIDTH_P)
        prev = pltpu.roll(ext, 1, 0)
        older = pltpu.roll(tap(1) * ext + tap(0) * prev, 2, 0)
        conv = vec(R_CONV_B, 2 * MLSTM_WIDTH_P) + tap(3) * ext[8:] + tap(2) * prev[8:] + older[8:]
        qk = _silu(conv)
        other_work.tick()

        gates_t = (z_ref[rows, GT:GT + LANES] + vec(R_GATE_BIAS, LANES)).T
        li = gates_t[0:8]
        fpre = pltpu.roll(li, MLSTM_HEADS, 0)
        lf = jnp.minimum(fpre, 0.0) - jnp.log(1.0 + jnp.exp(-jnp.abs(fpre)))
        bcum = _lane_scan(lf, jnp.add, 0.0)
        g = li - bcum
        cmax = _lane_scan(g, jnp.maximum, NEG)
        m_prev = mstate[...]
        big_g = jnp.maximum(m_prev, cmax)
        inter = jnp.exp(m_prev - big_g)
        emr = jnp.exp(-(bcum + big_g))
        g_last = jnp.broadcast_to(big_g[:, LANES - 1:LANES], (8, LANES))
        b_last = jnp.broadcast_to(bcum[:, LANES - 1:LANES], (8, LANES))
        wk = jnp.exp(g - g_last)
        s_old = jnp.exp(m_prev - g_last)
        mstate[...] = b_last + g_last
        g_cols = jnp.concatenate([g, jnp.zeros((CHUNK - 8, LANES), F32)], axis=0).T
        other_work.tick()

        v_all_t = z_ref[rows, MV:MV + MLSTM_WIDTH].T
        ones_row = jnp.where(lax.broadcasted_iota(jnp.int32, (8, LANES), 0) == 0, 1.0, 0.0)
        v_tail = jnp.concatenate([ones_row, jnp.zeros((HEAD_PAD - MLSTM_HEAD_DIM - 8, LANES), F32)], axis=0)
        heads = range(MLSTM_HEADS)
        row = lambda a, h: a[h:h + 1, :]
        qh = [(qk[:, h * HEAD_PAD:(h + 1) * HEAD_PAD] * (MLSTM_HEAD_DIM ** -0.5)).astype(BF16) for h in heads]
        kh = [qk[:, MLSTM_WIDTH_P + h * HEAD_PAD:MLSTM_WIDTH_P + (h + 1) * HEAD_PAD].astype(BF16) for h in heads]
        v_t = [jnp.concatenate([v_all_t[h * MLSTM_HEAD_DIM:(h + 1) * MLSTM_HEAD_DIM], v_tail], axis=0) for h in heads]
        c_t = [cstate[h] for h in heads]
        scores_t = [_dot_nt(kh[h], qh[h]) for h in heads]
        cross_t = [_dot_nt(c_t[h].astype(BF16), qh[h]) for h in heads]
        kv_t = [_dot((v_t[h] * row(wk, h)).astype(BF16), kh[h]) for h in heads]
        other_work.tick()
        for h in heads:
            cstate[h] = c_t[h] * row(s_old, h) + kv_t[h]
        decay_t = [jnp.exp(jnp.where(key_le_query, g_cols[:, h:h + 1] - row(big_g, h), NEG)) for h in heads]
        sc_t = [(scores_t[h] * decay_t[h]).astype(BF16) for h in heads]
        other_work.tick()
        nd_t = [_dot(v_t[h].astype(BF16), sc_t[h]) + cross_t[h] * row(inter, h) for h in heads]
        other_work.tick()
        hn_ts = []
        for h in heads:
            den = nd_t[h][MLSTM_HEAD_DIM:MLSTM_HEAD_DIM + 1, :]
            h_t = nd_t[h][0:MLSTM_HEAD_DIM, :] * (1.0 / jnp.maximum(jnp.abs(den), row(emr, h)))
            mu = jnp.sum(h_t, axis=0, keepdims=True) / MLSTM_HEAD_DIM
            hc = h_t - mu
            var = jnp.sum(hc * hc, axis=0, keepdims=True) / MLSTM_HEAD_DIM
            hn_ts.append(hc * lax.rsqrt(var + EPS))
        other_work.tick()
        hn = jnp.concatenate(hn_ts, axis=0).T
        o_gate = _sigmoid(z_ref[rows, MO:MO + MLSTM_WIDTH])
        mix_ref[rows, RET_WIDTH + POOL_WIDTH:MIX_WIDTH] = (
            o_gate * (hn * vec(R_MLSTM_GN, MLSTM_WIDTH))).astype(BF16)

    hbuf[0:HIST_ROWS, :] = hbuf[seq_tile:seq_tile + HIST_ROWS, :]


def _retention_tables():
    lg = np.log1p(-(2.0 ** (-5.0 - np.arange(RET_HEADS, dtype=np.float64))))
    idx = np.arange(CHUNK, dtype=np.float64)
    rel = idx[:, None] - idx[None, :]
    head_of_lane = np.arange(LANES) // RET_HEAD_DIM
    dtab, xi, zeta, cd = [], [], [], []
    same = (head_of_lane[:, None] == head_of_lane[None, :])
    key_scale = RET_HEAD_DIM ** -0.5
    for p in range(RET_PAIRS):
        hl = lg[2 * p + head_of_lane]
        dtab.append(key_scale * np.concatenate(
            [np.where(rel >= 0, np.exp(lg[2 * p + a] * np.maximum(rel, 0.0)), 0.0) for a in range(2)], axis=1))
        xi.append(np.exp(hl[None, :] * (idx[:, None] + 1.0)))
        zeta.append(key_scale * np.exp(hl[None, :] * (CHUNK - 1 - idx[:, None])))
        cd.append(np.where(same, np.exp(hl * CHUNK)[:, None], 0.0))
    f = lambda a: jnp.asarray(np.stack(a).astype(np.float32))
    avg = jnp.asarray((same / RET_HEAD_DIM).astype(np.float32)).astype(BF16)
    return f(dtab).astype(BF16), f(xi), f(zeta), f(cd), jnp.asarray(same.astype(np.float32)), avg


def _mixers(x, cos_t, sin_t, tables, rows, wpool, w_in, layer, seq):
    t = x.shape[0]
    n_tiles = t // SEQ_TILE
    ahead = lambda k: pl.BlockSpec((SEQ_TILE, D_MODEL), lambda j: (jnp.minimum(j + k, n_tiles - 1), 0))
    tile = lambda w: pl.BlockSpec((SEQ_TILE, w), lambda j: (jnp.maximum(j - 1, 0), 0))
    full = lambda a: pl.BlockSpec(a.shape, lambda j: (0,) * a.ndim)
    resident = lambda a: pl.BlockSpec((None,) + a.shape[1:], lambda j: (layer,) + (0,) * (a.ndim - 1),
                                      pipeline_mode=pl.Buffered(1))
    consts = tuple(tables) + (rows, wpool, w_in)
    return pl.pallas_call(
        functools.partial(_mixer_kernel, seq_tile=SEQ_TILE, tiles_per_seq=seq // SEQ_TILE),
        grid=(n_tiles + 1,),
        in_specs=[ahead(0), ahead(1), tile(LANES), tile(LANES)] + [full(a) for a in tables]
        + [_layer_block(rows, layer), _layer_block(wpool, layer), resident(w_in)],
        out_specs=tile(MIX_WIDTH),
        out_shape=jax.ShapeDtypeStruct((t, MIX_WIDTH), BF16),
        scratch_shapes=[
            pltpu.VMEM((D_MODEL, Z_WIDTH), BF16),
            pltpu.VMEM((SEQ_TILE, D_MODEL), BF16),
            pltpu.VMEM((SEQ_TILE, Z_WIDTH), F32),
            pltpu.VMEM((SEQ_TILE, Z_WIDTH), F32),
            pltpu.VMEM((SEQ_TILE + HIST_ROWS, HIST_COLS), F32),
            pltpu.VMEM((RET_PAIRS, LANES, LANES), F32),
            pltpu.VMEM((MLSTM_HEADS, LANES, LANES), F32),
            pltpu.VMEM((8, LANES), F32),
        ],
        compiler_params=pltpu.CompilerParams(
            dimension_semantics=("arbitrary",), vmem_limit_bytes=VMEM_LIMIT),
        name="token_mixers",
    )(x, x, cos_t, sin_t, *consts)


FF_CHUNK = 512


def _channel_kernel(x_ref, mix_ref, p_ref, rows_ref, wout_ref, wgu_ref, wdown_ref,
                    wpg_ref, wpp_ref, o_ref, *, final):
    vec = lambda r: rows_ref[r:r + 1, :]
    x1 = x_ref[...] + _dot(mix_ref[...], wout_ref[...])
    h = _rmsnorm(x1, vec(R_NORM_FFN)).astype(BF16)
    acc = x1
    for c0 in range(0, D_FF, FF_CHUNK):
        c1 = min(c0 + FF_CHUNK, D_FF)
        gate = _dot(h, wgu_ref[:, c0:c1])
        up = _dot(h, wgu_ref[:, D_FF + c0:D_FF + c1])
        act = (_silu(gate) * up).astype(BF16)
        acc = acc + _dot(act, wdown_ref[c0:c1, :])
    hp = _rmsnorm(acc, vec(R_NORM_PLE)).astype(BF16)
    emb = _dot(p_ref[...].astype(BF16), wpp_ref[...])
    x3 = acc + _sigmoid(_dot(hp, wpg_ref[...])) * emb
    if final:
        x3 = _rmsnorm(x3, vec(R_NORM_FINAL))
    o_ref[...] = x3


def _channel(x, mix, p, rows, wout, wgu, wdown, wpg, wpp, layer, final):
    t = x.shape[0]
    tile = lambda w: pl.BlockSpec((ROW_TILE, w), lambda i: (i, 0))
    resident = lambda a: pl.BlockSpec((None,) + a.shape[1:], lambda i: (layer,) + (0,) * (a.ndim - 1),
                                      pipeline_mode=pl.Buffered(1))
    consts = (rows, wout, wgu, wdown, wpg, wpp)
    return pl.pallas_call(
        functools.partial(_channel_kernel, final=final),
        grid=(t // ROW_TILE,),
        in_specs=[tile(D_MODEL), tile(MIX_WIDTH),
                  pl.BlockSpec((None, ROW_TILE, PLE_DIM), lambda i: (layer, i, 0))]
        + [resident(a) for a in consts],
        out_specs=tile(D_MODEL),
        out_shape=jax.ShapeDtypeStruct((t, D_MODEL), F32),
        compiler_params=pltpu.CompilerParams(
            dimension_semantics=("arbitrary",), vmem_limit_bytes=VMEM_LIMIT),
        name="channel_mixing",
    )(x, mix, p, *consts)


def _head_segments(v):
    zeros = jnp.zeros(v.shape[:-1] + (HEAD_PAD - MLSTM_HEAD_DIM,), v.dtype)
    out = []
    for h in range(v.shape[-1] // MLSTM_HEAD_DIM):
        out += [v[..., h * MLSTM_HEAD_DIM:(h + 1) * MLSTM_HEAD_DIM], zeros]
    return out


def _block_diag(w):
    depth, g, d, _ = w.shape
    eye = jnp.asarray(np.eye(g, dtype=np.float32))
    return (w[:, :, :, None, :] * eye[None, :, None, :, None]).reshape(depth, g * d, g * d)


def _pack_rows(depth, rows):
    segs = []
    for row in rows:
        width = sum(s.shape[-1] for s in row)
        segs += list(row) + [jnp.zeros((depth, D_MODEL - width), F32)]
    segs.append(jnp.zeros((depth, (N_ROWS - len(rows)) * D_MODEL), F32))
    return jnp.concatenate(segs, axis=-1).reshape(depth, N_ROWS, D_MODEL)


def kernel(x, p, positions, norm_mix, w_in, ret_gn, pool_w, pool_scale, conv_w, conv_b, b_igate, b_fgate,
           mlstm_gn, w_out, norm_ffn, w_gate_up, w_down, norm_ple, w_ple_gate, w_ple_proj, norm_final):
    batch, seq, d = x.shape
    depth = w_in.shape[0]
    t = batch * seq
    xf = x.reshape(t, d)
    cos_t, sin_t = _trig_tables(positions.astype(F32).reshape(t, 1))
    tables = _retention_tables()

    rows = _pack_rows(depth, [
        [norm_mix], [ret_gn], [pool_scale], _head_segments(conv_b), [mlstm_gn],
        [b_igate, b_fgate], [norm_ffn], [norm_ple], [jnp.broadcast_to(norm_final, (depth, d))],
    ] + [_head_segments(conv_w[:, k]) for k in range(MLSTM_CONV)])
    wpool = _block_diag(pool_w).astype(BF16)
    wout = w_out.astype(BF16)
    wgu, wdown = w_gate_up.astype(BF16), w_down.astype(BF16)
    wpg, wpp = w_ple_gate.astype(BF16), w_ple_proj.astype(BF16)
    pf = p.reshape(depth, t, PLE_DIM)

    for i in range(depth):
        mix = _mixers(xf, cos_t, sin_t, tables, rows, wpool, w_in, i, seq)
        xf = _channel(xf, mix, pf, rows, wout, wgu, wdown, wpg, wpp, i, final=(i == depth - 1))
    return xf.reshape(batch, seq, d)
```

```python
import functools

import numpy as np
import jax
import jax.numpy as jnp
from jax import lax
from jax.experimental import pallas as pl
from jax.experimental.pallas import tpu as pltpu

F32 = jnp.float32
BF16 = jnp.bfloat16

D_MODEL = 1024
PLE_DIM = 256
RET_HEADS = 6
RET_HEAD_DIM = 64
RET_WIDTH = RET_HEADS * RET_HEAD_DIM
RET_PAIRS = RET_HEADS // 2
POOL_WINDOWS = (2, 4, 8, 16)
POOL_GROUP_DIM = 64
POOL_WIDTH = len(POOL_WINDOWS) * POOL_GROUP_DIM
MLSTM_HEADS = 4
MLSTM_HEAD_DIM = 96
MLSTM_WIDTH = MLSTM_HEADS * MLSTM_HEAD_DIM
MLSTM_CONV = 4
CHUNK = 128
D_FF = 2816
ROPE_BASE = 10000.0
EPS = 1e-6

LANES = 128
HEAD_PAD = LANES
MLSTM_WIDTH_P = MLSTM_HEADS * HEAD_PAD

RQ = 0
RK = RQ + RET_WIDTH
RV = RK + RET_WIDTH
RG = RV + RET_WIDTH
PU = RG + RET_WIDTH
MQ = PU + POOL_WIDTH
MK = MQ + MLSTM_WIDTH_P
MV = MK + MLSTM_WIDTH_P
MO = MV + MLSTM_WIDTH
GT = MO + MLSTM_WIDTH
Z_WIDTH = GT + LANES
HIST_COLS = MV - PU
HIST_ROWS = 16
MIX_WIDTH = RET_WIDTH + POOL_WIDTH + MLSTM_WIDTH
NEG = -1e30

ROW_TILE = 512
SEQ_TILE = 256
VMEM_LIMIT = 56 * 1024 * 1024

(R_NORM_MIX, R_RET_GN, R_POOL_SCALE, R_CONV_B, R_MLSTM_GN, R_GATE_BIAS, R_NORM_FFN, R_NORM_PLE,
 R_NORM_FINAL, R_CONV_W) = range(10)
N_ROWS = 16


def _rmsnorm(x, g):
    return x * lax.rsqrt(jnp.mean(x * x, axis=-1, keepdims=True) + EPS) * g


def _sigmoid(x):
    return 1.0 / (1.0 + jnp.exp(-x))


def _silu(x):
    half = 0.5 * x
    return half + half * jnp.tanh(half)


def _dot(a, b):
    return jnp.dot(a, b, preferred_element_type=F32)


def _dot_nt(a, b):
    return lax.dot_general(a, b, (((1,), (1,)), ((), ())), preferred_element_type=F32)


def _trig_kernel(pos_ref, inv_ref, sign_ref, cos_ref, sin_ref):
    ang = pos_ref[...] * inv_ref[...]
    cos_ref[...] = jnp.cos(ang)
    sin_ref[...] = jnp.sin(ang) * sign_ref[...]


def _trig_tables(pos_col):
    t = pos_col.shape[0]
    half = RET_HEAD_DIM // 2
    inv = ROPE_BASE ** (-jnp.arange(half, dtype=F32) / half)
    inv_row = jnp.tile(inv, LANES // half).reshape(1, LANES)
    sign = np.where((np.arange(LANES) % RET_HEAD_DIM) < half, -1.0, 1.0).astype(np.float32).reshape(1, LANES)
    tile = 2 * ROW_TILE
    row = pl.BlockSpec((1, LANES), lambda i: (0, 0))
    out = pl.BlockSpec((tile, LANES), lambda i: (i, 0))
    return pl.pallas_call(
        _trig_kernel,
        grid=(t // tile,),
        in_specs=[pl.BlockSpec((tile, 1), lambda i: (i, 0)), row, row],
        out_specs=[out, out],
        out_shape=[jax.ShapeDtypeStruct((t, LANES), F32)] * 2,
        compiler_params=pltpu.CompilerParams(dimension_semantics=("arbitrary",)),
        name="rope_tables",
    )(pos_col, inv_row, jnp.asarray(sign))


def _layer_block(a, layer):
    return pl.BlockSpec((None,) + a.shape[1:], lambda *_: (layer,) + (0,) * (a.ndim - 1))


def _swap_halves(x, lo_half):
    return jnp.where(lo_half, pltpu.roll(x, LANES - RET_HEAD_DIM // 2, 1), pltpu.roll(x, RET_HEAD_DIM // 2, 1))


def _lane_scan(x, op, fill):
    lane = lax.broadcasted_iota(jnp.int32, x.shape, 1)
    sh = 1
    while sh < LANES:
        x = op(x, jnp.where(lane >= sh, pltpu.roll(x, sh, 1), fill))
        sh *= 2
    return x


def _mixer_kernel(x_ref, xnext_ref, cos_ref, sin_ref, dtab_ref, xi_ref, zeta_ref, cd_ref, bd_ref, avg_ref,
                  rows_ref, wpool_ref, wraw_ref,
                  mix_ref, win_ref, h_ref, z_even, z_odd, hbuf, rstate, cstate, mstate, *, seq_tile, tiles_per_seq):
    j = pl.program_id(0)
    tile_in_seq = jnp.maximum(j - 1, 0) % tiles_per_seq
    norm_in = lambda ref: _rmsnorm(ref[...], rows_ref[R_NORM_MIX:R_NORM_MIX + 1, :]).astype(BF16)

    @pl.when(j == 0)
    def _():
        h_ref[...] = norm_in(x_ref)
        def put(col, src, n):
            block = wraw_ref[src:src + n, :]
            if n < LANES:
                block = jnp.concatenate([block, jnp.zeros((LANES - n, D_MODEL), F32)], axis=0)
            win_ref[:, col:col + LANES] = block.T.astype(BF16)

        o = 4 * RET_WIDTH + POOL_WIDTH
        for c0 in range(0, o, LANES):
            put(c0, c0, LANES)
        for k in range(2 * MLSTM_HEADS):
            put(o + k * HEAD_PAD, o + k * MLSTM_HEAD_DIM, MLSTM_HEAD_DIM)
        src = o + 2 * MLSTM_WIDTH
        for c0 in range(0, 2 * MLSTM_WIDTH, LANES):
            put(MV + c0, src + c0, LANES)
        put(GT, src + 2 * MLSTM_WIDTH, 2 * MLSTM_HEADS)
        z_odd[...] = jnp.zeros_like(z_odd)

    @pl.when(tile_in_seq == 0)
    def _():
        hbuf[0:HIST_ROWS, :] = jnp.zeros((HIST_ROWS, HIST_COLS), F32)
        rstate[...] = jnp.zeros_like(rstate)
        cstate[...] = jnp.zeros_like(cstate)
        mstate[...] = jnp.zeros_like(mstate)

    def step(z_next, z_ref):
        def project(c0, c1):
            z_next[:, c0:c1] = _dot(h_ref[...], win_ref[:, c0:c1])

        slabs = [functools.partial(project, c0, min(c0 + PROJ_SLAB, Z_WIDTH)) for c0 in range(0, Z_WIDTH, PROJ_SLAB)]
        _mixer_tile(z_ref, cos_ref, sin_ref, dtab_ref, xi_ref, zeta_ref, cd_ref, bd_ref, avg_ref, rows_ref,
                    wpool_ref, mix_ref, hbuf, rstate, cstate, mstate, tile_in_seq, seq_tile,
                    _Interleave(slabs, (seq_tile // CHUNK) * STAGES_PER_CHUNK))
        h_ref[...] = norm_in(xnext_ref)

    @pl.when(j % 2 == 0)
    def _():
        step(z_even, z_odd)

    @pl.when(j % 2 == 1)
    def _():
        step(z_odd, z_even)


PROJ_SLAB = 256
STAGES_PER_CHUNK = 15


class _Interleave:
    def __init__(self, thunks, n_points):
        self.thunks, self.n_points, self.point, self.done = thunks, n_points, 0, 0

    def tick(self):
        self.point += 1
        due = len(self.thunks) if self.point >= self.n_points else (self.point * len(self.thunks)) // self.n_points
        while self.done < due:
            self.thunks[self.done]()
            self.done += 1

    def finish(self):
        self.point = self.n_points - 1
        self.tick()


def _mixer_tile(z_ref, cos_ref, sin_ref, dtab_ref, xi_ref, zeta_ref, cd_ref, bd_ref, avg_ref, rows_ref,
                wpool_ref, mix_ref, hbuf, rstate, cstate, mstate, j, seq_tile, other_work):
    vec = lambda r, n: rows_ref[r:r + 1, 0:n]
    hbuf[HIST_ROWS:, :] = z_ref[:, PU:MV]

    lane = lax.broadcasted_iota(jnp.int32, (CHUNK, LANES), 1)
    row_i = lax.broadcasted_iota(jnp.int32, (CHUNK, LANES), 0)
    lane_row = lax.broadcasted_iota(jnp.int32, (1, LANES), 1)
    in_a = (lane_row < RET_HEAD_DIM).astype(BF16)
    in_b = (lane_row >= RET_HEAD_DIM).astype(BF16)
    lo_half = (lane % RET_HEAD_DIM) < (RET_HEAD_DIM // 2)
    key_le_query = row_i <= lane
    lane_p = lax.broadcasted_iota(jnp.int32, (CHUNK, POOL_WIDTH), 1)
    row_p = lax.broadcasted_iota(jnp.int32, (CHUNK, POOL_WIDTH), 0)

    def pool_stages(r0, rows):
        ext = hbuf[r0:r0 + HIST_ROWS + CHUNK, 0:POOL_WIDTH]
        s2 = ext + pltpu.roll(ext, 1, 0)
        s4 = s2 + pltpu.roll(s2, 2, 0)
        s8 = s4 + pltpu.roll(s4, 4, 0)
        s16 = s8 + pltpu.roll(s8, 8, 0)
        u, s2, s4, s8, s16 = (a[HIST_ROWS:] for a in (ext, s2, s4, s8, s16))
        yield
        g0, g1, g2 = (lane_p < POOL_GROUP_DIM, lane_p < 2 * POOL_GROUP_DIM, lane_p < 3 * POOL_GROUP_DIM)
        wsum = jnp.where(g0, s2, jnp.where(g1, s4, jnp.where(g2, s8, s16)))
        width = jnp.where(g0, 2, jnp.where(g1, 4, jnp.where(g2, 8, 16)))
        tpos = row_p + (j * seq_tile + r0 + 1)
        count = jnp.minimum(tpos, width).astype(F32)
        pooled = wsum / count - u
        y_pool = _dot(pooled.astype(BF16), wpool_ref[...]) * vec(R_POOL_SCALE, POOL_WIDTH)
        mix_ref[rows, RET_WIDTH:RET_WIDTH + POOL_WIDTH] = y_pool.astype(BF16)

    def retention_stages(rows):
        pairs = range(RET_PAIRS)
        col = lambda base, p: z_ref[rows, base + p * LANES:base + (p + 1) * LANES]
        cosv = cos_ref[rows, :]
        sinv = sin_ref[rows, :]
        rope = lambda a: a * cosv + _swap_halves(a, lo_half) * sinv
        q = [rope(col(RQ, p)) for p in pairs]
        k = [rope(col(RK, p)) for p in pairs]
        yield
        kb = [k[p].astype(BF16) for p in pairs]
        vb = [col(RV, p).astype(BF16) for p in pairs]
        k2 = [jnp.concatenate([kb[p] * in_a, kb[p] * in_b], axis=0) for p in pairs]
        v2 = [jnp.concatenate([vb[p] * in_a, vb[p] * in_b], axis=0) for p in pairs]
        scores = [_dot_nt(q[p].astype(BF16), k2[p]) for p in pairs]
        kz_t = [(k[p] * zeta_ref[p]).T.astype(BF16) for p in pairs]
        yield
        lhs = [jnp.concatenate([scores[p].astype(BF16) * dtab_ref[p], (q[p] * xi_ref[p]).astype(BF16)], axis=1)
               for p in pairs]
        rhs = [jnp.concatenate([v2[p], rstate[p].astype(BF16)], axis=0) for p in pairs]
        ys = [_dot(lhs[p], rhs[p]) for p in pairs]
        kv = [_dot(kz_t[p], vb[p]) for p in pairs]
        for p in pairs:
            rstate[p] = rstate[p] * cd_ref[p] + kv[p] * bd_ref[...]
        yield
        y = jnp.concatenate(ys, axis=0)
        yc = y - _dot(y.astype(BF16), avg_ref[...])
        yield
        var = _dot((yc * yc).astype(BF16), avg_ref[...])
        yn = yc * lax.rsqrt(var + EPS)
        for p in pairs:
            cs = slice(p * LANES, (p + 1) * LANES)
            gain = rows_ref[R_RET_GN:R_RET_GN + 1, cs]
            mix_ref[rows, cs] = (_silu(col(RG, p)) * (yn[p * CHUNK:(p + 1) * CHUNK] * gain)).astype(BF16)

    def mlstm_stages(r0, rows):
        gates_t = (z_ref[rows, GT:GT + LANES] + vec(R_GATE_BIAS, LANES)).T
        li = gates_t[0:8]
        fpre = pltpu.roll(li, MLSTM_HEADS, 0)
        lf = jnp.minimum(fpre, 0.0) - jnp.log(1.0 + jnp.exp(-jnp.abs(fpre)))
        bcum = _lane_scan(lf, jnp.add, 0.0)
        g = li - bcum
        cmax = _lane_scan(g, jnp.maximum, NEG)
        m_prev = mstate[...]
        big_g = jnp.maximum(m_prev, cmax)
        inter = jnp.exp(m_prev - big_g)
        emr = jnp.exp(-(bcum + big_g))
        g_last = jnp.broadcast_to(big_g[:, LANES - 1:LANES], (8, LANES))
        b_last = jnp.broadcast_to(bcum[:, LANES - 1:LANES], (8, LANES))
        wk = jnp.exp(g - g_last)
        s_old = jnp.exp(m_prev - g_last)
        mstate[...] = b_last + g_last
        g_cols = jnp.concatenate([g, jnp.zeros((CHUNK - 8, LANES), F32)], axis=0).T
        yield
        ext = hbuf[r0 + HIST_ROWS - 8:r0 + HIST_ROWS + CHUNK, POOL_WIDTH:HIST_COLS]
        tap = lambda kk: vec(R_CONV_W + kk, 2 * MLSTM_WIDTH_P)
        prev = pltpu.roll(ext, 1, 0)
        older = pltpu.roll(tap(1) * ext + tap(0) * prev, 2, 0)
        yield
        conv = vec(R_CONV_B, 2 * MLSTM_WIDTH_P) + tap(3) * ext[8:] + tap(2) * prev[8:] + older[8:]
        qk = _silu(conv)
        yield
        v_all_t = z_ref[rows, MV:MV + MLSTM_WIDTH].T
        ones_row = jnp.where(lax.broadcasted_iota(jnp.int32, (8, LANES), 0) == 0, 1.0, 0.0)
        v_tail = jnp.concatenate([ones_row, jnp.zeros((HEAD_PAD - MLSTM_HEAD_DIM - 8, LANES), F32)], axis=0)
        heads = range(MLSTM_HEADS)
        row = lambda a, h: a[h:h + 1, :]
        qh = [(qk[:, h * HEAD_PAD:(h + 1) * HEAD_PAD] * (MLSTM_HEAD_DIM ** -0.5)).astype(BF16) for h in heads]
        kh = [qk[:, MLSTM_WIDTH_P + h * HEAD_PAD:MLSTM_WIDTH_P + (h + 1) * HEAD_PAD].astype(BF16) for h in heads]
        v_t = [jnp.concatenate([v_all_t[h * MLSTM_HEAD_DIM:(h + 1) * MLSTM_HEAD_DIM], v_tail], axis=0) for h in heads]
        c_t = [cstate[h] for h in heads]
        scores_t = [_dot_nt(kh[h], qh[h]) for h in heads]
        cross_t = [_dot_nt(c_t[h].astype(BF16), qh[h]) for h in heads]
        kv_t = [_dot((v_t[h] * row(wk, h)).astype(BF16), kh[h]) for h in heads]
        yield
        for h in heads:
            cstate[h] = c_t[h] * row(s_old, h) + kv_t[h]
        decay_t = [jnp.exp(jnp.where(key_le_query, g_cols[:, h:h + 1] - row(big_g, h), NEG)) for h in heads]
        sc_t = [(scores_t[h] * decay_t[h]).astype(BF16) for h in heads]
        yield
        nd_t = [_dot(v_t[h].astype(BF16), sc_t[h]) + cross_t[h] * row(inter, h) for h in heads]
        yield
        hn_ts = []
        for h in heads:
            den = nd_t[h][MLSTM_HEAD_DIM:MLSTM_HEAD_DIM + 1, :]
            h_t = nd_t[h][0:MLSTM_HEAD_DIM, :] * (1.0 / jnp.maximum(jnp.abs(den), row(emr, h)))
            mu = jnp.sum(h_t, axis=0, keepdims=True) / MLSTM_HEAD_DIM
            hc = h_t - mu
            var = jnp.sum(hc * hc, axis=0, keepdims=True) / MLSTM_HEAD_DIM
            hn_ts.append(hc * lax.rsqrt(var + EPS))
        yield
        hn = jnp.concatenate(hn_ts, axis=0).T
        o_gate = _sigmoid(z_ref[rows, MO:MO + MLSTM_WIDTH])
        mix_ref[rows, RET_WIDTH + POOL_WIDTH:MIX_WIDTH] = (
            o_gate * (hn * vec(R_MLSTM_GN, MLSTM_WIDTH))).astype(BF16)

    for c in range(seq_tile // CHUNK):
        r0 = c * CHUNK
        rows = pl.ds(r0, CHUNK)
        streams = [mlstm_stages(r0, rows), retention_stages(rows), pool_stages(r0, rows)]
        while streams:
            for s in list(streams):
                if next(s, StopIteration) is StopIteration:
                    streams.remove(s)
                other_work.tick()
    other_work.finish()

    hbuf[0:HIST_ROWS, :] = hbuf[seq_tile:seq_tile + HIST_ROWS, :]


def _retention_tables():
    lg = np.log1p(-(2.0 ** (-5.0 - np.arange(RET_HEADS, dtype=np.float64))))
    idx = np.arange(CHUNK, dtype=np.float64)
    rel = idx[:, None] - idx[None, :]
    head_of_lane = np.arange(LANES) // RET_HEAD_DIM
    dtab, xi, zeta, cd = [], [], [], []
    same = (head_of_lane[:, None] == head_of_lane[None, :])
    key_scale = RET_HEAD_DIM ** -0.5
    for p in range(RET_PAIRS):
        hl = lg[2 * p + head_of_lane]
        dtab.append(key_scale * np.concatenate(
            [np.where(rel >= 0, np.exp(lg[2 * p + a] * np.maximum(rel, 0.0)), 0.0) for a in range(2)], axis=1))
        xi.append(np.exp(hl[None, :] * (idx[:, None] + 1.0)))
        zeta.append(key_scale * np.exp(hl[None, :] * (CHUNK - 1 - idx[:, None])))
        cd.append(np.where(same, np.exp(hl * CHUNK)[:, None], 0.0))
    f = lambda a: jnp.asarray(np.stack(a).astype(np.float32))
    avg = jnp.asarray((same / RET_HEAD_DIM).astype(np.float32)).astype(BF16)
    return f(dtab).astype(BF16), f(xi), f(zeta), f(cd), jnp.asarray(same.astype(np.float32)), avg


def _mixers(x, cos_t, sin_t, tables, rows, wpool, w_in, layer, seq):
    t = x.shape[0]
    n_tiles = t // SEQ_TILE
    ahead = lambda k: pl.BlockSpec((SEQ_TILE, D_MODEL), lambda j: (jnp.minimum(j + k, n_tiles - 1), 0))
    tile = lambda w: pl.BlockSpec((SEQ_TILE, w), lambda j: (jnp.maximum(j - 1, 0), 0))
    full = lambda a: pl.BlockSpec(a.shape, lambda j: (0,) * a.ndim)
    resident = lambda a: pl.BlockSpec((None,) + a.shape[1:], lambda j: (layer,) + (0,) * (a.ndim - 1),
                                      pipeline_mode=pl.Buffered(1))
    consts = tuple(tables) + (rows, wpool, w_in)
    return pl.pallas_call(
        functools.partial(_mixer_kernel, seq_tile=SEQ_TILE, tiles_per_seq=seq // SEQ_TILE),
        grid=(n_tiles + 1,),
        in_specs=[ahead(0), ahead(1), tile(LANES), tile(LANES)] + [full(a) for a in tables]
        + [_layer_block(rows, layer), _layer_block(wpool, layer), resident(w_in)],
        out_specs=tile(MIX_WIDTH),
        out_shape=jax.ShapeDtypeStruct((t, MIX_WIDTH), BF16),
        scratch_shapes=[
            pltpu.VMEM((D_MODEL, Z_WIDTH), BF16),
            pltpu.VMEM((SEQ_TILE, D_MODEL), BF16),
            pltpu.VMEM((SEQ_TILE, Z_WIDTH), F32),
            pltpu.VMEM((SEQ_TILE, Z_WIDTH), F32),
            pltpu.VMEM((SEQ_TILE + HIST_ROWS, HIST_COLS), F32),
            pltpu.VMEM((RET_PAIRS, LANES, LANES), F32),
            pltpu.VMEM((MLSTM_HEADS, LANES, LANES), F32),
            pltpu.VMEM((8, LANES), F32),
        ],
        compiler_params=pltpu.CompilerParams(
            dimension_semantics=("arbitrary",), vmem_limit_bytes=VMEM_LIMIT),
        name="token_mixers",
    )(x, x, cos_t, sin_t, *consts)


FF_CHUNK = 512


def _channel_kernel(x_ref, mix_ref, p_ref, rows_ref, wout_ref, wgu_ref, wdown_ref,
                    wpg_ref, wpp_ref, o_ref, *, final):
    vec = lambda r: rows_ref[r:r + 1, :]
    x1 = x_ref[...] + _dot(mix_ref[...], wout_ref[...])
    h = _rmsnorm(x1, vec(R_NORM_FFN)).astype(BF16)
    acc = x1
    for c0 in range(0, D_FF, FF_CHUNK):
        c1 = min(c0 + FF_CHUNK, D_FF)
        gate = _dot(h, wgu_ref[:, c0:c1])
        up = _dot(h, wgu_ref[:, D_FF + c0:D_FF + c1])
        act = (_silu(gate) * up).astype(BF16)
        acc = acc + _dot(act, wdown_ref[c0:c1, :])
    hp = _rmsnorm(acc, vec(R_NORM_PLE)).astype(BF16)
    emb = _dot(p_ref[...].astype(BF16), wpp_ref[...])
    x3 = acc + _sigmoid(_dot(hp, wpg_ref[...])) * emb
    if final:
        x3 = _rmsnorm(x3, vec(R_NORM_FINAL))
    o_ref[...] = x3


def _channel(x, mix, p, rows, wout, wgu, wdown, wpg, wpp, layer, final):
    t = x.shape[0]
    tile = lambda w: pl.BlockSpec((ROW_TILE, w), lambda i: (i, 0))
    resident = lambda a: pl.BlockSpec((None,) + a.shape[1:], lambda i: (layer,) + (0,) * (a.ndim - 1),
                                      pipeline_mode=pl.Buffered(1))
    consts = (rows, wout, wgu, wdown, wpg, wpp)
    return pl.pallas_call(
        functools.partial(_channel_kernel, final=final),
        grid=(t // ROW_TILE,),
        in_specs=[tile(D_MODEL), tile(MIX_WIDTH),
                  pl.BlockSpec((None, ROW_TILE, PLE_DIM), lambda i: (layer, i, 0))]
        + [resident(a) for a in consts],
        out_specs=tile(D_MODEL),
        out_shape=jax.ShapeDtypeStruct((t, D_MODEL), F32),
        compiler_params=pltpu.CompilerParams(
            dimension_semantics=("arbitrary",), vmem_limit_bytes=VMEM_LIMIT),
        name="channel_mixing",
    )(x, mix, p, *consts)


def _head_segments(v):
    zeros = jnp.zeros(v.shape[:-1] + (HEAD_PAD - MLSTM_HEAD_DIM,), v.dtype)
    out = []
    for h in range(v.shape[-1] // MLSTM_HEAD_DIM):
        out += [v[..., h * MLSTM_HEAD_DIM:(h + 1) * MLSTM_HEAD_DIM], zeros]
    return out


def _block_diag(w):
    depth, g, d, _ = w.shape
    eye = jnp.asarray(np.eye(g, dtype=np.float32))
    return (w[:, :, :, None, :] * eye[None, :, None, :, None]).reshape(depth, g * d, g * d)


def _pack_rows(depth, rows):
    segs = []
    for row in rows:
        width = sum(s.shape[-1] for s in row)
        segs += list(row) + [jnp.zeros((depth, D_MODEL - width), F32)]
    segs.append(jnp.zeros((depth, (N_ROWS - len(rows)) * D_MODEL), F32))
    return jnp.concatenate(segs, axis=-1).reshape(depth, N_ROWS, D_MODEL)


def kernel(x, p, positions, norm_mix, w_in, ret_gn, pool_w, pool_scale, conv_w, conv_b, b_igate, b_fgate,
           mlstm_gn, w_out, norm_ffn, w_gate_up, w_down, norm_ple, w_ple_gate, w_ple_proj, norm_final):
    batch, seq, d = x.shape
    depth = w_in.shape[0]
    t = batch * seq
    xf = x.reshape(t, d)
    cos_t, sin_t = _trig_tables(positions.astype(F32).reshape(t, 1))
    tables = _retention_tables()

    rows = _pack_rows(depth, [
        [norm_mix], [ret_gn], [pool_scale], _head_segments(conv_b), [mlstm_gn],
        [b_igate, b_fgate], [norm_ffn], [norm_ple], [jnp.broadcast_to(norm_final, (depth, d))],
    ] + [_head_segments(conv_w[:, k]) for k in range(MLSTM_CONV)])
    wpool = _block_diag(pool_w).astype(BF16)
    w_in_t = jnp.swapaxes(w_in, 1, 2)
    wout = w_out.astype(BF16)
    wgu, wdown = w_gate_up.astype(BF16), w_down.astype(BF16)
    wpg, wpp = w_ple_gate.astype(BF16), w_ple_proj.astype(BF16)
    pf = p.reshape(depth, t, PLE_DIM)

    for i in range(depth):
        mix = _mixers(xf, cos_t, sin_t, tables, rows, wpool, w_in_t, i, seq)
        xf = _channel(xf, mix, pf, rows, wout, wgu, wdown, wpg, wpp, i, final=(i == depth - 1))
    return xf.reshape(batch, seq, d)
```

```python
import functools

import numpy as np
import jax
import jax.numpy as jnp
from jax import lax
from jax.experimental import pallas as pl
from jax.experimental.pallas import tpu as pltpu

F32 = jnp.float32
BF16 = jnp.bfloat16

D_MODEL = 1024
PLE_DIM = 256
RET_HEADS = 6
RET_HEAD_DIM = 64
RET_WIDTH = RET_HEADS * RET_HEAD_DIM
RET_PAIRS = RET_HEADS // 2
POOL_WINDOWS = (2, 4, 8, 16)
POOL_GROUP_DIM = 64
POOL_WIDTH = len(POOL_WINDOWS) * POOL_GROUP_DIM
MLSTM_HEADS = 4
MLSTM_HEAD_DIM = 96
MLSTM_WIDTH = MLSTM_HEADS * MLSTM_HEAD_DIM
MLSTM_CONV = 4
CHUNK = 128
D_FF = 2816
ROPE_BASE = 10000.0
EPS = 1e-6

LANES = 128
HEAD_PAD = LANES
MLSTM_WIDTH_P = MLSTM_HEADS * HEAD_PAD

RQ = 0
RK = RQ + RET_WIDTH
RV = RK + RET_WIDTH
RG = RV + RET_WIDTH
PU = RG + RET_WIDTH
MQ = PU + POOL_WIDTH
MK = MQ + MLSTM_WIDTH_P
MV = MK + MLSTM_WIDTH_P
MO = MV + MLSTM_WIDTH
GT = MO + MLSTM_WIDTH
Z_WIDTH = GT + LANES
HIST_COLS = MV - PU
HIST_ROWS = 16
MIX_WIDTH = RET_WIDTH + POOL_WIDTH + MLSTM_WIDTH
NEG = -1e30

ROW_TILE = 512
SEQ_TILE = 256
VMEM_LIMIT = 56 * 1024 * 1024

(R_NORM_MIX, R_RET_GN, R_POOL_SCALE, R_CONV_B, R_MLSTM_GN, R_GATE_BIAS, R_NORM_FFN, R_NORM_PLE,
 R_NORM_FINAL, R_CONV_W) = range(10)
N_ROWS = 16


def _rmsnorm(x, g):
    return x * lax.rsqrt(jnp.mean(x * x, axis=-1, keepdims=True) + EPS) * g


def _sigmoid(x):
    return 1.0 / (1.0 + jnp.exp(-x))


def _silu(x):
    half = 0.5 * x
    return half + half * jnp.tanh(half)


def _dot(a, b):
    return jnp.dot(a, b, preferred_element_type=F32)


def _dot_nt(a, b):
    return lax.dot_general(a, b, (((1,), (1,)), ((), ())), preferred_element_type=F32)


TOKENS_PER_ROW = LANES // (RET_HEAD_DIM // 2)


def _trig_kernel(pos_ref, inv_ref, spread_ref, sign_ref, cos_ref, sin_ref):
    pos = pos_ref[...]
    ang = pos[:, 0:1] * inv_ref[0:1, :]
    for r in range(1, TOKENS_PER_ROW):
        ang = ang + pos[:, r:r + 1] * inv_ref[r:r + 1, :]
    n = ang.shape[0]
    for val, out_ref, sign in ((jnp.cos(ang), cos_ref, None), (jnp.sin(ang), sin_ref, sign_ref[...])):
        hi = val.astype(BF16)
        parts = jnp.concatenate([hi, (val - hi.astype(F32)).astype(BF16)], axis=1)
        for r in range(TOKENS_PER_ROW):
            wide = _dot(parts, spread_ref[r])
            out_ref[pl.ds(r, n, stride=TOKENS_PER_ROW), :] = wide if sign is None else wide * sign


def _trig_tables(positions):
    t = positions.size
    half = RET_HEAD_DIM // 2
    inv = ROPE_BASE ** (-jnp.arange(half, dtype=F32) / half)
    eye = jnp.asarray(np.eye(TOKENS_PER_ROW, dtype=np.float32))
    inv_rows = (eye[:, :, None] * inv[None, None, :]).reshape(TOKENS_PER_ROW, LANES)
    lanes = np.arange(LANES)
    spread = np.stack([(np.arange(LANES)[:, None] == half * r + lanes[None, :] % half) for r in range(TOKENS_PER_ROW)])
    spread = jnp.asarray(np.concatenate([spread, spread], axis=1).astype(np.float32)).astype(BF16)
    sign = np.where((lanes % RET_HEAD_DIM) < half, -1.0, 1.0).astype(np.float32).reshape(1, LANES)
    tile = min(4 * ROW_TILE, t)
    rows = tile // TOKENS_PER_ROW
    full = lambda a: pl.BlockSpec(a.shape, lambda i: (0,) * a.ndim)
    out = pl.BlockSpec((tile, LANES), lambda i: (i, 0))
    pos = positions.astype(F32).reshape(t // TOKENS_PER_ROW, TOKENS_PER_ROW)
    sign = jnp.asarray(sign)
    return pl.pallas_call(
        _trig_kernel,
        grid=(t // tile,),
        in_specs=[pl.BlockSpec((rows, TOKENS_PER_ROW), lambda i: (i, 0)), full(inv_rows), full(spread), full(sign)],
        out_specs=[out, out],
        out_shape=[jax.ShapeDtypeStruct((t, LANES), F32)] * 2,
        compiler_params=pltpu.CompilerParams(dimension_semantics=("arbitrary",)),
        name="rope_tables",
    )(pos, inv_rows, spread, sign)


def _layer_block(a, layer):
    return pl.BlockSpec((None,) + a.shape[1:], lambda *_: (layer,) + (0,) * (a.ndim - 1))


def _swap_halves(x, lo_half):
    return jnp.where(lo_half, pltpu.roll(x, LANES - RET_HEAD_DIM // 2, 1), pltpu.roll(x, RET_HEAD_DIM // 2, 1))


def _lane_scan(x, op, fill):
    lane = lax.broadcasted_iota(jnp.int32, x.shape, 1)
    sh = 1
    while sh < LANES:
        x = op(x, jnp.where(lane >= sh, pltpu.roll(x, sh, 1), fill))
        sh *= 2
    return x


def _mixer_kernel(x_ref, xnext_ref, cos_ref, sin_ref, dtab_ref, xi_ref, zeta_ref, cd_ref, bd_ref, avg_ref,
                  rows_ref, wpool_ref, wraw_ref,
                  mix_ref, win_ref, h_ref, z_even, z_odd, hbuf, rstate, cstate, mstate, *, seq_tile, tiles_per_seq):
    j = pl.program_id(0)
    tile_in_seq = jnp.maximum(j - 1, 0) % tiles_per_seq
    norm_in = lambda ref: _rmsnorm(ref[...], rows_ref[R_NORM_MIX:R_NORM_MIX + 1, :]).astype(BF16)

    @pl.when(j == 0)
    def _():
        h_ref[...] = norm_in(x_ref)
        def put(col, src, n):
            block = wraw_ref[src:src + n, :]
            if n < LANES:
                block = jnp.concatenate([block, jnp.zeros((LANES - n, D_MODEL), F32)], axis=0)
            win_ref[:, col:col + LANES] = block.T.astype(BF16)

        o = 4 * RET_WIDTH + POOL_WIDTH
        for c0 in range(0, o, LANES):
            put(c0, c0, LANES)
        for k in range(2 * MLSTM_HEADS):
            put(o + k * HEAD_PAD, o + k * MLSTM_HEAD_DIM, MLSTM_HEAD_DIM)
        src = o + 2 * MLSTM_WIDTH
        for c0 in range(0, 2 * MLSTM_WIDTH, LANES):
            put(MV + c0, src + c0, LANES)
        put(GT, src + 2 * MLSTM_WIDTH, 2 * MLSTM_HEADS)
        z_odd[...] = jnp.zeros_like(z_odd)

    @pl.when(tile_in_seq == 0)
    def _():
        hbuf[0:HIST_ROWS, :] = jnp.zeros((HIST_ROWS, HIST_COLS), F32)
        rstate[...] = jnp.zeros_like(rstate)
        cstate[...] = jnp.zeros_like(cstate)
        mstate[...] = jnp.zeros_like(mstate)

    def step(z_next, z_ref):
        def project(c0, c1):
            z_next[:, c0:c1] = _dot(h_ref[...], win_ref[:, c0:c1])

        slabs = [functools.partial(project, c0, min(c0 + PROJ_SLAB, Z_WIDTH)) for c0 in range(0, Z_WIDTH, PROJ_SLAB)]
        _mixer_tile(z_ref, cos_ref, sin_ref, dtab_ref, xi_ref, zeta_ref, cd_ref, bd_ref, avg_ref, rows_ref,
                    wpool_ref, mix_ref, hbuf, rstate, cstate, mstate, tile_in_seq, seq_tile,
                    _Interleave(slabs, (seq_tile // CHUNK) * STAGES_PER_CHUNK))
        h_ref[...] = norm_in(xnext_ref)

    @pl.when(j % 2 == 0)
    def _():
        step(z_even, z_odd)

    @pl.when(j % 2 == 1)
    def _():
        step(z_odd, z_even)


PROJ_SLAB = 256
STAGES_PER_CHUNK = 15


class _Interleave:
    def __init__(self, thunks, n_points):
        self.thunks, self.n_points, self.point, self.done = thunks, n_points, 0, 0

    def tick(self):
        self.point += 1
        due = len(self.thunks) if self.point >= self.n_points else (self.point * len(self.thunks)) // self.n_points
        while self.done < due:
            self.thunks[self.done]()
            self.done += 1

    def finish(self):
        self.point = self.n_points - 1
        self.tick()


def _mixer_tile(z_ref, cos_ref, sin_ref, dtab_ref, xi_ref, zeta_ref, cd_ref, bd_ref, avg_ref, rows_ref,
                wpool_ref, mix_ref, hbuf, rstate, cstate, mstate, j, seq_tile, other_work):
    vec = lambda r, n: rows_ref[r:r + 1, 0:n]
    hbuf[HIST_ROWS:, :] = z_ref[:, PU:MV]

    lane = lax.broadcasted_iota(jnp.int32, (CHUNK, LANES), 1)
    row_i = lax.broadcasted_iota(jnp.int32, (CHUNK, LANES), 0)
    lane_row = lax.broadcasted_iota(jnp.int32, (1, LANES), 1)
    in_a = (lane_row < RET_HEAD_DIM).astype(BF16)
    in_b = (lane_row >= RET_HEAD_DIM).astype(BF16)
    lo_half = (lane % RET_HEAD_DIM) < (RET_HEAD_DIM // 2)
    key_le_query = row_i <= lane
    lane_p = lax.broadcasted_iota(jnp.int32, (CHUNK, POOL_WIDTH), 1)
    row_p = lax.broadcasted_iota(jnp.int32, (CHUNK, POOL_WIDTH), 0)

    def pool_stages(r0, rows):
        ext = hbuf[r0:r0 + HIST_ROWS + CHUNK, 0:POOL_WIDTH]
        s2 = ext + pltpu.roll(ext, 1, 0)
        s4 = s2 + pltpu.roll(s2, 2, 0)
        s8 = s4 + pltpu.roll(s4, 4, 0)
        s16 = s8 + pltpu.roll(s8, 8, 0)
        u, s2, s4, s8, s16 = (a[HIST_ROWS:] for a in (ext, s2, s4, s8, s16))
        yield
        g0, g1, g2 = (lane_p < POOL_GROUP_DIM, lane_p < 2 * POOL_GROUP_DIM, lane_p < 3 * POOL_GROUP_DIM)
        wsum = jnp.where(g0, s2, jnp.where(g1, s4, jnp.where(g2, s8, s16)))
        width = jnp.where(g0, 2, jnp.where(g1, 4, jnp.where(g2, 8, 16)))
        tpos = row_p + (j * seq_tile + r0 + 1)
        count = jnp.minimum(tpos, width).astype(F32)
        pooled = wsum / count - u
        y_pool = _dot(pooled.astype(BF16), wpool_ref[...]) * vec(R_POOL_SCALE, POOL_WIDTH)
        mix_ref[rows, RET_WIDTH:RET_WIDTH + POOL_WIDTH] = y_pool.astype(BF16)

    def retention_stages(rows):
        pairs = range(RET_PAIRS)
        col = lambda base, p: z_ref[rows, base + p * LANES:base + (p + 1) * LANES]
        cosv = cos_ref[rows, :]
        sinv = sin_ref[rows, :]
        rope = lambda a: a * cosv + _swap_halves(a, lo_half) * sinv
        q = [rope(col(RQ, p)) for p in pairs]
        k = [rope(col(RK, p)) for p in pairs]
        yield
        kb = [k[p].astype(BF16) for p in pairs]
        vb = [col(RV, p).astype(BF16) for p in pairs]
        k2 = [jnp.concatenate([kb[p] * in_a, kb[p] * in_b], axis=0) for p in pairs]
        v2 = [jnp.concatenate([vb[p] * in_a, vb[p] * in_b], axis=0) for p in pairs]
        scores = [_dot_nt(q[p].astype(BF16), k2[p]) for p in pairs]
        kz_t = [(k[p] * zeta_ref[p]).T.astype(BF16) for p in pairs]
        yield
        lhs = [jnp.concatenate([scores[p].astype(BF16) * dtab_ref[p], (q[p] * xi_ref[p]).astype(BF16)], axis=1)
               for p in pairs]
        rhs = [jnp.concatenate([v2[p], rstate[p].astype(BF16)], axis=0) for p in pairs]
        ys = [_dot(lhs[p], rhs[p]) for p in pairs]
        kv = [_dot(kz_t[p], vb[p]) for p in pairs]
        for p in pairs:
            rstate[p] = rstate[p] * cd_ref[p] + kv[p] * bd_ref[...]
        yield
        y = jnp.concatenate(ys, axis=0)
        yc = y - _dot(y.astype(BF16), avg_ref[...])
        yield
        var = _dot((yc * yc).astype(BF16), avg_ref[...])
        yn = yc * lax.rsqrt(var + EPS)
        for p in pairs:
            cs = slice(p * LANES, (p + 1) * LANES)
            gain = rows_ref[R_RET_GN:R_RET_GN + 1, cs]
            mix_ref[rows, cs] = (_silu(col(RG, p)) * (yn[p * CHUNK:(p + 1) * CHUNK] * gain)).astype(BF16)

    def mlstm_stages(r0, rows):
        gates_t = (z_ref[rows, GT:GT + LANES] + vec(R_GATE_BIAS, LANES)).T
        li = gates_t[0:8]
        fpre = pltpu.roll(li, MLSTM_HEADS, 0)
        lf = jnp.minimum(fpre, 0.0) - jnp.log(1.0 + jnp.exp(-jnp.abs(fpre)))
        bcum = _lane_scan(lf, jnp.add, 0.0)
        g = li - bcum
        cmax = _lane_scan(g, jnp.maximum, NEG)
        m_prev = mstate[...]
        big_g = jnp.maximum(m_prev, cmax)
        inter = jnp.exp(m_prev - big_g)
        emr = jnp.exp(-(bcum + big_g))
        g_last = jnp.broadcast_to(big_g[:, LANES - 1:LANES], (8, LANES))
        b_last = jnp.broadcast_to(bcum[:, LANES - 1:LANES], (8, LANES))
        wk = jnp.exp(g - g_last)
        s_old = jnp.exp(m_prev - g_last)
        mstate[...] = b_last + g_last
        g_cols = jnp.concatenate([g, jnp.zeros((CHUNK - 8, LANES), F32)], axis=0).T
        yield
        ext = hbuf[r0 + HIST_ROWS - 8:r0 + HIST_ROWS + CHUNK, POOL_WIDTH:HIST_COLS]
        tap = lambda kk: vec(R_CONV_W + kk, 2 * MLSTM_WIDTH_P)
        prev = pltpu.roll(ext, 1, 0)
        older = pltpu.roll(tap(1) * ext + tap(0) * prev, 2, 0)
        yield
        conv = vec(R_CONV_B, 2 * MLSTM_WIDTH_P) + tap(3) * ext[8:] + tap(2) * prev[8:] + older[8:]
        qk = _silu(conv)
        yield
        v_all_t = z_ref[rows, MV:MV + MLSTM_WIDTH].T
        ones_row = jnp.where(lax.broadcasted_iota(jnp.int32, (8, LANES), 0) == 0, 1.0, 0.0)
        v_tail = jnp.concatenate([ones_row, jnp.zeros((HEAD_PAD - MLSTM_HEAD_DIM - 8, LANES), F32)], axis=0)
        heads = range(MLSTM_HEADS)
        row = lambda a, h: a[h:h + 1, :]
        qh = [(qk[:, h * HEAD_PAD:(h + 1) * HEAD_PAD] * (MLSTM_HEAD_DIM ** -0.5)).astype(BF16) for h in heads]
        kh = [qk[:, MLSTM_WIDTH_P + h * HEAD_PAD:MLSTM_WIDTH_P + (h + 1) * HEAD_PAD].astype(BF16) for h in heads]
        v_t = [jnp.concatenate([v_all_t[h * MLSTM_HEAD_DIM:(h + 1) * MLSTM_HEAD_DIM], v_tail], axis=0) for h in heads]
        c_t = [cstate[h] for h in heads]
        scores_t = [_dot_nt(kh[h], qh[h]) for h in heads]
        cross_t = [_dot_nt(c_t[h].astype(BF16), qh[h]) for h in heads]
        kv_t = [_dot((v_t[h] * row(wk, h)).astype(BF16), kh[h]) for h in heads]
        yield
        for h in heads:
            cstate[h] = c_t[h] * row(s_old, h) + kv_t[h]
        decay_t = [jnp.exp(jnp.where(key_le_query, g_cols[:, h:h + 1] - row(big_g, h), NEG)) for h in heads]
        sc_t = [(scores_t[h] * decay_t[h]).astype(BF16) for h in heads]
        yield
        nd_t = [_dot(v_t[h].astype(BF16), sc_t[h]) + cross_t[h] * row(inter, h) for h in heads]
        yield
        hn_ts = []
        for h in heads:
            den = nd_t[h][MLSTM_HEAD_DIM:MLSTM_HEAD_DIM + 1, :]
            h_t = nd_t[h][0:MLSTM_HEAD_DIM, :] * (1.0 / jnp.maximum(jnp.abs(den), row(emr, h)))
            mu = jnp.sum(h_t, axis=0, keepdims=True) / MLSTM_HEAD_DIM
            hc = h_t - mu
            var = jnp.sum(hc * hc, axis=0, keepdims=True) / MLSTM_HEAD_DIM
            hn_ts.append(hc * lax.rsqrt(var + EPS))
        yield
        hn = jnp.concatenate(hn_ts, axis=0).T
        o_gate = _sigmoid(z_ref[rows, MO:MO + MLSTM_WIDTH])
        mix_ref[rows, RET_WIDTH + POOL_WIDTH:MIX_WIDTH] = (
            o_gate * (hn * vec(R_MLSTM_GN, MLSTM_WIDTH))).astype(BF16)

    for c in range(seq_tile // CHUNK):
        r0 = c * CHUNK
        rows = pl.ds(r0, CHUNK)
        streams = [mlstm_stages(r0, rows), retention_stages(rows), pool_stages(r0, rows)]
        while streams:
            for s in list(streams):
                if next(s, StopIteration) is StopIteration:
                    streams.remove(s)
                other_work.tick()
    other_work.finish()

    hbuf[0:HIST_ROWS, :] = hbuf[seq_tile:seq_tile + HIST_ROWS, :]


def _retention_tables():
    lg = np.log1p(-(2.0 ** (-5.0 - np.arange(RET_HEADS, dtype=np.float64))))
    idx = np.arange(CHUNK, dtype=np.float64)
    rel = idx[:, None] - idx[None, :]
    head_of_lane = np.arange(LANES) // RET_HEAD_DIM
    dtab, xi, zeta, cd = [], [], [], []
    same = (head_of_lane[:, None] == head_of_lane[None, :])
    key_scale = RET_HEAD_DIM ** -0.5
    for p in range(RET_PAIRS):
        hl = lg[2 * p + head_of_lane]
        dtab.append(key_scale * np.concatenate(
            [np.where(rel >= 0, np.exp(lg[2 * p + a] * np.maximum(rel, 0.0)), 0.0) for a in range(2)], axis=1))
        xi.append(np.exp(hl[None, :] * (idx[:, None] + 1.0)))
        zeta.append(key_scale * np.exp(hl[None, :] * (CHUNK - 1 - idx[:, None])))
        cd.append(np.where(same, np.exp(hl * CHUNK)[:, None], 0.0))
    f = lambda a: jnp.asarray(np.stack(a).astype(np.float32))
    avg = jnp.asarray((same / RET_HEAD_DIM).astype(np.float32)).astype(BF16)
    return f(dtab).astype(BF16), f(xi), f(zeta), f(cd), jnp.asarray(same.astype(np.float32)), avg


def _mixers(x, cos_t, sin_t, tables, rows, wpool, w_in, layer, seq):
    t = x.shape[0]
    n_tiles = t // SEQ_TILE
    ahead = lambda k: pl.BlockSpec((SEQ_TILE, D_MODEL), lambda j: (jnp.minimum(j + k, n_tiles - 1), 0))
    tile = lambda w: pl.BlockSpec((SEQ_TILE, w), lambda j: (jnp.maximum(j - 1, 0), 0))
    full = lambda a: pl.BlockSpec(a.shape, lambda j: (0,) * a.ndim)
    resident = lambda a: pl.BlockSpec((None,) + a.shape[1:], lambda j: (layer,) + (0,) * (a.ndim - 1),
                                      pipeline_mode=pl.Buffered(1))
    consts = tuple(tables) + (rows, wpool, w_in)
    return pl.pallas_call(
        functools.partial(_mixer_kernel, seq_tile=SEQ_TILE, tiles_per_seq=seq // SEQ_TILE),
        grid=(n_tiles + 1,),
        in_specs=[ahead(0), ahead(1), tile(LANES), tile(LANES)] + [full(a) for a in tables]
        + [_layer_block(rows, layer), _layer_block(wpool, layer), resident(w_in)],
        out_specs=tile(MIX_WIDTH),
        out_shape=jax.ShapeDtypeStruct((t, MIX_WIDTH), BF16),
        scratch_shapes=[
            pltpu.VMEM((D_MODEL, Z_WIDTH), BF16),
            pltpu.VMEM((SEQ_TILE, D_MODEL), BF16),
            pltpu.VMEM((SEQ_TILE, Z_WIDTH), F32),
            pltpu.VMEM((SEQ_TILE, Z_WIDTH), F32),
            pltpu.VMEM((SEQ_TILE + HIST_ROWS, HIST_COLS), F32),
            pltpu.VMEM((RET_PAIRS, LANES, LANES), F32),
            pltpu.VMEM((MLSTM_HEADS, LANES, LANES), F32),
            pltpu.VMEM((8, LANES), F32),
        ],
        compiler_params=pltpu.CompilerParams(
            dimension_semantics=("arbitrary",), vmem_limit_bytes=VMEM_LIMIT),
        name="token_mixers",
    )(x, x, cos_t, sin_t, *consts)


FF_CHUNK = 512


def _channel_kernel(x_ref, mix_ref, p_ref, rows_ref, wout_ref, wgu_ref, wdown_ref,
                    wpg_ref, wpp_ref, o_ref, *, final):
    vec = lambda r: rows_ref[r:r + 1, :]
    x1 = x_ref[...] + _dot(mix_ref[...], wout_ref[...])
    h = _rmsnorm(x1, vec(R_NORM_FFN)).astype(BF16)
    acc = x1
    for c0 in range(0, D_FF, FF_CHUNK):
        c1 = min(c0 + FF_CHUNK, D_FF)
        gate = _dot(h, wgu_ref[:, c0:c1])
        up = _dot(h, wgu_ref[:, D_FF + c0:D_FF + c1])
        act = (_silu(gate) * up).astype(BF16)
        acc = acc + _dot(act, wdown_ref[c0:c1, :])
    hp = _rmsnorm(acc, vec(R_NORM_PLE)).astype(BF16)
    emb = _dot(p_ref[...].astype(BF16), wpp_ref[...])
    x3 = acc + _sigmoid(_dot(hp, wpg_ref[...])) * emb
    if final:
        x3 = _rmsnorm(x3, vec(R_NORM_FINAL))
    o_ref[...] = x3


def _channel(x, mix, p, rows, wout, wgu, wdown, wpg, wpp, layer, final):
    t = x.shape[0]
    tile = lambda w: pl.BlockSpec((ROW_TILE, w), lambda i: (i, 0))
    resident = lambda a: pl.BlockSpec((None,) + a.shape[1:], lambda i: (layer,) + (0,) * (a.ndim - 1),
                                      pipeline_mode=pl.Buffered(1))
    consts = (rows, wout, wgu, wdown, wpg, wpp)
    return pl.pallas_call(
        functools.partial(_channel_kernel, final=final),
        grid=(t // ROW_TILE,),
        in_specs=[tile(D_MODEL), tile(MIX_WIDTH),
                  pl.BlockSpec((None, ROW_TILE, PLE_DIM), lambda i: (layer, i, 0))]
        + [resident(a) for a in consts],
        out_specs=tile(D_MODEL),
        out_shape=jax.ShapeDtypeStruct((t, D_MODEL), F32),
        compiler_params=pltpu.CompilerParams(
            dimension_semantics=("arbitrary",), vmem_limit_bytes=VMEM_LIMIT),
        name="channel_mixing",
    )(x, mix, p, *consts)


def _head_segments(v):
    zeros = jnp.zeros(v.shape[:-1] + (HEAD_PAD - MLSTM_HEAD_DIM,), v.dtype)
    out = []
    for h in range(v.shape[-1] // MLSTM_HEAD_DIM):
        out += [v[..., h * MLSTM_HEAD_DIM:(h + 1) * MLSTM_HEAD_DIM], zeros]
    return out


def _block_diag(w):
    depth, g, d, _ = w.shape
    eye = jnp.asarray(np.eye(g, dtype=np.float32))
    return (w[:, :, :, None, :] * eye[None, :, None, :, None]).reshape(depth, g * d, g * d)


def _pack_rows(depth, rows):
    segs = []
    for row in rows:
        width = sum(s.shape[-1] for s in row)
        segs += list(row) + [jnp.zeros((depth, D_MODEL - width), F32)]
    segs.append(jnp.zeros((depth, (N_ROWS - len(rows)) * D_MODEL), F32))
    return jnp.concatenate(segs, axis=-1).reshape(depth, N_ROWS, D_MODEL)


def kernel(x, p, positions, norm_mix, w_in, ret_gn, pool_w, pool_scale, conv_w, conv_b, b_igate, b_fgate,
           mlstm_gn, w_out, norm_ffn, w_gate_up, w_down, norm_ple, w_ple_gate, w_ple_proj, norm_final):
    batch, seq, d = x.shape
    depth = w_in.shape[0]
    t = batch * seq
    xf = x.reshape(t, d)
    cos_t, sin_t = _trig_tables(positions)
    tables = _retention_tables()

    rows = _pack_rows(depth, [
        [norm_mix], [ret_gn], [pool_scale], _head_segments(conv_b), [mlstm_gn],
        [b_igate, b_fgate], [norm_ffn], [norm_ple], [jnp.broadcast_to(norm_final, (depth, d))],
    ] + [_head_segments(conv_w[:, k]) for k in range(MLSTM_CONV)])
    wpool = _block_diag(pool_w).astype(BF16)
    w_in_t = jnp.swapaxes(w_in, 1, 2)
    wout = w_out.astype(BF16)
    wgu, wdown = w_gate_up.astype(BF16), w_down.astype(BF16)
    wpg, wpp = w_ple_gate.astype(BF16), w_ple_proj.astype(BF16)
    pf = p.reshape(depth, t, PLE_DIM)

    for i in range(depth):
        mix = _mixers(xf, cos_t, sin_t, tables, rows, wpool, w_in_t, i, seq)
        xf = _channel(xf, mix, pf, rows, wout, wgu, wdown, wpg, wpp, i, final=(i == depth - 1))
    return xf.reshape(batch, seq, d)
```

```python
import functools

import numpy as np
import jax
import jax.numpy as jnp
from jax import lax
from jax.experimental import pallas as pl
from jax.experimental.pallas import tpu as pltpu

F32 = jnp.float32
BF16 = jnp.bfloat16

D_MODEL = 1024
PLE_DIM = 256
RET_HEADS = 6
RET_HEAD_DIM = 64
RET_WIDTH = RET_HEADS * RET_HEAD_DIM
RET_PAIRS = RET_HEADS // 2
POOL_WINDOWS = (2, 4, 8, 16)
POOL_GROUP_DIM = 64
POOL_WIDTH = len(POOL_WINDOWS) * POOL_GROUP_DIM
MLSTM_HEADS = 4
MLSTM_HEAD_DIM = 96
MLSTM_WIDTH = MLSTM_HEADS * MLSTM_HEAD_DIM
MLSTM_CONV = 4
CHUNK = 128
D_FF = 2816
ROPE_BASE = 10000.0
EPS = 1e-6

LANES = 128
HEAD_PAD = LANES
MLSTM_WIDTH_P = MLSTM_HEADS * HEAD_PAD

RQ = 0
RK = RQ + RET_WIDTH
RV = RK + RET_WIDTH
RG = RV + RET_WIDTH
PU = RG + RET_WIDTH
MQ = PU + POOL_WIDTH
MK = MQ + MLSTM_WIDTH_P
MV = MK + MLSTM_WIDTH_P
MO = MV + MLSTM_WIDTH
GT = MO + MLSTM_WIDTH
Z_WIDTH = GT + LANES
HIST_COLS = MV - PU
HIST_ROWS = 16
MIX_WIDTH = RET_WIDTH + POOL_WIDTH + MLSTM_WIDTH
NEG = -1e30

ROW_TILE = 512
SEQ_TILE = 512
VMEM_LIMIT = 56 * 1024 * 1024

(R_NORM_MIX, R_RET_GN, R_POOL_SCALE, R_CONV_B, R_MLSTM_GN, R_GATE_BIAS, R_NORM_FFN, R_NORM_PLE,
 R_NORM_FINAL, R_CONV_W) = range(10)
N_ROWS = 16


def _rmsnorm(x, g):
    return x * lax.rsqrt(jnp.mean(x * x, axis=-1, keepdims=True) + EPS) * g


def _sigmoid(x):
    return 1.0 / (1.0 + jnp.exp(-x))


def _silu(x):
    half = 0.5 * x
    return half + half * jnp.tanh(half)


def _dot(a, b):
    return jnp.dot(a, b, preferred_element_type=F32)


def _dot_nt(a, b):
    return lax.dot_general(a, b, (((1,), (1,)), ((), ())), preferred_element_type=F32)


TOKENS_PER_ROW = LANES // (RET_HEAD_DIM // 2)


def _trig_kernel(pos_ref, inv_ref, spread_ref, sign_ref, cos_ref, sin_ref):
    pos = pos_ref[...]
    ang = pos[:, 0:1] * inv_ref[0:1, :]
    for r in range(1, TOKENS_PER_ROW):
        ang = ang + pos[:, r:r + 1] * inv_ref[r:r + 1, :]
    n = ang.shape[0]
    for val, out_ref, sign in ((jnp.cos(ang), cos_ref, None), (jnp.sin(ang), sin_ref, sign_ref[...])):
        hi = val.astype(BF16)
        parts = jnp.concatenate([hi, (val - hi.astype(F32)).astype(BF16)], axis=1)
        for r in range(TOKENS_PER_ROW):
            wide = _dot(parts, spread_ref[r])
            out_ref[pl.ds(r, n, stride=TOKENS_PER_ROW), :] = wide if sign is None else wide * sign


def _trig_tables(positions):
    t = positions.size
    half = RET_HEAD_DIM // 2
    inv = ROPE_BASE ** (-jnp.arange(half, dtype=F32) / half)
    eye = jnp.asarray(np.eye(TOKENS_PER_ROW, dtype=np.float32))
    inv_rows = (eye[:, :, None] * inv[None, None, :]).reshape(TOKENS_PER_ROW, LANES)
    lanes = np.arange(LANES)
    spread = np.stack([(np.arange(LANES)[:, None] == half * r + lanes[None, :] % half) for r in range(TOKENS_PER_ROW)])
    spread = jnp.asarray(np.concatenate([spread, spread], axis=1).astype(np.float32)).astype(BF16)
    sign = np.where((lanes % RET_HEAD_DIM) < half, -1.0, 1.0).astype(np.float32).reshape(1, LANES)
    tile = min(4 * ROW_TILE, t)
    rows = tile // TOKENS_PER_ROW
    full = lambda a: pl.BlockSpec(a.shape, lambda i: (0,) * a.ndim)
    out = pl.BlockSpec((tile, LANES), lambda i: (i, 0))
    pos = positions.astype(F32).reshape(t // TOKENS_PER_ROW, TOKENS_PER_ROW)
    sign = jnp.asarray(sign)
    return pl.pallas_call(
        _trig_kernel,
        grid=(t // tile,),
        in_specs=[pl.BlockSpec((rows, TOKENS_PER_ROW), lambda i: (i, 0)), full(inv_rows), full(spread), full(sign)],
        out_specs=[out, out],
        out_shape=[jax.ShapeDtypeStruct((t, LANES), F32)] * 2,
        compiler_params=pltpu.CompilerParams(dimension_semantics=("arbitrary",)),
        name="rope_tables",
    )(pos, inv_rows, spread, sign)


def _layer_block(a, layer):
    return pl.BlockSpec((None,) + a.shape[1:], lambda *_: (layer,) + (0,) * (a.ndim - 1))


def _swap_halves(x, lo_half):
    return jnp.where(lo_half, pltpu.roll(x, LANES - RET_HEAD_DIM // 2, 1), pltpu.roll(x, RET_HEAD_DIM // 2, 1))


def _lane_scan(x, op, fill):
    lane = lax.broadcasted_iota(jnp.int32, x.shape, 1)
    sh = 1
    while sh < LANES:
        x = op(x, jnp.where(lane >= sh, pltpu.roll(x, sh, 1), fill))
        sh *= 2
    return x


def _mixer_kernel(x_ref, xnext_ref, cos_ref, sin_ref, dtab_ref, xi_ref, zeta_ref, cd_ref, bd_ref, avg_ref,
                  rows_ref, wpool_ref, wraw_ref,
                  mix_ref, win_ref, h_ref, z_even, z_odd, hbuf, rstate, cstate, mstate, *, seq_tile, tiles_per_seq):
    j = pl.program_id(0)
    tile_in_seq = jnp.maximum(j - 1, 0) % tiles_per_seq
    norm_in = lambda ref: _rmsnorm(ref[...], rows_ref[R_NORM_MIX:R_NORM_MIX + 1, :]).astype(BF16)

    @pl.when(j == 0)
    def _():
        h_ref[...] = norm_in(x_ref)
        def put(col, src, n):
            block = wraw_ref[src:src + n, :]
            if n < LANES:
                block = jnp.concatenate([block, jnp.zeros((LANES - n, D_MODEL), F32)], axis=0)
            win_ref[:, col:col + LANES] = block.T.astype(BF16)

        o = 4 * RET_WIDTH + POOL_WIDTH
        for c0 in range(0, o, LANES):
            put(c0, c0, LANES)
        for k in range(2 * MLSTM_HEADS):
            put(o + k * HEAD_PAD, o + k * MLSTM_HEAD_DIM, MLSTM_HEAD_DIM)
        src = o + 2 * MLSTM_WIDTH
        for c0 in range(0, 2 * MLSTM_WIDTH, LANES):
            put(MV + c0, src + c0, LANES)
        put(GT, src + 2 * MLSTM_WIDTH, 2 * MLSTM_HEADS)
        z_odd[...] = jnp.zeros_like(z_odd)

    @pl.when(tile_in_seq == 0)
    def _():
        hbuf[0:HIST_ROWS, :] = jnp.zeros((HIST_ROWS, HIST_COLS), F32)
        rstate[...] = jnp.zeros_like(rstate)
        cstate[...] = jnp.zeros_like(cstate)
        mstate[...] = jnp.zeros_like(mstate)

    def step(z_next, z_ref):
        def project(c0, c1):
            z_next[:, c0:c1] = _dot(h_ref[...], win_ref[:, c0:c1])

        slabs = [functools.partial(project, c0, min(c0 + PROJ_SLAB, Z_WIDTH)) for c0 in range(0, Z_WIDTH, PROJ_SLAB)]
        _mixer_tile(z_ref, cos_ref, sin_ref, dtab_ref, xi_ref, zeta_ref, cd_ref, bd_ref, avg_ref, rows_ref,
                    wpool_ref, mix_ref, hbuf, rstate, cstate, mstate, tile_in_seq, seq_tile,
                    _Interleave(slabs, (seq_tile // CHUNK) * STAGES_PER_CHUNK))
        h_ref[...] = norm_in(xnext_ref)

    @pl.when(j % 2 == 0)
    def _():
        step(z_even, z_odd)

    @pl.when(j % 2 == 1)
    def _():
        step(z_odd, z_even)


PROJ_SLAB = 256
STAGES_PER_CHUNK = 15


class _Interleave:
    def __init__(self, thunks, n_points):
        self.thunks, self.n_points, self.point, self.done = thunks, n_points, 0, 0

    def tick(self):
        self.point += 1
        due = len(self.thunks) if self.point >= self.n_points else (self.point * len(self.thunks)) // self.n_points
        while self.done < due:
            self.thunks[self.done]()
            self.done += 1

    def finish(self):
        self.point = self.n_points - 1
        self.tick()


def _mixer_tile(z_ref, cos_ref, sin_ref, dtab_ref, xi_ref, zeta_ref, cd_ref, bd_ref, avg_ref, rows_ref,
                wpool_ref, mix_ref, hbuf, rstate, cstate, mstate, j, seq_tile, other_work):
    vec = lambda r, n: rows_ref[r:r + 1, 0:n]
    hbuf[HIST_ROWS:, :] = z_ref[:, PU:MV]

    lane = lax.broadcasted_iota(jnp.int32, (CHUNK, LANES), 1)
    row_i = lax.broadcasted_iota(jnp.int32, (CHUNK, LANES), 0)
    lane_row = lax.broadcasted_iota(jnp.int32, (1, LANES), 1)
    in_a = (lane_row < RET_HEAD_DIM).astype(BF16)
    in_b = (lane_row >= RET_HEAD_DIM).astype(BF16)
    lo_half = (lane % RET_HEAD_DIM) < (RET_HEAD_DIM // 2)
    key_le_query = row_i <= lane
    lane_p = lax.broadcasted_iota(jnp.int32, (CHUNK, POOL_WIDTH), 1)
    row_p = lax.broadcasted_iota(jnp.int32, (CHUNK, POOL_WIDTH), 0)

    def pool_stages(r0, rows):
        ext = hbuf[r0:r0 + HIST_ROWS + CHUNK, 0:POOL_WIDTH]
        s2 = ext + pltpu.roll(ext, 1, 0)
        s4 = s2 + pltpu.roll(s2, 2, 0)
        s8 = s4 + pltpu.roll(s4, 4, 0)
        s16 = s8 + pltpu.roll(s8, 8, 0)
        u, s2, s4, s8, s16 = (a[HIST_ROWS:] for a in (ext, s2, s4, s8, s16))
        yield
        g0, g1, g2 = (lane_p < POOL_GROUP_DIM, lane_p < 2 * POOL_GROUP_DIM, lane_p < 3 * POOL_GROUP_DIM)
        wsum = jnp.where(g0, s2, jnp.where(g1, s4, jnp.where(g2, s8, s16)))
        width = jnp.where(g0, 2, jnp.where(g1, 4, jnp.where(g2, 8, 16)))
        tpos = row_p + (j * seq_tile + r0 + 1)
        count = jnp.minimum(tpos, width).astype(F32)
        pooled = wsum / count - u
        y_pool = _dot(pooled.astype(BF16), wpool_ref[...]) * vec(R_POOL_SCALE, POOL_WIDTH)
        mix_ref[rows, RET_WIDTH:RET_WIDTH + POOL_WIDTH] = y_pool.astype(BF16)

    def retention_stages(rows):
        pairs = range(RET_PAIRS)
        col = lambda base, p: z_ref[rows, base + p * LANES:base + (p + 1) * LANES]
        cosv = cos_ref[rows, :]
        sinv = sin_ref[rows, :]
        rope = lambda a: a * cosv + _swap_halves(a, lo_half) * sinv
        q = [rope(col(RQ, p)) for p in pairs]
        k = [rope(col(RK, p)) for p in pairs]
        yield
        kb = [k[p].astype(BF16) for p in pairs]
        vb = [col(RV, p).astype(BF16) for p in pairs]
        k2 = [jnp.concatenate([kb[p] * in_a, kb[p] * in_b], axis=0) for p in pairs]
        v2 = [jnp.concatenate([vb[p] * in_a, vb[p] * in_b], axis=0) for p in pairs]
        scores = [_dot_nt(q[p].astype(BF16), k2[p]) for p in pairs]
        kz_t = [(k[p] * zeta_ref[p]).T.astype(BF16) for p in pairs]
        yield
        lhs = [jnp.concatenate([scores[p].astype(BF16) * dtab_ref[p], (q[p] * xi_ref[p]).astype(BF16)], axis=1)
               for p in pairs]
        rhs = [jnp.concatenate([v2[p], rstate[p].astype(BF16)], axis=0) for p in pairs]
        ys = [_dot(lhs[p], rhs[p]) for p in pairs]
        kv = [_dot(kz_t[p], vb[p]) for p in pairs]
        for p in pairs:
            rstate[p] = rstate[p] * cd_ref[p] + kv[p] * bd_ref[...]
        yield
        y = jnp.concatenate(ys, axis=0)
        yc = y - _dot(y.astype(BF16), avg_ref[...])
        yield
        var = _dot((yc * yc).astype(BF16), avg_ref[...])
        yn = yc * lax.rsqrt(var + EPS)
        for p in pairs:
            cs = slice(p * LANES, (p + 1) * LANES)
            gain = rows_ref[R_RET_GN:R_RET_GN + 1, cs]
            mix_ref[rows, cs] = (_silu(col(RG, p)) * (yn[p * CHUNK:(p + 1) * CHUNK] * gain)).astype(BF16)

    def mlstm_stages(r0, rows):
        gates_t = (z_ref[rows, GT:GT + LANES] + vec(R_GATE_BIAS, LANES)).T
        li = gates_t[0:8]
        fpre = pltpu.roll(li, MLSTM_HEADS, 0)
        lf = jnp.minimum(fpre, 0.0) - jnp.log(1.0 + jnp.exp(-jnp.abs(fpre)))
        bcum = _lane_scan(lf, jnp.add, 0.0)
        g = li - bcum
        cmax = _lane_scan(g, jnp.maximum, NEG)
        m_prev = mstate[...]
        big_g = jnp.maximum(m_prev, cmax)
        inter = jnp.exp(m_prev - big_g)
        emr = jnp.exp(-(bcum + big_g))
        g_last = jnp.broadcast_to(big_g[:, LANES - 1:LANES], (8, LANES))
        b_last = jnp.broadcast_to(bcum[:, LANES - 1:LANES], (8, LANES))
        wk = jnp.exp(g - g_last)
        s_old = jnp.exp(m_prev - g_last)
        mstate[...] = b_last + g_last
        g_cols = jnp.concatenate([g, jnp.zeros((CHUNK - 8, LANES), F32)], axis=0).T
        yield
        ext = hbuf[r0 + HIST_ROWS - 8:r0 + HIST_ROWS + CHUNK, POOL_WIDTH:HIST_COLS]
        tap = lambda kk: vec(R_CONV_W + kk, 2 * MLSTM_WIDTH_P)
        prev = pltpu.roll(ext, 1, 0)
        older = pltpu.roll(tap(1) * ext + tap(0) * prev, 2, 0)
        yield
        conv = vec(R_CONV_B, 2 * MLSTM_WIDTH_P) + tap(3) * ext[8:] + tap(2) * prev[8:] + older[8:]
        qk = _silu(conv)
        yield
        v_all_t = z_ref[rows, MV:MV + MLSTM_WIDTH].T
        ones_row = jnp.where(lax.broadcasted_iota(jnp.int32, (8, LANES), 0) == 0, 1.0, 0.0)
        v_tail = jnp.concatenate([ones_row, jnp.zeros((HEAD_PAD - MLSTM_HEAD_DIM - 8, LANES), F32)], axis=0)
        heads = range(MLSTM_HEADS)
        row = lambda a, h: a[h:h + 1, :]
        qh = [(qk[:, h * HEAD_PAD:(h + 1) * HEAD_PAD] * (MLSTM_HEAD_DIM ** -0.5)).astype(BF16) for h in heads]
        kh = [qk[:, MLSTM_WIDTH_P + h * HEAD_PAD:MLSTM_WIDTH_P + (h + 1) * HEAD_PAD].astype(BF16) for h in heads]
        v_t = [jnp.concatenate([v_all_t[h * MLSTM_HEAD_DIM:(h + 1) * MLSTM_HEAD_DIM], v_tail], axis=0) for h in heads]
        c_t = [cstate[h] for h in heads]
        scores_t = [_dot_nt(kh[h], qh[h]) for h in heads]
        cross_t = [_dot_nt(c_t[h].astype(BF16), qh[h]) for h in heads]
        kv_t = [_dot((v_t[h] * row(wk, h)).astype(BF16), kh[h]) for h in heads]
        yield
        for h in heads:
            cstate[h] = c_t[h] * row(s_old, h) + kv_t[h]
        decay_t = [jnp.exp(jnp.where(key_le_query, g_cols[:, h:h + 1] - row(big_g, h), NEG)) for h in heads]
        sc_t = [(scores_t[h] * decay_t[h]).astype(BF16) for h in heads]
        yield
        nd_t = [_dot(v_t[h].astype(BF16), sc_t[h]) + cross_t[h] * row(inter, h) for h in heads]
        yield
        hn_ts = []
        for h in heads:
            den = nd_t[h][MLSTM_HEAD_DIM:MLSTM_HEAD_DIM + 1, :]
            h_t = nd_t[h][0:MLSTM_HEAD_DIM, :] * (1.0 / jnp.maximum(jnp.abs(den), row(emr, h)))
            mu = jnp.sum(h_t, axis=0, keepdims=True) / MLSTM_HEAD_DIM
            hc = h_t - mu
            var = jnp.sum(hc * hc, axis=0, keepdims=True) / MLSTM_HEAD_DIM
            hn_ts.append(hc * lax.rsqrt(var + EPS))
        yield
        hn = jnp.concatenate(hn_ts, axis=0).T
        o_gate = _sigmoid(z_ref[rows, MO:MO + MLSTM_WIDTH])
        mix_ref[rows, RET_WIDTH + POOL_WIDTH:MIX_WIDTH] = (
            o_gate * (hn * vec(R_MLSTM_GN, MLSTM_WIDTH))).astype(BF16)

    for c in range(seq_tile // CHUNK):
        r0 = c * CHUNK
        rows = pl.ds(r0, CHUNK)
        streams = [mlstm_stages(r0, rows), retention_stages(rows), pool_stages(r0, rows)]
        while streams:
            for s in list(streams):
                if next(s, StopIteration) is StopIteration:
                    streams.remove(s)
                other_work.tick()
    other_work.finish()

    hbuf[0:HIST_ROWS, :] = hbuf[seq_tile:seq_tile + HIST_ROWS, :]


def _retention_tables():
    lg = np.log1p(-(2.0 ** (-5.0 - np.arange(RET_HEADS, dtype=np.float64))))
    idx = np.arange(CHUNK, dtype=np.float64)
    rel = idx[:, None] - idx[None, :]
    head_of_lane = np.arange(LANES) // RET_HEAD_DIM
    dtab, xi, zeta, cd = [], [], [], []
    same = (head_of_lane[:, None] == head_of_lane[None, :])
    key_scale = RET_HEAD_DIM ** -0.5
    for p in range(RET_PAIRS):
        hl = lg[2 * p + head_of_lane]
        dtab.append(key_scale * np.concatenate(
            [np.where(rel >= 0, np.exp(lg[2 * p + a] * np.maximum(rel, 0.0)), 0.0) for a in range(2)], axis=1))
        xi.append(np.exp(hl[None, :] * (idx[:, None] + 1.0)))
        zeta.append(key_scale * np.exp(hl[None, :] * (CHUNK - 1 - idx[:, None])))
        cd.append(np.where(same, np.exp(hl * CHUNK)[:, None], 0.0))
    f = lambda a: jnp.asarray(np.stack(a).astype(np.float32))
    avg = jnp.asarray((same / RET_HEAD_DIM).astype(np.float32)).astype(BF16)
    return f(dtab).astype(BF16), f(xi), f(zeta), f(cd), jnp.asarray(same.astype(np.float32)), avg


def _mixers(x, cos_t, sin_t, tables, rows, wpool, w_in, layer, seq):
    t = x.shape[0]
    n_tiles = t // SEQ_TILE
    ahead = lambda k: pl.BlockSpec((SEQ_TILE, D_MODEL), lambda j: (jnp.minimum(j + k, n_tiles - 1), 0))
    tile = lambda w: pl.BlockSpec((SEQ_TILE, w), lambda j: (jnp.maximum(j - 1, 0), 0))
    full = lambda a: pl.BlockSpec(a.shape, lambda j: (0,) * a.ndim)
    resident = lambda a: pl.BlockSpec((None,) + a.shape[1:], lambda j: (layer,) + (0,) * (a.ndim - 1),
                                      pipeline_mode=pl.Buffered(1))
    consts = tuple(tables) + (rows, wpool, w_in)
    return pl.pallas_call(
        functools.partial(_mixer_kernel, seq_tile=SEQ_TILE, tiles_per_seq=seq // SEQ_TILE),
        grid=(n_tiles + 1,),
        in_specs=[ahead(0), ahead(1), tile(LANES), tile(LANES)] + [full(a) for a in tables]
        + [_layer_block(rows, layer), _layer_block(wpool, layer), resident(w_in)],
        out_specs=tile(MIX_WIDTH),
        out_shape=jax.ShapeDtypeStruct((t, MIX_WIDTH), BF16),
        scratch_shapes=[
            pltpu.VMEM((D_MODEL, Z_WIDTH), BF16),
            pltpu.VMEM((SEQ_TILE, D_MODEL), BF16),
            pltpu.VMEM((SEQ_TILE, Z_WIDTH), F32),
            pltpu.VMEM((SEQ_TILE, Z_WIDTH), F32),
            pltpu.VMEM((SEQ_TILE + HIST_ROWS, HIST_COLS), F32),
            pltpu.VMEM((RET_PAIRS, LANES, LANES), F32),
            pltpu.VMEM((MLSTM_HEADS, LANES, LANES), F32),
            pltpu.VMEM((8, LANES), F32),
        ],
        compiler_params=pltpu.CompilerParams(
            dimension_semantics=("arbitrary",), vmem_limit_bytes=VMEM_LIMIT),
        name="token_mixers",
    )(x, x, cos_t, sin_t, *consts)


FF_CHUNK = 512


def _channel_kernel(x_ref, mix_ref, p_ref, rows_ref, wout_ref, wgu_ref, wdown_ref,
                    wpg_ref, wpp_ref, o_ref, *, final):
    vec = lambda r: rows_ref[r:r + 1, :]
    x1 = x_ref[...] + _dot(mix_ref[...], wout_ref[...])
    h = _rmsnorm(x1, vec(R_NORM_FFN)).astype(BF16)
    acc = x1
    for c0 in range(0, D_FF, FF_CHUNK):
        c1 = min(c0 + FF_CHUNK, D_FF)
        gate = _dot(h, wgu_ref[:, c0:c1])
        up = _dot(h, wgu_ref[:, D_FF + c0:D_FF + c1])
        act = (_silu(gate) * up).astype(BF16)
        acc = acc + _dot(act, wdown_ref[c0:c1, :])
    hp = _rmsnorm(acc, vec(R_NORM_PLE)).astype(BF16)
    emb = _dot(p_ref[...].astype(BF16), wpp_ref[...])
    x3 = acc + _sigmoid(_dot(hp, wpg_ref[...])) * emb
    if final:
        x3 = _rmsnorm(x3, vec(R_NORM_FINAL))
    o_ref[...] = x3


def _channel(x, mix, p, rows, wout, wgu, wdown, wpg, wpp, layer, final):
    t = x.shape[0]
    tile = lambda w: pl.BlockSpec((ROW_TILE, w), lambda i: (i, 0))
    resident = lambda a: pl.BlockSpec((None,) + a.shape[1:], lambda i: (layer,) + (0,) * (a.ndim - 1),
                                      pipeline_mode=pl.Buffered(1))
    consts = (rows, wout, wgu, wdown, wpg, wpp)
    return pl.pallas_call(
        functools.partial(_channel_kernel, final=final),
        grid=(t // ROW_TILE,),
        in_specs=[tile(D_MODEL), tile(MIX_WIDTH),
                  pl.BlockSpec((None, ROW_TILE, PLE_DIM), lambda i: (layer, i, 0))]
        + [resident(a) for a in consts],
        out_specs=tile(D_MODEL),
        out_shape=jax.ShapeDtypeStruct((t, D_MODEL), F32),
        compiler_params=pltpu.CompilerParams(
            dimension_semantics=("arbitrary",), vmem_limit_bytes=VMEM_LIMIT),
        name="channel_mixing",
    )(x, mix, p, *consts)


def _head_segments(v):
    zeros = jnp.zeros(v.shape[:-1] + (HEAD_PAD - MLSTM_HEAD_DIM,), v.dtype)
    out = []
    for h in range(v.shape[-1] // MLSTM_HEAD_DIM):
        out += [v[..., h * MLSTM_HEAD_DIM:(h + 1) * MLSTM_HEAD_DIM], zeros]
    return out


def _block_diag(w):
    depth, g, d, _ = w.shape
    eye = jnp.asarray(np.eye(g, dtype=np.float32))
    return (w[:, :, :, None, :] * eye[None, :, None, :, None]).reshape(depth, g * d, g * d)


def _pack_rows(depth, rows):
    segs = []
    for row in rows:
        width = sum(s.shape[-1] for s in row)
        segs += list(row) + [jnp.zeros((depth, D_MODEL - width), F32)]
    segs.append(jnp.zeros((depth, (N_ROWS - len(rows)) * D_MODEL), F32))
    return jnp.concatenate(segs, axis=-1).reshape(depth, N_ROWS, D_MODEL)


def kernel(x, p, positions, norm_mix, w_in, ret_gn, pool_w, pool_scale, conv_w, conv_b, b_igate, b_fgate,
           mlstm_gn, w_out, norm_ffn, w_gate_up, w_down, norm_ple, w_ple_gate, w_ple_proj, norm_final):
    batch, seq, d = x.shape
    depth = w_in.shape[0]
    t = batch * seq
    xf = x.reshape(t, d)
    cos_t, sin_t = _trig_tables(positions)
    tables = _retention_tables()

    rows = _pack_rows(depth, [
        [norm_mix], [ret_gn], [pool_scale], _head_segments(conv_b), [mlstm_gn],
        [b_igate, b_fgate], [norm_ffn], [norm_ple], [jnp.broadcast_to(norm_final, (depth, d))],
    ] + [_head_segments(conv_w[:, k]) for k in range(MLSTM_CONV)])
    wpool = _block_diag(pool_w).astype(BF16)
    w_in_t = jnp.swapaxes(w_in, 1, 2)
    wout = w_out.astype(BF16)
    wgu, wdown = w_gate_up.astype(BF16), w_down.astype(BF16)
    wpg, wpp = w_ple_gate.astype(BF16), w_ple_proj.astype(BF16)
    pf = p.reshape(depth, t, PLE_DIM)

    for i in range(depth):
        mix = _mixers(xf, cos_t, sin_t, tables, rows, wpool, w_in_t, i, seq)
        xf = _channel(xf, mix, pf, rows, wout, wgu, wdown, wpg, wpp, i, final=(i == depth - 1))
    return xf.reshape(batch, seq, d)
```

```python
import functools

import numpy as np
import jax
import jax.numpy as jnp
from jax import lax
from jax.experimental import pallas as pl
from jax.experimental.pallas import tpu as pltpu

F32 = jnp.float32
BF16 = jnp.bfloat16

D_MODEL = 1024
PLE_DIM = 256
RET_HEADS = 6
RET_HEAD_DIM = 64
RET_WIDTH = RET_HEADS * RET_HEAD_DIM
RET_PAIRS = RET_HEADS // 2
POOL_WINDOWS = (2, 4, 8, 16)
POOL_GROUP_DIM = 64
POOL_WIDTH = len(POOL_WINDOWS) * POOL_GROUP_DIM
MLSTM_HEADS = 4
MLSTM_HEAD_DIM = 96
MLSTM_WIDTH = MLSTM_HEADS * MLSTM_HEAD_DIM
MLSTM_CONV = 4
CHUNK = 128
D_FF = 2816
ROPE_BASE = 10000.0
EPS = 1e-6

LANES = 128
HEAD_PAD = LANES
MLSTM_WIDTH_P = MLSTM_HEADS * HEAD_PAD

RQ = 0
RK = RQ + RET_WIDTH
RV = RK + RET_WIDTH
RG = RV + RET_WIDTH
PU = RG + RET_WIDTH
MQ = PU + POOL_WIDTH
MK = MQ + MLSTM_WIDTH_P
MV = MK + MLSTM_WIDTH_P
MO = MV + MLSTM_WIDTH
GT = MO + MLSTM_WIDTH
Z_WIDTH = GT + LANES
HIST_COLS = MV - PU
HIST_ROWS = 16
MIX_WIDTH = RET_WIDTH + POOL_WIDTH + MLSTM_WIDTH
NEG = -1e30

ROW_TILE = 512
SEQ_TILE = 512
VMEM_LIMIT = 56 * 1024 * 1024

(R_NORM_MIX, R_RET_GN, R_POOL_SCALE, R_CONV_B, R_MLSTM_GN, R_GATE_BIAS, R_NORM_FFN, R_NORM_PLE,
 R_NORM_FINAL, R_CONV_W) = range(10)
N_ROWS = 16


def _rmsnorm(x, g):
    return x * lax.rsqrt(jnp.mean(x * x, axis=-1, keepdims=True) + EPS) * g


def _sigmoid(x):
    return 1.0 / (1.0 + jnp.exp(-x))


def _silu(x):
    half = 0.5 * x
    return half + half * jnp.tanh(half)


def _dot(a, b):
    return jnp.dot(a, b, preferred_element_type=F32)


def _dot_nt(a, b):
    return lax.dot_general(a, b, (((1,), (1,)), ((), ())), preferred_element_type=F32)


TOKENS_PER_ROW = LANES // (RET_HEAD_DIM // 2)


def _trig_kernel(pos_ref, inv_ref, spread_ref, sign_ref, cos_ref, sin_ref):
    pos = pos_ref[...]
    ang = pos[:, 0:1] * inv_ref[0:1, :]
    for r in range(1, TOKENS_PER_ROW):
        ang = ang + pos[:, r:r + 1] * inv_ref[r:r + 1, :]
    n = ang.shape[0]
    for val, out_ref, sign in ((jnp.cos(ang), cos_ref, None), (jnp.sin(ang), sin_ref, sign_ref[...])):
        hi = val.astype(BF16)
        parts = jnp.concatenate([hi, (val - hi.astype(F32)).astype(BF16)], axis=1)
        for r in range(TOKENS_PER_ROW):
            wide = _dot(parts, spread_ref[r])
            out_ref[pl.ds(r, n, stride=TOKENS_PER_ROW), :] = wide if sign is None else wide * sign


def _trig_tables(positions):
    t = positions.size
    half = RET_HEAD_DIM // 2
    inv = ROPE_BASE ** (-jnp.arange(half, dtype=F32) / half)
    eye = jnp.asarray(np.eye(TOKENS_PER_ROW, dtype=np.float32))
    inv_rows = (eye[:, :, None] * inv[None, None, :]).reshape(TOKENS_PER_ROW, LANES)
    lanes = np.arange(LANES)
    spread = np.stack([(np.arange(LANES)[:, None] == half * r + lanes[None, :] % half) for r in range(TOKENS_PER_ROW)])
    spread = jnp.asarray(np.concatenate([spread, spread], axis=1).astype(np.float32)).astype(BF16)
    sign = np.where((lanes % RET_HEAD_DIM) < half, -1.0, 1.0).astype(np.float32).reshape(1, LANES)
    tile = min(4 * ROW_TILE, t)
    rows = tile // TOKENS_PER_ROW
    full = lambda a: pl.BlockSpec(a.shape, lambda i: (0,) * a.ndim)
    out = pl.BlockSpec((tile, LANES), lambda i: (i, 0))
    pos = positions.astype(F32).reshape(t // TOKENS_PER_ROW, TOKENS_PER_ROW)
    sign = jnp.asarray(sign)
    return pl.pallas_call(
        _trig_kernel,
        grid=(t // tile,),
        in_specs=[pl.BlockSpec((rows, TOKENS_PER_ROW), lambda i: (i, 0)), full(inv_rows), full(spread), full(sign)],
        out_specs=[out, out],
        out_shape=[jax.ShapeDtypeStruct((t, LANES), F32)] * 2,
        compiler_params=pltpu.CompilerParams(dimension_semantics=("arbitrary",)),
        name="rope_tables",
    )(pos, inv_rows, spread, sign)


def _layer_block(a, layer):
    return pl.BlockSpec((None,) + a.shape[1:], lambda *_: (layer,) + (0,) * (a.ndim - 1))


def _swap_halves(x, lo_half):
    return jnp.where(lo_half, pltpu.roll(x, LANES - RET_HEAD_DIM // 2, 1), pltpu.roll(x, RET_HEAD_DIM // 2, 1))


def _lane_scan(x, op, fill):
    lane = lax.broadcasted_iota(jnp.int32, x.shape, 1)
    sh = 1
    while sh < LANES:
        x = op(x, jnp.where(lane >= sh, pltpu.roll(x, sh, 1), fill))
        sh *= 2
    return x


def _mixer_kernel(x_ref, xnext_ref, cos_ref, sin_ref, dtab_ref, xi_ref, zeta_ref, cd_ref, bd_ref, avg_ref,
                  rows_ref, wpool_ref, wraw_ref,
                  mix_ref, win_ref, h_ref, z_even, z_odd, hbuf, rstate, cstate, mstate, *, seq_tile, tiles_per_seq):
    j = pl.program_id(0)
    tile_in_seq = jnp.maximum(j - 1, 0) % tiles_per_seq
    norm_in = lambda ref: _rmsnorm(ref[...], rows_ref[R_NORM_MIX:R_NORM_MIX + 1, :]).astype(BF16)

    @pl.when(j == 0)
    def _():
        h_ref[...] = norm_in(x_ref)
        def put(col, src, n):
            block = wraw_ref[src:src + n, :]
            if n < LANES:
                block = jnp.concatenate([block, jnp.zeros((LANES - n, D_MODEL), F32)], axis=0)
            win_ref[:, col:col + LANES] = block.T.astype(BF16)

        o = 4 * RET_WIDTH + POOL_WIDTH
        for c0 in range(0, o, LANES):
            put(c0, c0, LANES)
        for k in range(2 * MLSTM_HEADS):
            put(o + k * HEAD_PAD, o + k * MLSTM_HEAD_DIM, MLSTM_HEAD_DIM)
        src = o + 2 * MLSTM_WIDTH
        for c0 in range(0, 2 * MLSTM_WIDTH, LANES):
            put(MV + c0, src + c0, LANES)
        put(GT, src + 2 * MLSTM_WIDTH, 2 * MLSTM_HEADS)
        z_odd[...] = jnp.zeros_like(z_odd)

    @pl.when(tile_in_seq == 0)
    def _():
        hbuf[0:HIST_ROWS, :] = jnp.zeros((HIST_ROWS, HIST_COLS), F32)
        rstate[...] = jnp.zeros_like(rstate)
        cstate[...] = jnp.zeros_like(cstate)
        mstate[...] = jnp.zeros_like(mstate)

    def step(z_next, z_ref):
        def project(c0, c1):
            z_next[:, c0:c1] = _dot(h_ref[...], win_ref[:, c0:c1])

        slabs = [functools.partial(project, c0, min(c0 + PROJ_SLAB, Z_WIDTH)) for c0 in range(0, Z_WIDTH, PROJ_SLAB)]
        _mixer_tile(z_ref, cos_ref, sin_ref, dtab_ref, xi_ref, zeta_ref, cd_ref, bd_ref, avg_ref, rows_ref,
                    wpool_ref, mix_ref, hbuf, rstate, cstate, mstate, tile_in_seq, seq_tile,
                    _Interleave(slabs, (seq_tile // CHUNK) * STAGES_PER_CHUNK))
        h_ref[...] = norm_in(xnext_ref)

    @pl.when(j % 2 == 0)
    def _():
        step(z_even, z_odd)

    @pl.when(j % 2 == 1)
    def _():
        step(z_odd, z_even)


PROJ_SLAB = 512
STAGES_PER_CHUNK = 15


class _Interleave:
    def __init__(self, thunks, n_points):
        self.thunks, self.n_points, self.point, self.done = thunks, n_points, 0, 0

    def tick(self):
        self.point += 1
        due = len(self.thunks) if self.point >= self.n_points else (self.point * len(self.thunks)) // self.n_points
        while self.done < due:
            self.thunks[self.done]()
            self.done += 1

    def finish(self):
        self.point = self.n_points - 1
        self.tick()


def _mixer_tile(z_ref, cos_ref, sin_ref, dtab_ref, xi_ref, zeta_ref, cd_ref, bd_ref, avg_ref, rows_ref,
                wpool_ref, mix_ref, hbuf, rstate, cstate, mstate, j, seq_tile, other_work):
    vec = lambda r, n: rows_ref[r:r + 1, 0:n]
    hbuf[HIST_ROWS:, :] = z_ref[:, PU:MV]

    lane = lax.broadcasted_iota(jnp.int32, (CHUNK, LANES), 1)
    row_i = lax.broadcasted_iota(jnp.int32, (CHUNK, LANES), 0)
    lane_row = lax.broadcasted_iota(jnp.int32, (1, LANES), 1)
    in_a = (lane_row < RET_HEAD_DIM).astype(BF16)
    in_b = (lane_row >= RET_HEAD_DIM).astype(BF16)
    lo_half = (lane % RET_HEAD_DIM) < (RET_HEAD_DIM // 2)
    key_le_query = row_i <= lane
    lane_p = lax.broadcasted_iota(jnp.int32, (CHUNK, POOL_WIDTH), 1)
    row_p = lax.broadcasted_iota(jnp.int32, (CHUNK, POOL_WIDTH), 0)

    def pool_stages(r0, rows):
        ext = hbuf[r0:r0 + HIST_ROWS + CHUNK, 0:POOL_WIDTH]
        s2 = ext + pltpu.roll(ext, 1, 0)
        s4 = s2 + pltpu.roll(s2, 2, 0)
        s8 = s4 + pltpu.roll(s4, 4, 0)
        s16 = s8 + pltpu.roll(s8, 8, 0)
        u, s2, s4, s8, s16 = (a[HIST_ROWS:] for a in (ext, s2, s4, s8, s16))
        yield
        g0, g1, g2 = (lane_p < POOL_GROUP_DIM, lane_p < 2 * POOL_GROUP_DIM, lane_p < 3 * POOL_GROUP_DIM)
        wsum = jnp.where(g0, s2, jnp.where(g1, s4, jnp.where(g2, s8, s16)))
        width = jnp.where(g0, 2, jnp.where(g1, 4, jnp.where(g2, 8, 16)))
        tpos = row_p + (j * seq_tile + r0 + 1)
        count = jnp.minimum(tpos, width).astype(F32)
        pooled = wsum / count - u
        y_pool = _dot(pooled.astype(BF16), wpool_ref[...]) * vec(R_POOL_SCALE, POOL_WIDTH)
        mix_ref[rows, RET_WIDTH:RET_WIDTH + POOL_WIDTH] = y_pool.astype(BF16)

    def retention_stages(rows):
        pairs = range(RET_PAIRS)
        col = lambda base, p: z_ref[rows, base + p * LANES:base + (p + 1) * LANES]
        cosv = cos_ref[rows, :]
        sinv = sin_ref[rows, :]
        rope = lambda a: a * cosv + _swap_halves(a, lo_half) * sinv
        q = [rope(col(RQ, p)) for p in pairs]
        k = [rope(col(RK, p)) for p in pairs]
        yield
        kb = [k[p].astype(BF16) for p in pairs]
        vb = [col(RV, p).astype(BF16) for p in pairs]
        k2 = [jnp.concatenate([kb[p] * in_a, kb[p] * in_b], axis=0) for p in pairs]
        v2 = [jnp.concatenate([vb[p] * in_a, vb[p] * in_b], axis=0) for p in pairs]
        scores = [_dot_nt(q[p].astype(BF16), k2[p]) for p in pairs]
        kz_t = [(k[p] * zeta_ref[p]).T.astype(BF16) for p in pairs]
        yield
        lhs = [jnp.concatenate([scores[p].astype(BF16) * dtab_ref[p], (q[p] * xi_ref[p]).astype(BF16)], axis=1)
               for p in pairs]
        rhs = [jnp.concatenate([v2[p], rstate[p].astype(BF16)], axis=0) for p in pairs]
        ys = [_dot(lhs[p], rhs[p]) for p in pairs]
        kv = [_dot(kz_t[p], vb[p]) for p in pairs]
        for p in pairs:
            rstate[p] = rstate[p] * cd_ref[p] + kv[p] * bd_ref[...]
        yield
        y = jnp.concatenate(ys, axis=0)
        yc = y - _dot(y.astype(BF16), avg_ref[...])
        yield
        var = _dot((yc * yc).astype(BF16), avg_ref[...])
        yn = yc * lax.rsqrt(var + EPS)
        for p in pairs:
            cs = slice(p * LANES, (p + 1) * LANES)
            gain = rows_ref[R_RET_GN:R_RET_GN + 1, cs]
            mix_ref[rows, cs] = (_silu(col(RG, p)) * (yn[p * CHUNK:(p + 1) * CHUNK] * gain)).astype(BF16)

    def mlstm_stages(r0, rows):
        gates_t = (z_ref[rows, GT:GT + LANES] + vec(R_GATE_BIAS, LANES)).T
        li = gates_t[0:8]
        fpre = pltpu.roll(li, MLSTM_HEADS, 0)
        lf = jnp.minimum(fpre, 0.0) - jnp.log(1.0 + jnp.exp(-jnp.abs(fpre)))
        bcum = _lane_scan(lf, jnp.add, 0.0)
        g = li - bcum
        cmax = _lane_scan(g, jnp.maximum, NEG)
        m_prev = mstate[...]
        big_g = jnp.maximum(m_prev, cmax)
        inter = jnp.exp(m_prev - big_g)
        emr = jnp.exp(-(bcum + big_g))
        g_last = jnp.broadcast_to(big_g[:, LANES - 1:LANES], (8, LANES))
        b_last = jnp.broadcast_to(bcum[:, LANES - 1:LANES], (8, LANES))
        wk = jnp.exp(g - g_last)
        s_old = jnp.exp(m_prev - g_last)
        mstate[...] = b_last + g_last
        g_cols = jnp.concatenate([g, jnp.zeros((CHUNK - 8, LANES), F32)], axis=0).T
        yield
        ext = hbuf[r0 + HIST_ROWS - 8:r0 + HIST_ROWS + CHUNK, POOL_WIDTH:HIST_COLS]
        tap = lambda kk: vec(R_CONV_W + kk, 2 * MLSTM_WIDTH_P)
        prev = pltpu.roll(ext, 1, 0)
        older = pltpu.roll(tap(1) * ext + tap(0) * prev, 2, 0)
        yield
        conv = vec(R_CONV_B, 2 * MLSTM_WIDTH_P) + tap(3) * ext[8:] + tap(2) * prev[8:] + older[8:]
        qk = _silu(conv)
        yield
        v_all_t = z_ref[rows, MV:MV + MLSTM_WIDTH].T
        ones_row = jnp.where(lax.broadcasted_iota(jnp.int32, (8, LANES), 0) == 0, 1.0, 0.0)
        v_tail = jnp.concatenate([ones_row, jnp.zeros((HEAD_PAD - MLSTM_HEAD_DIM - 8, LANES), F32)], axis=0)
        heads = range(MLSTM_HEADS)
        row = lambda a, h: a[h:h + 1, :]
        qh = [(qk[:, h * HEAD_PAD:(h + 1) * HEAD_PAD] * (MLSTM_HEAD_DIM ** -0.5)).astype(BF16) for h in heads]
        kh = [qk[:, MLSTM_WIDTH_P + h * HEAD_PAD:MLSTM_WIDTH_P + (h + 1) * HEAD_PAD].astype(BF16) for h in heads]
        v_t = [jnp.concatenate([v_all_t[h * MLSTM_HEAD_DIM:(h + 1) * MLSTM_HEAD_DIM], v_tail], axis=0) for h in heads]
        c_t = [cstate[h] for h in heads]
        scores_t = [_dot_nt(kh[h], qh[h]) for h in heads]
        cross_t = [_dot_nt(c_t[h].astype(BF16), qh[h]) for h in heads]
        kv_t = [_dot((v_t[h] * row(wk, h)).astype(BF16), kh[h]) for h in heads]
        yield
        for h in heads:
            cstate[h] = c_t[h] * row(s_old, h) + kv_t[h]
        decay_t = [jnp.exp(jnp.where(key_le_query, g_cols[:, h:h + 1] - row(big_g, h), NEG)) for h in heads]
        sc_t = [(scores_t[h] * decay_t[h]).astype(BF16) for h in heads]
        yield
        nd_t = [_dot(v_t[h].astype(BF16), sc_t[h]) + cross_t[h] * row(inter, h) for h in heads]
        yield
        hn_ts = []
        for h in heads:
            den = nd_t[h][MLSTM_HEAD_DIM:MLSTM_HEAD_DIM + 1, :]
            h_t = nd_t[h][0:MLSTM_HEAD_DIM, :] * (1.0 / jnp.maximum(jnp.abs(den), row(emr, h)))
            mu = jnp.sum(h_t, axis=0, keepdims=True) / MLSTM_HEAD_DIM
            hc = h_t - mu
            var = jnp.sum(hc * hc, axis=0, keepdims=True) / MLSTM_HEAD_DIM
            hn_ts.append(hc * lax.rsqrt(var + EPS))
        yield
        hn = jnp.concatenate(hn_ts, axis=0).T
        o_gate = _sigmoid(z_ref[rows, MO:MO + MLSTM_WIDTH])
        mix_ref[rows, RET_WIDTH + POOL_WIDTH:MIX_WIDTH] = (
            o_gate * (hn * vec(R_MLSTM_GN, MLSTM_WIDTH))).astype(BF16)

    for c in range(seq_tile // CHUNK):
        r0 = c * CHUNK
        rows = pl.ds(r0, CHUNK)
        streams = [mlstm_stages(r0, rows), retention_stages(rows), pool_stages(r0, rows)]
        while streams:
            for s in list(streams):
                if next(s, StopIteration) is StopIteration:
                    streams.remove(s)
                other_work.tick()
    other_work.finish()

    hbuf[0:HIST_ROWS, :] = hbuf[seq_tile:seq_tile + HIST_ROWS, :]


def _retention_tables():
    lg = np.log1p(-(2.0 ** (-5.0 - np.arange(RET_HEADS, dtype=np.float64))))
    idx = np.arange(CHUNK, dtype=np.float64)
    rel = idx[:, None] - idx[None, :]
    head_of_lane = np.arange(LANES) // RET_HEAD_DIM
    dtab, xi, zeta, cd = [], [], [], []
    same = (head_of_lane[:, None] == head_of_lane[None, :])
    key_scale = RET_HEAD_DIM ** -0.5
    for p in range(RET_PAIRS):
        hl = lg[2 * p + head_of_lane]
        dtab.append(key_scale * np.concatenate(
            [np.where(rel >= 0, np.exp(lg[2 * p + a] * np.maximum(rel, 0.0)), 0.0) for a in range(2)], axis=1))
        xi.append(np.exp(hl[None, :] * (idx[:, None] + 1.0)))
        zeta.append(key_scale * np.exp(hl[None, :] * (CHUNK - 1 - idx[:, None])))
        cd.append(np.where(same, np.exp(hl * CHUNK)[:, None], 0.0))
    f = lambda a: jnp.asarray(np.stack(a).astype(np.float32))
    avg = jnp.asarray((same / RET_HEAD_DIM).astype(np.float32)).astype(BF16)
    return f(dtab).astype(BF16), f(xi), f(zeta), f(cd), jnp.asarray(same.astype(np.float32)), avg


def _mixers(x, cos_t, sin_t, tables, rows, wpool, w_in, layer, seq):
    t = x.shape[0]
    n_tiles = t // SEQ_TILE
    ahead = lambda k: pl.BlockSpec((SEQ_TILE, D_MODEL), lambda j: (jnp.minimum(j + k, n_tiles - 1), 0))
    tile = lambda w: pl.BlockSpec((SEQ_TILE, w), lambda j: (jnp.maximum(j - 1, 0), 0))
    full = lambda a: pl.BlockSpec(a.shape, lambda j: (0,) * a.ndim)
    resident = lambda a: pl.BlockSpec((None,) + a.shape[1:], lambda j: (layer,) + (0,) * (a.ndim - 1),
                                      pipeline_mode=pl.Buffered(1))
    consts = tuple(tables) + (rows, wpool, w_in)
    return pl.pallas_call(
        functools.partial(_mixer_kernel, seq_tile=SEQ_TILE, tiles_per_seq=seq // SEQ_TILE),
        grid=(n_tiles + 1,),
        in_specs=[ahead(0), ahead(1), tile(LANES), tile(LANES)] + [full(a) for a in tables]
        + [_layer_block(rows, layer), _layer_block(wpool, layer), resident(w_in)],
        out_specs=tile(MIX_WIDTH),
        out_shape=jax.ShapeDtypeStruct((t, MIX_WIDTH), BF16),
        scratch_shapes=[
            pltpu.VMEM((D_MODEL, Z_WIDTH), BF16),
            pltpu.VMEM((SEQ_TILE, D_MODEL), BF16),
            pltpu.VMEM((SEQ_TILE, Z_WIDTH), F32),
            pltpu.VMEM((SEQ_TILE, Z_WIDTH), F32),
            pltpu.VMEM((SEQ_TILE + HIST_ROWS, HIST_COLS), F32),
            pltpu.VMEM((RET_PAIRS, LANES, LANES), F32),
            pltpu.VMEM((MLSTM_HEADS, LANES, LANES), F32),
            pltpu.VMEM((8, LANES), F32),
        ],
        compiler_params=pltpu.CompilerParams(
            dimension_semantics=("arbitrary",), vmem_limit_bytes=VMEM_LIMIT),
        name="token_mixers",
    )(x, x, cos_t, sin_t, *consts)


FF_CHUNK = 512


def _channel_kernel(x_ref, mix_ref, p_ref, rows_ref, wout_ref, wgu_ref, wdown_ref,
                    wpg_ref, wpp_ref, o_ref, *, final):
    vec = lambda r: rows_ref[r:r + 1, :]
    x1 = x_ref[...] + _dot(mix_ref[...], wout_ref[...])
    h = _rmsnorm(x1, vec(R_NORM_FFN)).astype(BF16)
    acc = x1
    for c0 in range(0, D_FF, FF_CHUNK):
        c1 = min(c0 + FF_CHUNK, D_FF)
        gate = _dot(h, wgu_ref[:, c0:c1])
        up = _dot(h, wgu_ref[:, D_FF + c0:D_FF + c1])
        act = (_silu(gate) * up).astype(BF16)
        acc = acc + _dot(act, wdown_ref[c0:c1, :])
    hp = _rmsnorm(acc, vec(R_NORM_PLE)).astype(BF16)
    emb = _dot(p_ref[...].astype(BF16), wpp_ref[...])
    x3 = acc + _sigmoid(_dot(hp, wpg_ref[...])) * emb
    if final:
        x3 = _rmsnorm(x3, vec(R_NORM_FINAL))
    o_ref[...] = x3


def _channel(x, mix, p, rows, wout, wgu, wdown, wpg, wpp, layer, final):
    t = x.shape[0]
    tile = lambda w: pl.BlockSpec((ROW_TILE, w), lambda i: (i, 0))
    resident = lambda a: pl.BlockSpec((None,) + a.shape[1:], lambda i: (layer,) + (0,) * (a.ndim - 1),
                                      pipeline_mode=pl.Buffered(1))
    consts = (rows, wout, wgu, wdown, wpg, wpp)
    return pl.pallas_call(
        functools.partial(_channel_kernel, final=final),
        grid=(t // ROW_TILE,),
        in_specs=[tile(D_MODEL), tile(MIX_WIDTH),
                  pl.BlockSpec((None, ROW_TILE, PLE_DIM), lambda i: (layer, i, 0))]
        + [resident(a) for a in consts],
        out_specs=tile(D_MODEL),
        out_shape=jax.ShapeDtypeStruct((t, D_MODEL), F32),
        compiler_params=pltpu.CompilerParams(
            dimension_semantics=("arbitrary",), vmem_limit_bytes=VMEM_LIMIT),
        name="channel_mixing",
    )(x, mix, p, *consts)


def _head_segments(v):
    zeros = jnp.zeros(v.shape[:-1] + (HEAD_PAD - MLSTM_HEAD_DIM,), v.dtype)
    out = []
    for h in range(v.shape[-1] // MLSTM_HEAD_DIM):
        out += [v[..., h * MLSTM_HEAD_DIM:(h + 1) * MLSTM_HEAD_DIM], zeros]
    return out


def _block_diag(w):
    depth, g, d, _ = w.shape
    eye = jnp.asarray(np.eye(g, dtype=np.float32))
    return (w[:, :, :, None, :] * eye[None, :, None, :, None]).reshape(depth, g * d, g * d)


def _pack_rows(depth, rows):
    segs = []
    for row in rows:
        width = sum(s.shape[-1] for s in row)
        segs += list(row) + [jnp.zeros((depth, D_MODEL - width), F32)]
    segs.append(jnp.zeros((depth, (N_ROWS - len(rows)) * D_MODEL), F32))
    return jnp.concatenate(segs, axis=-1).reshape(depth, N_ROWS, D_MODEL)


def kernel(x, p, positions, norm_mix, w_in, ret_gn, pool_w, pool_scale, conv_w, conv_b, b_igate, b_fgate,
           mlstm_gn, w_out, norm_ffn, w_gate_up, w_down, norm_ple, w_ple_gate, w_ple_proj, norm_final):
    batch, seq, d = x.shape
    depth = w_in.shape[0]
    t = batch * seq
    xf = x.reshape(t, d)
    cos_t, sin_t = _trig_tables(positions)
    tables = _retention_tables()

    rows = _pack_rows(depth, [
        [norm_mix], [ret_gn], [pool_scale], _head_segments(conv_b), [mlstm_gn],
        [b_igate, b_fgate], [norm_ffn], [norm_ple], [jnp.broadcast_to(norm_final, (depth, d))],
    ] + [_head_segments(conv_w[:, k]) for k in range(MLSTM_CONV)])
    wpool = _block_diag(pool_w).astype(BF16)
    w_in_t = jnp.swapaxes(w_in, 1, 2)
    wout = w_out.astype(BF16)
    wgu, wdown = w_gate_up.astype(BF16), w_down.astype(BF16)
    wpg, wpp = w_ple_gate.astype(BF16), w_ple_proj.astype(BF16)
    pf = p.reshape(depth, t, PLE_DIM)

    for i in range(depth):
        mix = _mixers(xf, cos_t, sin_t, tables, rows, wpool, w_in_t, i, seq)
        xf = _channel(xf, mix, pf, rows, wout, wgu, wdown, wpg, wpp, i, final=(i == depth - 1))
    return xf.reshape(batch, seq, d)
```

```python
import functools

import numpy as np
import jax
import jax.numpy as jnp
from jax import lax
from jax.experimental import pallas as pl
from jax.experimental.pallas import tpu as pltpu

F32 = jnp.float32
BF16 = jnp.bfloat16

D_MODEL = 1024
PLE_DIM = 256
RET_HEADS = 6
RET_HEAD_DIM = 64
RET_WIDTH = RET_HEADS * RET_HEAD_DIM
RET_PAIRS = RET_HEADS // 2
POOL_WINDOWS = (2, 4, 8, 16)
POOL_GROUP_DIM = 64
POOL_WIDTH = len(POOL_WINDOWS) * POOL_GROUP_DIM
MLSTM_HEADS = 4
MLSTM_HEAD_DIM = 96
MLSTM_WIDTH = MLSTM_HEADS * MLSTM_HEAD_DIM
MLSTM_CONV = 4
CHUNK = 128
D_FF = 2816
ROPE_BASE = 10000.0
EPS = 1e-6

LANES = 128
HEAD_PAD = LANES
MLSTM_WIDTH_P = MLSTM_HEADS * HEAD_PAD

RQ = 0
RK = RQ + RET_WIDTH
RV = RK + RET_WIDTH
RG = RV + RET_WIDTH
PU = RG + RET_WIDTH
MQ = PU + POOL_WIDTH
MK = MQ + MLSTM_WIDTH_P
MV = MK + MLSTM_WIDTH_P
MO = MV + MLSTM_WIDTH
GATE_LANE = MLSTM_HEAD_DIM
Z_WIDTH = MO + MLSTM_WIDTH
HIST_COLS = MV - PU
HIST_ROWS = 16
MIX_WIDTH = RET_WIDTH + POOL_WIDTH + MLSTM_WIDTH
NEG = -1e30

ROW_TILE = 512
SEQ_TILE = 512
VMEM_LIMIT = 56 * 1024 * 1024

(R_NORM_MIX, R_RET_GN, R_POOL_SCALE, R_CONV_B, R_MLSTM_GN, R_GATE_BIAS, R_NORM_FFN, R_NORM_PLE,
 R_NORM_FINAL, R_CONV_W) = range(10)
N_ROWS = 16


def _rmsnorm(x, g):
    return x * lax.rsqrt(jnp.mean(x * x, axis=-1, keepdims=True) + EPS) * g


def _sigmoid(x):
    return 1.0 / (1.0 + jnp.exp(-x))


def _silu(x):
    half = 0.5 * x
    return half + half * jnp.tanh(half)


def _dot(a, b):
    return jnp.dot(a, b, preferred_element_type=F32)


def _dot_nt(a, b):
    return lax.dot_general(a, b, (((1,), (1,)), ((), ())), preferred_element_type=F32)


TOKENS_PER_ROW = LANES // (RET_HEAD_DIM // 2)


def _trig_kernel(pos_ref, inv_ref, spread_ref, sign_ref, cos_ref, sin_ref):
    pos = pos_ref[...]
    ang = pos[:, 0:1] * inv_ref[0:1, :]
    for r in range(1, TOKENS_PER_ROW):
        ang = ang + pos[:, r:r + 1] * inv_ref[r:r + 1, :]
    n = ang.shape[0]
    for val, out_ref, sign in ((jnp.cos(ang), cos_ref, None), (jnp.sin(ang), sin_ref, sign_ref[...])):
        hi = val.astype(BF16)
        parts = jnp.concatenate([hi, (val - hi.astype(F32)).astype(BF16)], axis=1)
        for r in range(TOKENS_PER_ROW):
            wide = _dot(parts, spread_ref[r])
            out_ref[pl.ds(r, n, stride=TOKENS_PER_ROW), :] = wide if sign is None else wide * sign


def _trig_tables(positions):
    t = positions.size
    half = RET_HEAD_DIM // 2
    inv = ROPE_BASE ** (-jnp.arange(half, dtype=F32) / half)
    eye = jnp.asarray(np.eye(TOKENS_PER_ROW, dtype=np.float32))
    inv_rows = (eye[:, :, None] * inv[None, None, :]).reshape(TOKENS_PER_ROW, LANES)
    lanes = np.arange(LANES)
    spread = np.stack([(np.arange(LANES)[:, None] == half * r + lanes[None, :] % half) for r in range(TOKENS_PER_ROW)])
    spread = jnp.asarray(np.concatenate([spread, spread], axis=1).astype(np.float32)).astype(BF16)
    sign = np.where((lanes % RET_HEAD_DIM) < half, -1.0, 1.0).astype(np.float32).reshape(1, LANES)
    tile = min(4 * ROW_TILE, t)
    rows = tile // TOKENS_PER_ROW
    full = lambda a: pl.BlockSpec(a.shape, lambda i: (0,) * a.ndim)
    out = pl.BlockSpec((tile, LANES), lambda i: (i, 0))
    pos = positions.astype(F32).reshape(t // TOKENS_PER_ROW, TOKENS_PER_ROW)
    sign = jnp.asarray(sign)
    return pl.pallas_call(
        _trig_kernel,
        grid=(t // tile,),
        in_specs=[pl.BlockSpec((rows, TOKENS_PER_ROW), lambda i: (i, 0)), full(inv_rows), full(spread), full(sign)],
        out_specs=[out, out],
        out_shape=[jax.ShapeDtypeStruct((t, LANES), F32)] * 2,
        compiler_params=pltpu.CompilerParams(dimension_semantics=("arbitrary",)),
        name="rope_tables",
    )(pos, inv_rows, spread, sign)


def _layer_block(a, layer):
    return pl.BlockSpec((None,) + a.shape[1:], lambda *_: (layer,) + (0,) * (a.ndim - 1))


def _swap_halves(x, lo_half):
    return jnp.where(lo_half, pltpu.roll(x, LANES - RET_HEAD_DIM // 2, 1), pltpu.roll(x, RET_HEAD_DIM // 2, 1))


def _lane_scan(x, op, fill):
    lane = lax.broadcasted_iota(jnp.int32, x.shape, 1)
    sh = 1
    while sh < LANES:
        x = op(x, jnp.where(lane >= sh, pltpu.roll(x, sh, 1), fill))
        sh *= 2
    return x


def _mixer_kernel(x_ref, xnext_ref, cos_ref, sin_ref, dtab_ref, xi_ref, zeta_ref, cd_ref, bd_ref, avg_ref,
                  rows_ref, wpool_ref, wraw_ref,
                  mix_ref, win_ref, h_ref, z_even, z_odd, hbuf, rstate, cstate, mstate, *, seq_tile, tiles_per_seq):
    j = pl.program_id(0)
    tile_in_seq = jnp.maximum(j - 1, 0) % tiles_per_seq
    norm_in = lambda ref: _rmsnorm(ref[...], rows_ref[R_NORM_MIX:R_NORM_MIX + 1, :]).astype(BF16)

    @pl.when(j == 0)
    def _():
        h_ref[...] = norm_in(x_ref)
        def put(col, parts):
            rows = [wraw_ref[src:src + n, :] for src, n in parts]
            used = sum(n for _, n in parts)
            if used < LANES:
                rows.append(jnp.zeros((LANES - used, D_MODEL), F32))
            win_ref[:, col:col + LANES] = jnp.concatenate(rows, axis=0).T.astype(BF16)

        o = 4 * RET_WIDTH + POOL_WIDTH
        gates_src = o + 4 * MLSTM_WIDTH
        for c0 in range(0, o, LANES):
            put(c0, [(c0, LANES)])
        for k in range(2 * MLSTM_HEADS):
            extra = [(gates_src, 2 * MLSTM_HEADS)] if k == 0 else []
            put(o + k * HEAD_PAD, [(o + k * MLSTM_HEAD_DIM, MLSTM_HEAD_DIM)] + extra)
        src = o + 2 * MLSTM_WIDTH
        for c0 in range(0, 2 * MLSTM_WIDTH, LANES):
            put(MV + c0, [(src + c0, LANES)])
        z_odd[...] = jnp.zeros_like(z_odd)

    @pl.when(tile_in_seq == 0)
    def _():
        hbuf[0:HIST_ROWS, :] = jnp.zeros((HIST_ROWS, HIST_COLS), F32)
        rstate[...] = jnp.zeros_like(rstate)
        cstate[...] = jnp.zeros_like(cstate)
        mstate[...] = jnp.zeros_like(mstate)

    def step(z_next, z_ref):
        def project(c0, c1):
            z_next[:, c0:c1] = _dot(h_ref[...], win_ref[:, c0:c1])

        slabs = [functools.partial(project, c0, min(c0 + PROJ_SLAB, Z_WIDTH)) for c0 in range(0, Z_WIDTH, PROJ_SLAB)]
        _mixer_tile(z_ref, cos_ref, sin_ref, dtab_ref, xi_ref, zeta_ref, cd_ref, bd_ref, avg_ref, rows_ref,
                    wpool_ref, mix_ref, hbuf, rstate, cstate, mstate, tile_in_seq, seq_tile,
                    _Interleave(slabs, (seq_tile // CHUNK) * STAGES_PER_CHUNK))
        h_ref[...] = norm_in(xnext_ref)

    @pl.when(j % 2 == 0)
    def _():
        step(z_even, z_odd)

    @pl.when(j % 2 == 1)
    def _():
        step(z_odd, z_even)


PROJ_SLAB = 512
STAGES_PER_CHUNK = 15


class _Interleave:
    def __init__(self, thunks, n_points):
        self.thunks, self.n_points, self.point, self.done = thunks, n_points, 0, 0

    def tick(self):
        self.point += 1
        count = len(self.thunks)
        due = count if self.point >= self.n_points else min(count, 1 + ((self.point - 1) * count) // self.n_points)
        while self.done < due:
            self.thunks[self.done]()
            self.done += 1

    def finish(self):
        self.point = self.n_points - 1
        self.tick()


def _mixer_tile(z_ref, cos_ref, sin_ref, dtab_ref, xi_ref, zeta_ref, cd_ref, bd_ref, avg_ref, rows_ref,
                wpool_ref, mix_ref, hbuf, rstate, cstate, mstate, j, seq_tile, other_work):
    vec = lambda r, n: rows_ref[r:r + 1, 0:n]
    hbuf[HIST_ROWS:, :] = z_ref[:, PU:MV]

    lane = lax.broadcasted_iota(jnp.int32, (CHUNK, LANES), 1)
    row_i = lax.broadcasted_iota(jnp.int32, (CHUNK, LANES), 0)
    lane_row = lax.broadcasted_iota(jnp.int32, (1, LANES), 1)
    in_a = (lane_row < RET_HEAD_DIM).astype(BF16)
    in_b = (lane_row >= RET_HEAD_DIM).astype(BF16)
    lo_half = (lane % RET_HEAD_DIM) < (RET_HEAD_DIM // 2)
    key_le_query = row_i <= lane
    lane_p = lax.broadcasted_iota(jnp.int32, (CHUNK, POOL_WIDTH), 1)
    row_p = lax.broadcasted_iota(jnp.int32, (CHUNK, POOL_WIDTH), 0)

    def pool_stages(r0, rows):
        ext = hbuf[r0:r0 + HIST_ROWS + CHUNK, 0:POOL_WIDTH]
        s2 = ext + pltpu.roll(ext, 1, 0)
        s4 = s2 + pltpu.roll(s2, 2, 0)
        s8 = s4 + pltpu.roll(s4, 4, 0)
        s16 = s8 + pltpu.roll(s8, 8, 0)
        u, s2, s4, s8, s16 = (a[HIST_ROWS:] for a in (ext, s2, s4, s8, s16))
        yield
        g0, g1, g2 = (lane_p < POOL_GROUP_DIM, lane_p < 2 * POOL_GROUP_DIM, lane_p < 3 * POOL_GROUP_DIM)
        wsum = jnp.where(g0, s2, jnp.where(g1, s4, jnp.where(g2, s8, s16)))
        width = jnp.where(g0, 2, jnp.where(g1, 4, jnp.where(g2, 8, 16)))
        tpos = row_p + (j * seq_tile + r0 + 1)
        count = jnp.minimum(tpos, width).astype(F32)
        pooled = wsum / count - u
        y_pool = _dot(pooled.astype(BF16), wpool_ref[...]) * vec(R_POOL_SCALE, POOL_WIDTH)
        mix_ref[rows, RET_WIDTH:RET_WIDTH + POOL_WIDTH] = y_pool.astype(BF16)

    def retention_stages(rows):
        pairs = range(RET_PAIRS)
        col = lambda base, p: z_ref[rows, base + p * LANES:base + (p + 1) * LANES]
        cosv = cos_ref[rows, :]
        sinv = sin_ref[rows, :]
        rope = lambda a: a * cosv + _swap_halves(a, lo_half) * sinv
        q = [rope(col(RQ, p)) for p in pairs]
        k = [rope(col(RK, p)) for p in pairs]
        yield
        kb = [k[p].astype(BF16) for p in pairs]
        vb = [col(RV, p).astype(BF16) for p in pairs]
        k2 = [jnp.concatenate([kb[p] * in_a, kb[p] * in_b], axis=0) for p in pairs]
        v2 = [jnp.concatenate([vb[p] * in_a, vb[p] * in_b], axis=0) for p in pairs]
        scores = [_dot_nt(q[p].astype(BF16), k2[p]) for p in pairs]
        kz_t = [(k[p] * zeta_ref[p]).T.astype(BF16) for p in pairs]
        yield
        lhs = [jnp.concatenate([scores[p].astype(BF16) * dtab_ref[p], (q[p] * xi_ref[p]).astype(BF16)], axis=1)
               for p in pairs]
        rhs = [jnp.concatenate([v2[p], rstate[p].astype(BF16)], axis=0) for p in pairs]
        ys = [_dot(lhs[p], rhs[p]) for p in pairs]
        kv = [_dot(kz_t[p], vb[p]) for p in pairs]
        for p in pairs:
            rstate[p] = rstate[p] * cd_ref[p] + kv[p] * bd_ref[...]
        yield
        y = jnp.concatenate(ys, axis=0)
        yc = y - _dot(y.astype(BF16), avg_ref[...])
        yield
        var = _dot((yc * yc).astype(BF16), avg_ref[...])
        yn = yc * lax.rsqrt(var + EPS)
        for p in pairs:
            cs = slice(p * LANES, (p + 1) * LANES)
            gain = rows_ref[R_RET_GN:R_RET_GN + 1, cs]
            mix_ref[rows, cs] = (_silu(col(RG, p)) * (yn[p * CHUNK:(p + 1) * CHUNK] * gain)).astype(BF16)

    def mlstm_stages(r0, rows):
        gates_t = (z_ref[rows, MQ:MQ + LANES] + vec(R_GATE_BIAS, LANES)).T
        li = gates_t[GATE_LANE:GATE_LANE + 8]
        fpre = pltpu.roll(li, MLSTM_HEADS, 0)
        lf = jnp.minimum(fpre, 0.0) - jnp.log(1.0 + jnp.exp(-jnp.abs(fpre)))
        bcum = _lane_scan(lf, jnp.add, 0.0)
        g = li - bcum
        cmax = _lane_scan(g, jnp.maximum, NEG)
        m_prev = mstate[...]
        big_g = jnp.maximum(m_prev, cmax)
        inter = jnp.exp(m_prev - big_g)
        emr = jnp.exp(-(bcum + big_g))
        g_last = jnp.broadcast_to(big_g[:, LANES - 1:LANES], (8, LANES))
        b_last = jnp.broadcast_to(bcum[:, LANES - 1:LANES], (8, LANES))
        wk = jnp.exp(g - g_last)
        s_old = jnp.exp(m_prev - g_last)
        mstate[...] = b_last + g_last
        g_cols = jnp.concatenate([g, jnp.zeros((CHUNK - 8, LANES), F32)], axis=0).T
        yield
        ext = hbuf[r0 + HIST_ROWS - 8:r0 + HIST_ROWS + CHUNK, POOL_WIDTH:HIST_COLS]
        tap = lambda kk: vec(R_CONV_W + kk, 2 * MLSTM_WIDTH_P)
        prev = pltpu.roll(ext, 1, 0)
        older = pltpu.roll(tap(1) * ext + tap(0) * prev, 2, 0)
        yield
        conv = vec(R_CONV_B, 2 * MLSTM_WIDTH_P) + tap(3) * ext[8:] + tap(2) * prev[8:] + older[8:]
        qk = _silu(conv)
        yield
        v_all_t = z_ref[rows, MV:MV + MLSTM_WIDTH].T
        ones_row = jnp.where(lax.broadcasted_iota(jnp.int32, (8, LANES), 0) == 0, 1.0, 0.0)
        v_tail = jnp.concatenate([ones_row, jnp.zeros((HEAD_PAD - MLSTM_HEAD_DIM - 8, LANES), F32)], axis=0)
        heads = range(MLSTM_HEADS)
        row = lambda a, h: a[h:h + 1, :]
        qh = [(qk[:, h * HEAD_PAD:(h + 1) * HEAD_PAD] * (MLSTM_HEAD_DIM ** -0.5)).astype(BF16) for h in heads]
        kh = [qk[:, MLSTM_WIDTH_P + h * HEAD_PAD:MLSTM_WIDTH_P + (h + 1) * HEAD_PAD].astype(BF16) for h in heads]
        v_t = [jnp.concatenate([v_all_t[h * MLSTM_HEAD_DIM:(h + 1) * MLSTM_HEAD_DIM], v_tail], axis=0) for h in heads]
        c_t = [cstate[h] for h in heads]
        scores_t = [_dot_nt(kh[h], qh[h]) for h in heads]
        cross_t = [_dot_nt(c_t[h].astype(BF16), qh[h]) for h in heads]
        kv_t = [_dot((v_t[h] * row(wk, h)).astype(BF16), kh[h]) for h in heads]
        yield
        for h in heads:
            cstate[h] = c_t[h] * row(s_old, h) + kv_t[h]
        decay_t = [jnp.exp(jnp.where(key_le_query, g_cols[:, h:h + 1] - row(big_g, h), NEG)) for h in heads]
        sc_t = [(scores_t[h] * decay_t[h]).astype(BF16) for h in heads]
        yield
        nd_t = [_dot(v_t[h].astype(BF16), sc_t[h]) + cross_t[h] * row(inter, h) for h in heads]
        yield
        hn_ts = []
        for h in heads:
            den = nd_t[h][MLSTM_HEAD_DIM:MLSTM_HEAD_DIM + 1, :]
            h_t = nd_t[h][0:MLSTM_HEAD_DIM, :] * (1.0 / jnp.maximum(jnp.abs(den), row(emr, h)))
            mu = jnp.sum(h_t, axis=0, keepdims=True) / MLSTM_HEAD_DIM
            hc = h_t - mu
            var = jnp.sum(hc * hc, axis=0, keepdims=True) / MLSTM_HEAD_DIM
            hn_ts.append(hc * lax.rsqrt(var + EPS))
        yield
        hn = jnp.concatenate(hn_ts, axis=0).T
        o_gate = _sigmoid(z_ref[rows, MO:MO + MLSTM_WIDTH])
        mix_ref[rows, RET_WIDTH + POOL_WIDTH:MIX_WIDTH] = (
            o_gate * (hn * vec(R_MLSTM_GN, MLSTM_WIDTH))).astype(BF16)

    for c in range(seq_tile // CHUNK):
        r0 = c * CHUNK
        rows = pl.ds(r0, CHUNK)
        streams = [mlstm_stages(r0, rows), retention_stages(rows), pool_stages(r0, rows)]
        while streams:
            for s in list(streams):
                if next(s, StopIteration) is StopIteration:
                    streams.remove(s)
                other_work.tick()
    other_work.finish()

    hbuf[0:HIST_ROWS, :] = hbuf[seq_tile:seq_tile + HIST_ROWS, :]


def _retention_tables():
    lg = np.log1p(-(2.0 ** (-5.0 - np.arange(RET_HEADS, dtype=np.float64))))
    idx = np.arange(CHUNK, dtype=np.float64)
    rel = idx[:, None] - idx[None, :]
    head_of_lane = np.arange(LANES) // RET_HEAD_DIM
    dtab, xi, zeta, cd = [], [], [], []
    same = (head_of_lane[:, None] == head_of_lane[None, :])
    key_scale = RET_HEAD_DIM ** -0.5
    for p in range(RET_PAIRS):
        hl = lg[2 * p + head_of_lane]
        dtab.append(key_scale * np.concatenate(
            [np.where(rel >= 0, np.exp(lg[2 * p + a] * np.maximum(rel, 0.0)), 0.0) for a in range(2)], axis=1))
        xi.append(np.exp(hl[None, :] * (idx[:, None] + 1.0)))
        zeta.append(key_scale * np.exp(hl[None, :] * (CHUNK - 1 - idx[:, None])))
        cd.append(np.where(same, np.exp(hl * CHUNK)[:, None], 0.0))
    f = lambda a: jnp.asarray(np.stack(a).astype(np.float32))
    avg = jnp.asarray((same / RET_HEAD_DIM).astype(np.float32)).astype(BF16)
    return f(dtab).astype(BF16), f(xi), f(zeta), f(cd), jnp.asarray(same.astype(np.float32)), avg


def _mixers(x, cos_t, sin_t, tables, rows, wpool, w_in, layer, seq):
    t = x.shape[0]
    n_tiles = t // SEQ_TILE
    ahead = lambda k: pl.BlockSpec((SEQ_TILE, D_MODEL), lambda j: (jnp.minimum(j + k, n_tiles - 1), 0))
    tile = lambda w: pl.BlockSpec((SEQ_TILE, w), lambda j: (jnp.maximum(j - 1, 0), 0))
    full = lambda a: pl.BlockSpec(a.shape, lambda j: (0,) * a.ndim)
    resident = lambda a: pl.BlockSpec((None,) + a.shape[1:], lambda j: (layer,) + (0,) * (a.ndim - 1),
                                      pipeline_mode=pl.Buffered(1))
    consts = tuple(tables) + (rows, wpool, w_in)
    return pl.pallas_call(
        functools.partial(_mixer_kernel, seq_tile=SEQ_TILE, tiles_per_seq=seq // SEQ_TILE),
        grid=(n_tiles + 1,),
        in_specs=[ahead(0), ahead(1), tile(LANES), tile(LANES)] + [full(a) for a in tables]
        + [_layer_block(rows, layer), _layer_block(wpool, layer), resident(w_in)],
        out_specs=tile(MIX_WIDTH),
        out_shape=jax.ShapeDtypeStruct((t, MIX_WIDTH), BF16),
        scratch_shapes=[
            pltpu.VMEM((D_MODEL, Z_WIDTH), BF16),
            pltpu.VMEM((SEQ_TILE, D_MODEL), BF16),
            pltpu.VMEM((SEQ_TILE, Z_WIDTH), F32),
            pltpu.VMEM((SEQ_TILE, Z_WIDTH), F32),
            pltpu.VMEM((SEQ_TILE + HIST_ROWS, HIST_COLS), F32),
            pltpu.VMEM((RET_PAIRS, LANES, LANES), F32),
            pltpu.VMEM((MLSTM_HEADS, LANES, LANES), F32),
            pltpu.VMEM((8, LANES), F32),
        ],
        compiler_params=pltpu.CompilerParams(
            dimension_semantics=("arbitrary",), vmem_limit_bytes=VMEM_LIMIT),
        name="token_mixers",
    )(x, x, cos_t, sin_t, *consts)


FF_CHUNK = 512


def _channel_kernel(x_ref, mix_ref, p_ref, rows_ref, wout_ref, wgu_ref, wdown_ref,
                    wpg_ref, wpp_ref, o_ref, *, final):
    vec = lambda r: rows_ref[r:r + 1, :]
    x1 = x_ref[...] + _dot(mix_ref[...], wout_ref[...])
    h = _rmsnorm(x1, vec(R_NORM_FFN)).astype(BF16)
    acc = x1
    for c0 in range(0, D_FF, FF_CHUNK):
        c1 = min(c0 + FF_CHUNK, D_FF)
        gate = _dot(h, wgu_ref[:, c0:c1])
        up = _dot(h, wgu_ref[:, D_FF + c0:D_FF + c1])
        act = (_silu(gate) * up).astype(BF16)
        acc = acc + _dot(act, wdown_ref[c0:c1, :])
    hp = _rmsnorm(acc, vec(R_NORM_PLE)).astype(BF16)
    emb = _dot(p_ref[...].astype(BF16), wpp_ref[...])
    x3 = acc + _sigmoid(_dot(hp, wpg_ref[...])) * emb
    if final:
        x3 = _rmsnorm(x3, vec(R_NORM_FINAL))
    o_ref[...] = x3


def _channel(x, mix, p, rows, wout, wgu, wdown, wpg, wpp, layer, final):
    t = x.shape[0]
    tile = lambda w: pl.BlockSpec((ROW_TILE, w), lambda i: (i, 0))
    resident = lambda a: pl.BlockSpec((None,) + a.shape[1:], lambda i: (layer,) + (0,) * (a.ndim - 1),
                                      pipeline_mode=pl.Buffered(1))
    consts = (rows, wout, wgu, wdown, wpg, wpp)
    return pl.pallas_call(
        functools.partial(_channel_kernel, final=final),
        grid=(t // ROW_TILE,),
        in_specs=[tile(D_MODEL), tile(MIX_WIDTH),
                  pl.BlockSpec((None, ROW_TILE, PLE_DIM), lambda i: (layer, i, 0))]
        + [resident(a) for a in consts],
        out_specs=tile(D_MODEL),
        out_shape=jax.ShapeDtypeStruct((t, D_MODEL), F32),
        compiler_params=pltpu.CompilerParams(
            dimension_semantics=("arbitrary",), vmem_limit_bytes=VMEM_LIMIT),
        name="channel_mixing",
    )(x, mix, p, *consts)


def _head_segments(v):
    zeros = jnp.zeros(v.shape[:-1] + (HEAD_PAD - MLSTM_HEAD_DIM,), v.dtype)
    out = []
    for h in range(v.shape[-1] // MLSTM_HEAD_DIM):
        out += [v[..., h * MLSTM_HEAD_DIM:(h + 1) * MLSTM_HEAD_DIM], zeros]
    return out


def _block_diag(w):
    depth, g, d, _ = w.shape
    eye = jnp.asarray(np.eye(g, dtype=np.float32))
    return (w[:, :, :, None, :] * eye[None, :, None, :, None]).reshape(depth, g * d, g * d)


def _pack_rows(depth, rows):
    segs = []
    for row in rows:
        width = sum(s.shape[-1] for s in row)
        segs += list(row) + [jnp.zeros((depth, D_MODEL - width), F32)]
    segs.append(jnp.zeros((depth, (N_ROWS - len(rows)) * D_MODEL), F32))
    return jnp.concatenate(segs, axis=-1).reshape(depth, N_ROWS, D_MODEL)


def kernel(x, p, positions, norm_mix, w_in, ret_gn, pool_w, pool_scale, conv_w, conv_b, b_igate, b_fgate,
           mlstm_gn, w_out, norm_ffn, w_gate_up, w_down, norm_ple, w_ple_gate, w_ple_proj, norm_final):
    batch, seq, d = x.shape
    depth = w_in.shape[0]
    t = batch * seq
    xf = x.reshape(t, d)
    cos_t, sin_t = _trig_tables(positions)
    tables = _retention_tables()

    rows = _pack_rows(depth, [
        [norm_mix], [ret_gn], [pool_scale], _head_segments(conv_b), [mlstm_gn],
        [jnp.zeros((depth, GATE_LANE), F32), b_igate, b_fgate], [norm_ffn], [norm_ple],
        [jnp.broadcast_to(norm_final, (depth, d))],
    ] + [_head_segments(conv_w[:, k]) for k in range(MLSTM_CONV)])
    wpool = _block_diag(pool_w).astype(BF16)
    w_in_t = jnp.swapaxes(w_in, 1, 2)
    wout = w_out.astype(BF16)
    wgu, wdown = w_gate_up.astype(BF16), w_down.astype(BF16)
    wpg, wpp = w_ple_gate.astype(BF16), w_ple_proj.astype(BF16)
    pf = p.reshape(depth, t, PLE_DIM)

    for i in range(depth):
        mix = _mixers(xf, cos_t, sin_t, tables, rows, wpool, w_in_t, i, seq)
        xf = _channel(xf, mix, pf, rows, wout, wgu, wdown, wpg, wpp, i, final=(i == depth - 1))
    return xf.reshape(batch, seq, d)
```

```python
import functools

import numpy as np
import jax
import jax.numpy as jnp
from jax import lax
from jax.experimental import pallas as pl
from jax.experimental.pallas import tpu as pltpu

F32 = jnp.float32
BF16 = jnp.bfloat16

D_MODEL = 1024
PLE_DIM = 256
RET_HEADS = 6
RET_HEAD_DIM = 64
RET_WIDTH = RET_HEADS * RET_HEAD_DIM
RET_PAIRS = RET_HEADS // 2
POOL_WINDOWS = (2, 4, 8, 16)
POOL_GROUP_DIM = 64
POOL_WIDTH = len(POOL_WINDOWS) * POOL_GROUP_DIM
MLSTM_HEADS = 4
MLSTM_HEAD_DIM = 96
MLSTM_WIDTH = MLSTM_HEADS * MLSTM_HEAD_DIM
MLSTM_CONV = 4
CHUNK = 128
D_FF = 2816
ROPE_BASE = 10000.0
EPS = 1e-6

LANES = 128
HEAD_PAD = LANES
MLSTM_WIDTH_P = MLSTM_HEADS * HEAD_PAD

RQ = 0
RK = RQ + RET_WIDTH
RV = RK + RET_WIDTH
RG = RV + RET_WIDTH
PU = RG + RET_WIDTH
MQ = PU + POOL_WIDTH
MK = MQ + MLSTM_WIDTH_P
MV = MK + MLSTM_WIDTH_P
MO = MV + MLSTM_WIDTH
GT = MO + MLSTM_WIDTH
Z_WIDTH = GT + LANES
HIST_COLS = MV - PU
HIST_ROWS = 16
MIX_WIDTH = RET_WIDTH + POOL_WIDTH + MLSTM_WIDTH
NEG = -1e30

ROW_TILE = 1024
SEQ_TILE = 512
VMEM_LIMIT = 56 * 1024 * 1024

(R_NORM_MIX, R_RET_GN, R_POOL_SCALE, R_CONV_B, R_MLSTM_GN, R_GATE_BIAS, R_NORM_FFN, R_NORM_PLE,
 R_NORM_FINAL, R_CONV_W) = range(10)
N_ROWS = 16


def _rmsnorm(x, g):
    return x * lax.rsqrt(jnp.mean(x * x, axis=-1, keepdims=True) + EPS) * g


def _sigmoid(x):
    return 1.0 / (1.0 + jnp.exp(-x))


def _silu(x):
    half = 0.5 * x
    return half + half * jnp.tanh(half)


def _dot(a, b):
    return jnp.dot(a, b, preferred_element_type=F32)


def _dot_nt(a, b):
    return lax.dot_general(a, b, (((1,), (1,)), ((), ())), preferred_element_type=F32)


TOKENS_PER_ROW = LANES // (RET_HEAD_DIM // 2)


def _trig_kernel(pos_ref, inv_ref, spread_ref, sign_ref, cos_ref, sin_ref):
    pos = pos_ref[...]
    ang = pos[:, 0:1] * inv_ref[0:1, :]
    for r in range(1, TOKENS_PER_ROW):
        ang = ang + pos[:, r:r + 1] * inv_ref[r:r + 1, :]
    n = ang.shape[0]
    for val, out_ref, sign in ((jnp.cos(ang), cos_ref, None), (jnp.sin(ang), sin_ref, sign_ref[...])):
        hi = val.astype(BF16)
        parts = jnp.concatenate([hi, (val - hi.astype(F32)).astype(BF16)], axis=1)
        for r in range(TOKENS_PER_ROW):
            wide = _dot(parts, spread_ref[r])
            out_ref[pl.ds(r, n, stride=TOKENS_PER_ROW), :] = wide if sign is None else wide * sign


def _trig_tables(positions):
    t = positions.size
    half = RET_HEAD_DIM // 2
    inv = ROPE_BASE ** (-jnp.arange(half, dtype=F32) / half)
    eye = jnp.asarray(np.eye(TOKENS_PER_ROW, dtype=np.float32))
    inv_rows = (eye[:, :, None] * inv[None, None, :]).reshape(TOKENS_PER_ROW, LANES)
    lanes = np.arange(LANES)
    spread = np.stack([(np.arange(LANES)[:, None] == half * r + lanes[None, :] % half) for r in range(TOKENS_PER_ROW)])
    spread = jnp.asarray(np.concatenate([spread, spread], axis=1).astype(np.float32)).astype(BF16)
    sign = np.where((lanes % RET_HEAD_DIM) < half, -1.0, 1.0).astype(np.float32).reshape(1, LANES)
    tile = min(2 * ROW_TILE, t)
    rows = tile // TOKENS_PER_ROW
    full = lambda a: pl.BlockSpec(a.shape, lambda i: (0,) * a.ndim)
    out = pl.BlockSpec((tile, LANES), lambda i: (i, 0))
    pos = positions.astype(F32).reshape(t // TOKENS_PER_ROW, TOKENS_PER_ROW)
    sign = jnp.asarray(sign)
    return pl.pallas_call(
        _trig_kernel,
        grid=(t // tile,),
        in_specs=[pl.BlockSpec((rows, TOKENS_PER_ROW), lambda i: (i, 0)), full(inv_rows), full(spread), full(sign)],
        out_specs=[out, out],
        out_shape=[jax.ShapeDtypeStruct((t, LANES), F32)] * 2,
        compiler_params=pltpu.CompilerParams(dimension_semantics=("arbitrary",)),
        name="rope_tables",
    )(pos, inv_rows, spread, sign)


def _layer_block(a, layer):
    return pl.BlockSpec((None,) + a.shape[1:], lambda *_: (layer,) + (0,) * (a.ndim - 1))


def _swap_halves(x, lo_half):
    return jnp.where(lo_half, pltpu.roll(x, LANES - RET_HEAD_DIM // 2, 1), pltpu.roll(x, RET_HEAD_DIM // 2, 1))


def _lane_scan(x, op, fill):
    lane = lax.broadcasted_iota(jnp.int32, x.shape, 1)
    sh = 1
    while sh < LANES:
        x = op(x, jnp.where(lane >= sh, pltpu.roll(x, sh, 1), fill))
        sh *= 2
    return x


def _mixer_kernel(x_ref, xnext_ref, cos_ref, sin_ref, dtab_ref, xi_ref, zeta_ref, cd_ref, bd_ref, avg_ref,
                  rows_ref, wpool_ref, wraw_ref,
                  mix_ref, win_ref, h_ref, z_even, z_odd, hbuf, rstate, cstate, mstate, *, seq_tile, tiles_per_seq):
    j = pl.program_id(0)
    tile_in_seq = jnp.maximum(j - 1, 0) % tiles_per_seq
    norm_in = lambda ref: _rmsnorm(ref[...], rows_ref[R_NORM_MIX:R_NORM_MIX + 1, :]).astype(BF16)

    @pl.when(j == 0)
    def _():
        h_ref[...] = norm_in(x_ref)
        def put(col, src, n):
            block = wraw_ref[src:src + n, :]
            if n < LANES:
                block = jnp.concatenate([block, jnp.zeros((LANES - n, D_MODEL), F32)], axis=0)
            win_ref[:, col:col + LANES] = block.T.astype(BF16)

        o = 4 * RET_WIDTH + POOL_WIDTH
        for c0 in range(0, o, LANES):
            put(c0, c0, LANES)
        for k in range(2 * MLSTM_HEADS):
            put(o + k * HEAD_PAD, o + k * MLSTM_HEAD_DIM, MLSTM_HEAD_DIM)
        src = o + 2 * MLSTM_WIDTH
        for c0 in range(0, 2 * MLSTM_WIDTH, LANES):
            put(MV + c0, src + c0, LANES)
        put(GT, src + 2 * MLSTM_WIDTH, 2 * MLSTM_HEADS)
        z_odd[...] = jnp.zeros_like(z_odd)

    @pl.when(tile_in_seq == 0)
    def _():
        hbuf[0:HIST_ROWS, :] = jnp.zeros((HIST_ROWS, HIST_COLS), F32)
        rstate[...] = jnp.zeros_like(rstate)
        cstate[...] = jnp.zeros_like(cstate)
        mstate[...] = jnp.zeros_like(mstate)

    def step(z_next, z_ref):
        def project(c0, c1):
            z_next[:, c0:c1] = _dot(h_ref[...], win_ref[:, c0:c1])

        slabs = [functools.partial(project, c0, min(c0 + PROJ_SLAB, Z_WIDTH)) for c0 in range(0, Z_WIDTH, PROJ_SLAB)]
        _mixer_tile(z_ref, cos_ref, sin_ref, dtab_ref, xi_ref, zeta_ref, cd_ref, bd_ref, avg_ref, rows_ref,
                    wpool_ref, mix_ref, hbuf, rstate, cstate, mstate, tile_in_seq, seq_tile,
                    _Interleave(slabs, (seq_tile // CHUNK) * STAGES_PER_CHUNK))
        h_ref[...] = norm_in(xnext_ref)

    @pl.when(j % 2 == 0)
    def _():
        step(z_even, z_odd)

    @pl.when(j % 2 == 1)
    def _():
        step(z_odd, z_even)


PROJ_SLAB = 512
STAGES_PER_CHUNK = 15


class _Interleave:
    def __init__(self, thunks, n_points):
        self.thunks, self.n_points, self.point, self.done = thunks, n_points, 0, 0

    def tick(self):
        self.point += 1
        due = len(self.thunks) if self.point >= self.n_points else (self.point * len(self.thunks)) // self.n_points
        while self.done < due:
            self.thunks[self.done]()
            self.done += 1

    def finish(self):
        self.point = self.n_points - 1
        self.tick()


def _mixer_tile(z_ref, cos_ref, sin_ref, dtab_ref, xi_ref, zeta_ref, cd_ref, bd_ref, avg_ref, rows_ref,
                wpool_ref, mix_ref, hbuf, rstate, cstate, mstate, j, seq_tile, other_work):
    vec = lambda r, n: rows_ref[r:r + 1, 0:n]
    hbuf[HIST_ROWS:, :] = z_ref[:, PU:MV]

    lane = lax.broadcasted_iota(jnp.int32, (CHUNK, LANES), 1)
    row_i = lax.broadcasted_iota(jnp.int32, (CHUNK, LANES), 0)
    lane_row = lax.broadcasted_iota(jnp.int32, (1, LANES), 1)
    in_a = (lane_row < RET_HEAD_DIM).astype(BF16)
    in_b = (lane_row >= RET_HEAD_DIM).astype(BF16)
    lo_half = (lane % RET_HEAD_DIM) < (RET_HEAD_DIM // 2)
    key_le_query = row_i <= lane
    lane_p = lax.broadcasted_iota(jnp.int32, (CHUNK, POOL_WIDTH), 1)
    row_p = lax.broadcasted_iota(jnp.int32, (CHUNK, POOL_WIDTH), 0)

    def pool_stages(r0, rows):
        ext = hbuf[r0:r0 + HIST_ROWS + CHUNK, 0:POOL_WIDTH]
        s2 = ext + pltpu.roll(ext, 1, 0)
        s4 = s2 + pltpu.roll(s2, 2, 0)
        s8 = s4 + pltpu.roll(s4, 4, 0)
        s16 = s8 + pltpu.roll(s8, 8, 0)
        u, s2, s4, s8, s16 = (a[HIST_ROWS:] for a in (ext, s2, s4, s8, s16))
        yield
        g0, g1, g2 = (lane_p < POOL_GROUP_DIM, lane_p < 2 * POOL_GROUP_DIM, lane_p < 3 * POOL_GROUP_DIM)
        wsum = jnp.where(g0, s2, jnp.where(g1, s4, jnp.where(g2, s8, s16)))
        width = jnp.where(g0, 2, jnp.where(g1, 4, jnp.where(g2, 8, 16)))
        tpos = row_p + (j * seq_tile + r0 + 1)
        count = jnp.minimum(tpos, width).astype(F32)
        pooled = wsum / count - u
        y_pool = _dot(pooled.astype(BF16), wpool_ref[...]) * vec(R_POOL_SCALE, POOL_WIDTH)
        mix_ref[rows, RET_WIDTH:RET_WIDTH + POOL_WIDTH] = y_pool.astype(BF16)

    def retention_stages(rows):
        pairs = range(RET_PAIRS)
        col = lambda base, p: z_ref[rows, base + p * LANES:base + (p + 1) * LANES]
        cosv = cos_ref[rows, :]
        sinv = sin_ref[rows, :]
        rope = lambda a: a * cosv + _swap_halves(a, lo_half) * sinv
        q = [rope(col(RQ, p)) for p in pairs]
        k = [rope(col(RK, p)) for p in pairs]
        yield
        kb = [k[p].astype(BF16) for p in pairs]
        vb = [col(RV, p).astype(BF16) for p in pairs]
        k2 = [jnp.concatenate([kb[p] * in_a, kb[p] * in_b], axis=0) for p in pairs]
        v2 = [jnp.concatenate([vb[p] * in_a, vb[p] * in_b], axis=0) for p in pairs]
        scores = [_dot_nt(q[p].astype(BF16), k2[p]) for p in pairs]
        kz_t = [(k[p] * zeta_ref[p]).T.astype(BF16) for p in pairs]
        yield
        lhs = [jnp.concatenate([scores[p].astype(BF16) * dtab_ref[p], (q[p] * xi_ref[p]).astype(BF16)], axis=1)
               for p in pairs]
        rhs = [jnp.concatenate([v2[p], rstate[p].astype(BF16)], axis=0) for p in pairs]
        ys = [_dot(lhs[p], rhs[p]) for p in pairs]
        kv = [_dot(kz_t[p], vb[p]) for p in pairs]
        for p in pairs:
            rstate[p] = rstate[p] * cd_ref[p] + kv[p] * bd_ref[...]
        yield
        y = jnp.concatenate(ys, axis=0)
        yc = y - _dot(y.astype(BF16), avg_ref[...])
        yield
        var = _dot((yc * yc).astype(BF16), avg_ref[...])
        yn = yc * lax.rsqrt(var + EPS)
        for p in pairs:
            cs = slice(p * LANES, (p + 1) * LANES)
            gain = rows_ref[R_RET_GN:R_RET_GN + 1, cs]
            mix_ref[rows, cs] = (_silu(col(RG, p)) * (yn[p * CHUNK:(p + 1) * CHUNK] * gain)).astype(BF16)

    def mlstm_stages(r0, rows):
        gates_t = (z_ref[rows, GT:GT + LANES] + vec(R_GATE_BIAS, LANES)).T
        li = gates_t[0:8]
        fpre = pltpu.roll(li, MLSTM_HEADS, 0)
        lf = jnp.minimum(fpre, 0.0) - jnp.log(1.0 + jnp.exp(-jnp.abs(fpre)))
        bcum = _lane_scan(lf, jnp.add, 0.0)
        g = li - bcum
        cmax = _lane_scan(g, jnp.maximum, NEG)
        m_prev = mstate[...]
        big_g = jnp.maximum(m_prev, cmax)
        inter = jnp.exp(m_prev - big_g)
        emr = jnp.exp(-(bcum + big_g))
        g_last = jnp.broadcast_to(big_g[:, LANES - 1:LANES], (8, LANES))
        b_last = jnp.broadcast_to(bcum[:, LANES - 1:LANES], (8, LANES))
        wk = jnp.exp(g - g_last)
        s_old = jnp.exp(m_prev - g_last)
        mstate[...] = b_last + g_last
        g_cols = jnp.concatenate([g, jnp.zeros((CHUNK - 8, LANES), F32)], axis=0).T
        yield
        ext = hbuf[r0 + HIST_ROWS - 8:r0 + HIST_ROWS + CHUNK, POOL_WIDTH:HIST_COLS]
        tap = lambda kk: vec(R_CONV_W + kk, 2 * MLSTM_WIDTH_P)
        prev = pltpu.roll(ext, 1, 0)
        older = pltpu.roll(tap(1) * ext + tap(0) * prev, 2, 0)
        yield
        conv = vec(R_CONV_B, 2 * MLSTM_WIDTH_P) + tap(3) * ext[8:] + tap(2) * prev[8:] + older[8:]
        qk = _silu(conv)
        yield
        v_all_t = z_ref[rows, MV:MV + MLSTM_WIDTH].T
        ones_row = jnp.where(lax.broadcasted_iota(jnp.int32, (8, LANES), 0) == 0, 1.0, 0.0)
        v_tail = jnp.concatenate([ones_row, jnp.zeros((HEAD_PAD - MLSTM_HEAD_DIM - 8, LANES), F32)], axis=0)
        heads = range(MLSTM_HEADS)
        row = lambda a, h: a[h:h + 1, :]
        qh = [(qk[:, h * HEAD_PAD:(h + 1) * HEAD_PAD] * (MLSTM_HEAD_DIM ** -0.5)).astype(BF16) for h in heads]
        kh = [qk[:, MLSTM_WIDTH_P + h * HEAD_PAD:MLSTM_WIDTH_P + (h + 1) * HEAD_PAD].astype(BF16) for h in heads]
        v_t = [jnp.concatenate([v_all_t[h * MLSTM_HEAD_DIM:(h + 1) * MLSTM_HEAD_DIM], v_tail], axis=0) for h in heads]
        c_t = [cstate[h] for h in heads]
        scores_t = [_dot_nt(kh[h], qh[h]) for h in heads]
        cross_t = [_dot_nt(c_t[h].astype(BF16), qh[h]) for h in heads]
        kv_t = [_dot((v_t[h] * row(wk, h)).astype(BF16), kh[h]) for h in heads]
        yield
        for h in heads:
            cstate[h] = c_t[h] * row(s_old, h) + kv_t[h]
        decay_t = [jnp.exp(jnp.where(key_le_query, g_cols[:, h:h + 1] - row(big_g, h), NEG)) for h in heads]
        sc_t = [(scores_t[h] * decay_t[h]).astype(BF16) for h in heads]
        yield
        nd_t = [_dot(v_t[h].astype(BF16), sc_t[h]) + cross_t[h] * row(inter, h) for h in heads]
        yield
        hn_ts = []
        for h in heads:
            den = nd_t[h][MLSTM_HEAD_DIM:MLSTM_HEAD_DIM + 1, :]
            h_t = nd_t[h][0:MLSTM_HEAD_DIM, :] * (1.0 / jnp.maximum(jnp.abs(den), row(emr, h)))
            mu = jnp.sum(h_t, axis=0, keepdims=True) / MLSTM_HEAD_DIM
            hc = h_t - mu
            var = jnp.sum(hc * hc, axis=0, keepdims=True) / MLSTM_HEAD_DIM
            hn_ts.append(hc * lax.rsqrt(var + EPS))
        yield
        hn = jnp.concatenate(hn_ts, axis=0).T
        o_gate = _sigmoid(z_ref[rows, MO:MO + MLSTM_WIDTH])
        mix_ref[rows, RET_WIDTH + POOL_WIDTH:MIX_WIDTH] = (
            o_gate * (hn * vec(R_MLSTM_GN, MLSTM_WIDTH))).astype(BF16)

    for c in range(seq_tile // CHUNK):
        r0 = c * CHUNK
        rows = pl.ds(r0, CHUNK)
        streams = [mlstm_stages(r0, rows), retention_stages(rows), pool_stages(r0, rows)]
        while streams:
            for s in list(streams):
                if next(s, StopIteration) is StopIteration:
                    streams.remove(s)
                other_work.tick()
    other_work.finish()

    hbuf[0:HIST_ROWS, :] = hbuf[seq_tile:seq_tile + HIST_ROWS, :]


def _retention_tables():
    lg = np.log1p(-(2.0 ** (-5.0 - np.arange(RET_HEADS, dtype=np.float64))))
    idx = np.arange(CHUNK, dtype=np.float64)
    rel = idx[:, None] - idx[None, :]
    head_of_lane = np.arange(LANES) // RET_HEAD_DIM
    dtab, xi, zeta, cd = [], [], [], []
    same = (head_of_lane[:, None] == head_of_lane[None, :])
    key_scale = RET_HEAD_DIM ** -0.5
    for p in range(RET_PAIRS):
        hl = lg[2 * p + head_of_lane]
        dtab.append(key_scale * np.concatenate(
            [np.where(rel >= 0, np.exp(lg[2 * p + a] * np.maximum(rel, 0.0)), 0.0) for a in range(2)], axis=1))
        xi.append(np.exp(hl[None, :] * (idx[:, None] + 1.0)))
        zeta.append(key_scale * np.exp(hl[None, :] * (CHUNK - 1 - idx[:, None])))
        cd.append(np.where(same, np.exp(hl * CHUNK)[:, None], 0.0))
    f = lambda a: jnp.asarray(np.stack(a).astype(np.float32))
    avg = jnp.asarray((same / RET_HEAD_DIM).astype(np.float32)).astype(BF16)
    return f(dtab).astype(BF16), f(xi), f(zeta), f(cd), jnp.asarray(same.astype(np.float32)), avg


def _mixers(x, cos_t, sin_t, tables, rows, wpool, w_in, layer, seq):
    t = x.shape[0]
    n_tiles = t // SEQ_TILE
    ahead = lambda k: pl.BlockSpec((SEQ_TILE, D_MODEL), lambda j: (jnp.minimum(j + k, n_tiles - 1), 0))
    tile = lambda w: pl.BlockSpec((SEQ_TILE, w), lambda j: (jnp.maximum(j - 1, 0), 0))
    full = lambda a: pl.BlockSpec(a.shape, lambda j: (0,) * a.ndim)
    resident = lambda a: pl.BlockSpec((None,) + a.shape[1:], lambda j: (layer,) + (0,) * (a.ndim - 1),
                                      pipeline_mode=pl.Buffered(1))
    consts = tuple(tables) + (rows, wpool, w_in)
    return pl.pallas_call(
        functools.partial(_mixer_kernel, seq_tile=SEQ_TILE, tiles_per_seq=seq // SEQ_TILE),
        grid=(n_tiles + 1,),
        in_specs=[ahead(0), ahead(1), tile(LANES), tile(LANES)] + [full(a) for a in tables]
        + [_layer_block(rows, layer), _layer_block(wpool, layer), resident(w_in)],
        out_specs=tile(MIX_WIDTH),
        out_shape=jax.ShapeDtypeStruct((t, MIX_WIDTH), BF16),
        scratch_shapes=[
            pltpu.VMEM((D_MODEL, Z_WIDTH), BF16),
            pltpu.VMEM((SEQ_TILE, D_MODEL), BF16),
            pltpu.VMEM((SEQ_TILE, Z_WIDTH), F32),
            pltpu.VMEM((SEQ_TILE, Z_WIDTH), F32),
            pltpu.VMEM((SEQ_TILE + HIST_ROWS, HIST_COLS), F32),
            pltpu.VMEM((RET_PAIRS, LANES, LANES), F32),
            pltpu.VMEM((MLSTM_HEADS, LANES, LANES), F32),
            pltpu.VMEM((8, LANES), F32),
        ],
        compiler_params=pltpu.CompilerParams(
            dimension_semantics=("arbitrary",), vmem_limit_bytes=VMEM_LIMIT),
        name="token_mixers",
    )(x, x, cos_t, sin_t, *consts)


FF_CHUNK = 256


def _channel_kernel(x_ref, mix_ref, p_ref, rows_ref, wout_ref, wgu_ref, wdown_ref,
                    wpg_ref, wpp_ref, o_ref, *, final):
    vec = lambda r: rows_ref[r:r + 1, :]
    x1 = x_ref[...] + _dot(mix_ref[...], wout_ref[...])
    h = _rmsnorm(x1, vec(R_NORM_FFN)).astype(BF16)
    acc = x1
    for c0 in range(0, D_FF, FF_CHUNK):
        c1 = min(c0 + FF_CHUNK, D_FF)
        gate = _dot(h, wgu_ref[:, c0:c1])
        up = _dot(h, wgu_ref[:, D_FF + c0:D_FF + c1])
        act = (_silu(gate) * up).astype(BF16)
        acc = acc + _dot(act, wdown_ref[c0:c1, :])
    hp = _rmsnorm(acc, vec(R_NORM_PLE)).astype(BF16)
    emb = _dot(p_ref[...].astype(BF16), wpp_ref[...])
    x3 = acc + _sigmoid(_dot(hp, wpg_ref[...])) * emb
    if final:
        x3 = _rmsnorm(x3, vec(R_NORM_FINAL))
    o_ref[...] = x3


def _channel(x, mix, p, rows, wout, wgu, wdown, wpg, wpp, layer, final):
    t = x.shape[0]
    tile = lambda w: pl.BlockSpec((ROW_TILE, w), lambda i: (i, 0))
    resident = lambda a: pl.BlockSpec((None,) + a.shape[1:], lambda i: (layer,) + (0,) * (a.ndim - 1),
                                      pipeline_mode=pl.Buffered(1))
    consts = (rows, wout, wgu, wdown, wpg, wpp)
    return pl.pallas_call(
        functools.partial(_channel_kernel, final=final),
        grid=(t // ROW_TILE,),
        in_specs=[tile(D_MODEL), tile(MIX_WIDTH),
                  pl.BlockSpec((None, ROW_TILE, PLE_DIM), lambda i: (layer, i, 0))]
        + [resident(a) for a in consts],
        out_specs=tile(D_MODEL),
        out_shape=jax.ShapeDtypeStruct((t, D_MODEL), F32),
        compiler_params=pltpu.CompilerParams(
            dimension_semantics=("arbitrary",), vmem_limit_bytes=VMEM_LIMIT),
        name="channel_mixing",
    )(x, mix, p, *consts)


def _head_segments(v):
    zeros = jnp.zeros(v.shape[:-1] + (HEAD_PAD - MLSTM_HEAD_DIM,), v.dtype)
    out = []
    for h in range(v.shape[-1] // MLSTM_HEAD_DIM):
        out += [v[..., h * MLSTM_HEAD_DIM:(h + 1) * MLSTM_HEAD_DIM], zeros]
    return out


def _block_diag(w):
    depth, g, d, _ = w.shape
    eye = jnp.asarray(np.eye(g, dtype=np.float32))
    return (w[:, :, :, None, :] * eye[None, :, None, :, None]).reshape(depth, g * d, g * d)


def _pack_rows(depth, rows):
    segs = []
    for row in rows:
        width = sum(s.shape[-1] for s in row)
        segs += list(row) + [jnp.zeros((depth, D_MODEL - width), F32)]
    segs.append(jnp.zeros((depth, (N_ROWS - len(rows)) * D_MODEL), F32))
    return jnp.concatenate(segs, axis=-1).reshape(depth, N_ROWS, D_MODEL)


def kernel(x, p, positions, norm_mix, w_in, ret_gn, pool_w, pool_scale, conv_w, conv_b, b_igate, b_fgate,
           mlstm_gn, w_out, norm_ffn, w_gate_up, w_down, norm_ple, w_ple_gate, w_ple_proj, norm_final):
    batch, seq, d = x.shape
    depth = w_in.shape[0]
    t = batch * seq
    xf = x.reshape(t, d)
    cos_t, sin_t = _trig_tables(positions)
    tables = _retention_tables()

    rows = _pack_rows(depth, [
        [norm_mix], [ret_gn], [pool_scale], _head_segments(conv_b), [mlstm_gn],
        [b_igate, b_fgate], [norm_ffn], [norm_ple], [jnp.broadcast_to(norm_final, (depth, d))],
    ] + [_head_segments(conv_w[:, k]) for k in range(MLSTM_CONV)])
    wpool = _block_diag(pool_w).astype(BF16)
    w_in_t = jnp.swapaxes(w_in, 1, 2)
    wout = w_out.astype(BF16)
    wgu, wdown = w_gate_up.astype(BF16), w_down.astype(BF16)
    wpg, wpp = w_ple_gate.astype(BF16), w_ple_proj.astype(BF16)
    pf = p.reshape(depth, t, PLE_DIM)

    for i in range(depth):
        mix = _mixers(xf, cos_t, sin_t, tables, rows, wpool, w_in_t, i, seq)
        xf = _channel(xf, mix, pf, rows, wout, wgu, wdown, wpg, wpp, i, final=(i == depth - 1))
    return xf.reshape(batch, seq, d)
```

```python
import functools

import numpy as np
import jax
import jax.numpy as jnp
from jax import lax
from jax.experimental import pallas as pl
from jax.experimental.pallas import tpu as pltpu

F32 = jnp.float32
BF16 = jnp.bfloat16

D_MODEL = 1024
PLE_DIM = 256
RET_HEADS = 6
RET_HEAD_DIM = 64
RET_WIDTH = RET_HEADS * RET_HEAD_DIM
RET_PAIRS = RET_HEADS // 2
POOL_WINDOWS = (2, 4, 8, 16)
POOL_GROUP_DIM = 64
POOL_WIDTH = len(POOL_WINDOWS) * POOL_GROUP_DIM
MLSTM_HEADS = 4
MLSTM_HEAD_DIM = 96
MLSTM_WIDTH = MLSTM_HEADS * MLSTM_HEAD_DIM
MLSTM_CONV = 4
CHUNK = 128
D_FF = 2816
ROPE_BASE = 10000.0
EPS = 1e-6

LANES = 128
HEAD_PAD = LANES
MLSTM_WIDTH_P = MLSTM_HEADS * HEAD_PAD

RQ = 0
RK = RQ + RET_WIDTH
RV = RK + RET_WIDTH
RG = RV + RET_WIDTH
PU = RG + RET_WIDTH
MQ = PU + POOL_WIDTH
MK = MQ + MLSTM_WIDTH_P
MV = MK + MLSTM_WIDTH_P
MO = MV + MLSTM_WIDTH
GT = MO + MLSTM_WIDTH
Z_WIDTH = GT + LANES
HIST_COLS = MV - PU
HIST_ROWS = 16
MIX_WIDTH = RET_WIDTH + POOL_WIDTH + MLSTM_WIDTH
NEG = -1e30

ROW_TILE = 1024
SEQ_TILE = 512
VMEM_LIMIT = 56 * 1024 * 1024

(R_NORM_MIX, R_RET_GN, R_POOL_SCALE, R_CONV_B, R_MLSTM_GN, R_GATE_BIAS, R_NORM_FFN, R_NORM_PLE,
 R_NORM_FINAL, R_CONV_W) = range(10)
N_ROWS = 16


def _rmsnorm(x, g):
    return x * lax.rsqrt(jnp.mean(x * x, axis=-1, keepdims=True) + EPS) * g


def _sigmoid(x):
    return 1.0 / (1.0 + jnp.exp(-x))


def _silu(x):
    half = 0.5 * x
    return half + half * jnp.tanh(half)


def _dot(a, b):
    return jnp.dot(a, b, preferred_element_type=F32)


def _dot_nt(a, b):
    return lax.dot_general(a, b, (((1,), (1,)), ((), ())), preferred_element_type=F32)


TOKENS_PER_ROW = LANES // (RET_HEAD_DIM // 2)


def _trig_kernel(pos_ref, inv_ref, spread_ref, sign_ref, cos_ref, sin_ref):
    pos = pos_ref[...]
    ang = pos[:, 0:1] * inv_ref[0:1, :]
    for r in range(1, TOKENS_PER_ROW):
        ang = ang + pos[:, r:r + 1] * inv_ref[r:r + 1, :]
    n = ang.shape[0]
    for val, out_ref, sign in ((jnp.cos(ang), cos_ref, None), (jnp.sin(ang), sin_ref, sign_ref[...])):
        hi = val.astype(BF16)
        parts = jnp.concatenate([hi, (val - hi.astype(F32)).astype(BF16)], axis=1)
        for r in range(TOKENS_PER_ROW):
            wide = _dot(parts, spread_ref[r])
            out_ref[pl.ds(r, n, stride=TOKENS_PER_ROW), :] = wide if sign is None else wide * sign


def _trig_tables(positions):
    t = positions.size
    half = RET_HEAD_DIM // 2
    inv = ROPE_BASE ** (-jnp.arange(half, dtype=F32) / half)
    eye = jnp.asarray(np.eye(TOKENS_PER_ROW, dtype=np.float32))
    inv_rows = (eye[:, :, None] * inv[None, None, :]).reshape(TOKENS_PER_ROW, LANES)
    lanes = np.arange(LANES)
    spread = np.stack([(np.arange(LANES)[:, None] == half * r + lanes[None, :] % half) for r in range(TOKENS_PER_ROW)])
    spread = jnp.asarray(np.concatenate([spread, spread], axis=1).astype(np.float32)).astype(BF16)
    sign = np.where((lanes % RET_HEAD_DIM) < half, -1.0, 1.0).astype(np.float32).reshape(1, LANES)
    tile = min(2 * ROW_TILE, t)
    rows = tile // TOKENS_PER_ROW
    full = lambda a: pl.BlockSpec(a.shape, lambda i: (0,) * a.ndim)
    out = pl.BlockSpec((tile, LANES), lambda i: (i, 0))
    pos = positions.astype(F32).reshape(t // TOKENS_PER_ROW, TOKENS_PER_ROW)
    sign = jnp.asarray(sign)
    return pl.pallas_call(
        _trig_kernel,
        grid=(t // tile,),
        in_specs=[pl.BlockSpec((rows, TOKENS_PER_ROW), lambda i: (i, 0)), full(inv_rows), full(spread), full(sign)],
        out_specs=[out, out],
        out_shape=[jax.ShapeDtypeStruct((t, LANES), F32)] * 2,
        compiler_params=pltpu.CompilerParams(dimension_semantics=("arbitrary",)),
        name="rope_tables",
    )(pos, inv_rows, spread, sign)


def _layer_block(a, layer):
    return pl.BlockSpec((None,) + a.shape[1:], lambda *_: (layer,) + (0,) * (a.ndim - 1))


def _swap_halves(x, lo_half):
    return jnp.where(lo_half, pltpu.roll(x, LANES - RET_HEAD_DIM // 2, 1), pltpu.roll(x, RET_HEAD_DIM // 2, 1))


def _lane_scan(x, op, fill):
    lane = lax.broadcasted_iota(jnp.int32, x.shape, 1)
    sh = 1
    while sh < LANES:
        x = op(x, jnp.where(lane >= sh, pltpu.roll(x, sh, 1), fill))
        sh *= 2
    return x


def _mixer_kernel(x_ref, xnext_ref, cos_ref, sin_ref, dtab_ref, xi_ref, zeta_ref, cd_ref, bd_ref, avg_ref,
                  rows_ref, wpool_ref, wraw_ref,
                  mix_ref, win_ref, h_ref, z_even, z_odd, hbuf, rstate, cstate, mstate, *, seq_tile, tiles_per_seq):
    j = pl.program_id(0)
    tile_in_seq = jnp.maximum(j - 1, 0) % tiles_per_seq
    norm_in = lambda ref: _rmsnorm(ref[...], rows_ref[R_NORM_MIX:R_NORM_MIX + 1, :]).astype(BF16)

    @pl.when(j == 0)
    def _():
        h_ref[...] = norm_in(x_ref)
        def put(col, src, n):
            block = wraw_ref[src:src + n, :]
            if n < LANES:
                block = jnp.concatenate([block, jnp.zeros((LANES - n, D_MODEL), F32)], axis=0)
            win_ref[:, col:col + LANES] = block.T.astype(BF16)

        o = 4 * RET_WIDTH + POOL_WIDTH
        for c0 in range(0, o, LANES):
            put(c0, c0, LANES)
        for k in range(2 * MLSTM_HEADS):
            put(o + k * HEAD_PAD, o + k * MLSTM_HEAD_DIM, MLSTM_HEAD_DIM)
        src = o + 2 * MLSTM_WIDTH
        for c0 in range(0, 2 * MLSTM_WIDTH, LANES):
            put(MV + c0, src + c0, LANES)
        put(GT, src + 2 * MLSTM_WIDTH, 2 * MLSTM_HEADS)
        z_odd[...] = jnp.zeros_like(z_odd)

    @pl.when(tile_in_seq == 0)
    def _():
        hbuf[0:HIST_ROWS, :] = jnp.zeros((HIST_ROWS, HIST_COLS), F32)
        rstate[...] = jnp.zeros_like(rstate)
        cstate[...] = jnp.zeros_like(cstate)
        mstate[...] = jnp.zeros_like(mstate)

    def step(z_next, z_ref):
        def project(c0, c1):
            z_next[:, c0:c1] = _dot(h_ref[...], win_ref[:, c0:c1])

        slabs = [functools.partial(project, c0, min(c0 + PROJ_SLAB, Z_WIDTH)) for c0 in range(0, Z_WIDTH, PROJ_SLAB)]
        _mixer_tile(z_ref, cos_ref, sin_ref, dtab_ref, xi_ref, zeta_ref, cd_ref, bd_ref, avg_ref, rows_ref,
                    wpool_ref, mix_ref, hbuf, rstate, cstate, mstate, tile_in_seq, seq_tile,
                    _Interleave(slabs, (seq_tile // CHUNK) * STAGES_PER_CHUNK))
        h_ref[...] = norm_in(xnext_ref)

    @pl.when(j % 2 == 0)
    def _():
        step(z_even, z_odd)

    @pl.when(j % 2 == 1)
    def _():
        step(z_odd, z_even)


PROJ_SLAB = 512
STAGES_PER_CHUNK = 15


class _Interleave:
    def __init__(self, thunks, n_points):
        self.thunks, self.n_points, self.point, self.done = thunks, n_points, 0, 0

    def tick(self):
        self.point += 1
        due = len(self.thunks) if self.point >= self.n_points else (self.point * len(self.thunks)) // self.n_points
        while self.done < due:
            self.thunks[self.done]()
            self.done += 1

    def finish(self):
        self.point = self.n_points - 1
        self.tick()


def _mixer_tile(z_ref, cos_ref, sin_ref, dtab_ref, xi_ref, zeta_ref, cd_ref, bd_ref, avg_ref, rows_ref,
                wpool_ref, mix_ref, hbuf, rstate, cstate, mstate, j, seq_tile, other_work):
    vec = lambda r, n: rows_ref[r:r + 1, 0:n]
    hbuf[HIST_ROWS:, :] = z_ref[:, PU:MV]

    lane = lax.broadcasted_iota(jnp.int32, (CHUNK, LANES), 1)
    row_i = lax.broadcasted_iota(jnp.int32, (CHUNK, LANES), 0)
    lane_row = lax.broadcasted_iota(jnp.int32, (1, LANES), 1)
    in_a = (lane_row < RET_HEAD_DIM).astype(BF16)
    in_b = (lane_row >= RET_HEAD_DIM).astype(BF16)
    lo_half = (lane % RET_HEAD_DIM) < (RET_HEAD_DIM // 2)
    key_le_query = row_i <= lane
    lane_p = lax.broadcasted_iota(jnp.int32, (CHUNK, POOL_WIDTH), 1)
    row_p = lax.broadcasted_iota(jnp.int32, (CHUNK, POOL_WIDTH), 0)

    def pool_stages(r0, rows):
        ext = hbuf[r0:r0 + HIST_ROWS + CHUNK, 0:POOL_WIDTH]
        s2 = ext + pltpu.roll(ext, 1, 0)
        s4 = s2 + pltpu.roll(s2, 2, 0)
        s8 = s4 + pltpu.roll(s4, 4, 0)
        s16 = s8 + pltpu.roll(s8, 8, 0)
        u, s2, s4, s8, s16 = (a[HIST_ROWS:] for a in (ext, s2, s4, s8, s16))
        yield
        g0, g1, g2 = (lane_p < POOL_GROUP_DIM, lane_p < 2 * POOL_GROUP_DIM, lane_p < 3 * POOL_GROUP_DIM)
        wsum = jnp.where(g0, s2, jnp.where(g1, s4, jnp.where(g2, s8, s16)))
        width = jnp.where(g0, 2, jnp.where(g1, 4, jnp.where(g2, 8, 16)))
        tpos = row_p + (j * seq_tile + r0 + 1)
        count = jnp.minimum(tpos, width).astype(F32)
        pooled = wsum / count - u
        y_pool = _dot(pooled.astype(BF16), wpool_ref[...]) * vec(R_POOL_SCALE, POOL_WIDTH)
        mix_ref[rows, RET_WIDTH:RET_WIDTH + POOL_WIDTH] = y_pool.astype(BF16)

    def retention_stages(rows):
        pairs = range(RET_PAIRS)
        col = lambda base, p: z_ref[rows, base + p * LANES:base + (p + 1) * LANES]
        cosv = cos_ref[rows, :]
        sinv = sin_ref[rows, :]
        rope = lambda a: a * cosv + _swap_halves(a, lo_half) * sinv
        q = [rope(col(RQ, p)) for p in pairs]
        k = [rope(col(RK, p)) for p in pairs]
        yield
        kb = [k[p].astype(BF16) for p in pairs]
        vb = [col(RV, p).astype(BF16) for p in pairs]
        k2 = [jnp.concatenate([kb[p] * in_a, kb[p] * in_b], axis=0) for p in pairs]
        v2 = [jnp.concatenate([vb[p] * in_a, vb[p] * in_b], axis=0) for p in pairs]
        scores = [_dot_nt(q[p].astype(BF16), k2[p]) for p in pairs]
        kz_t = [(k[p] * zeta_ref[p]).T.astype(BF16) for p in pairs]
        yield
        lhs = [jnp.concatenate([scores[p].astype(BF16) * dtab_ref[p], (q[p] * xi_ref[p]).astype(BF16)], axis=1)
               for p in pairs]
        rhs = [jnp.concatenate([v2[p], rstate[p].astype(BF16)], axis=0) for p in pairs]
        ys = [_dot(lhs[p], rhs[p]) for p in pairs]
        kv = [_dot(kz_t[p], vb[p]) for p in pairs]
        for p in pairs:
            rstate[p] = rstate[p] * cd_ref[p] + kv[p] * bd_ref[...]
        yield
        y = jnp.concatenate(ys, axis=0)
        first = lax.broadcasted_iota(jnp.int32, y.shape, 1) < RET_HEAD_DIM

        def half_mean(a):
            sa = jnp.sum(jnp.where(first, a, 0.0), axis=1, keepdims=True)
            sb = jnp.sum(jnp.where(first, 0.0, a), axis=1, keepdims=True)
            return jnp.where(first, sa, sb) * (1.0 / RET_HEAD_DIM)

        yc = y - half_mean(y)
        yield
        var = half_mean(yc * yc)
        yn = yc * lax.rsqrt(var + EPS)
        for p in pairs:
            cs = slice(p * LANES, (p + 1) * LANES)
            gain = rows_ref[R_RET_GN:R_RET_GN + 1, cs]
            mix_ref[rows, cs] = (_silu(col(RG, p)) * (yn[p * CHUNK:(p + 1) * CHUNK] * gain)).astype(BF16)

    def mlstm_stages(r0, rows):
        gates_t = (z_ref[rows, GT:GT + LANES] + vec(R_GATE_BIAS, LANES)).T
        li = gates_t[0:8]
        fpre = pltpu.roll(li, MLSTM_HEADS, 0)
        lf = jnp.minimum(fpre, 0.0) - jnp.log(1.0 + jnp.exp(-jnp.abs(fpre)))
        bcum = _lane_scan(lf, jnp.add, 0.0)
        g = li - bcum
        cmax = _lane_scan(g, jnp.maximum, NEG)
        m_prev = mstate[...]
        big_g = jnp.maximum(m_prev, cmax)
        inter = jnp.exp(m_prev - big_g)
        emr = jnp.exp(-(bcum + big_g))
        g_last = jnp.broadcast_to(big_g[:, LANES - 1:LANES], (8, LANES))
        b_last = jnp.broadcast_to(bcum[:, LANES - 1:LANES], (8, LANES))
        wk = jnp.exp(g - g_last)
        s_old = jnp.exp(m_prev - g_last)
        mstate[...] = b_last + g_last
        g_cols = jnp.concatenate([g, jnp.zeros((CHUNK - 8, LANES), F32)], axis=0).T
        yield
        ext = hbuf[r0 + HIST_ROWS - 8:r0 + HIST_ROWS + CHUNK, POOL_WIDTH:HIST_COLS]
        tap = lambda kk: vec(R_CONV_W + kk, 2 * MLSTM_WIDTH_P)
        prev = pltpu.roll(ext, 1, 0)
        older = pltpu.roll(tap(1) * ext + tap(0) * prev, 2, 0)
        yield
        conv = vec(R_CONV_B, 2 * MLSTM_WIDTH_P) + tap(3) * ext[8:] + tap(2) * prev[8:] + older[8:]
        qk = _silu(conv)
        yield
        v_all_t = z_ref[rows, MV:MV + MLSTM_WIDTH].T
        ones_row = jnp.where(lax.broadcasted_iota(jnp.int32, (8, LANES), 0) == 0, 1.0, 0.0)
        v_tail = jnp.concatenate([ones_row, jnp.zeros((HEAD_PAD - MLSTM_HEAD_DIM - 8, LANES), F32)], axis=0)
        heads = range(MLSTM_HEADS)
        row = lambda a, h: a[h:h + 1, :]
        qh = [(qk[:, h * HEAD_PAD:(h + 1) * HEAD_PAD] * (MLSTM_HEAD_DIM ** -0.5)).astype(BF16) for h in heads]
        kh = [qk[:, MLSTM_WIDTH_P + h * HEAD_PAD:MLSTM_WIDTH_P + (h + 1) * HEAD_PAD].astype(BF16) for h in heads]
        v_t = [jnp.concatenate([v_all_t[h * MLSTM_HEAD_DIM:(h + 1) * MLSTM_HEAD_DIM], v_tail], axis=0) for h in heads]
        c_t = [cstate[h] for h in heads]
        scores_t = [_dot_nt(kh[h], qh[h]) for h in heads]
        cross_t = [_dot_nt(c_t[h].astype(BF16), qh[h]) for h in heads]
        kv_t = [_dot((v_t[h] * row(wk, h)).astype(BF16), kh[h]) for h in heads]
        yield
        for h in heads:
            cstate[h] = c_t[h] * row(s_old, h) + kv_t[h]
        decay_t = [jnp.exp(jnp.where(key_le_query, g_cols[:, h:h + 1] - row(big_g, h), NEG)) for h in heads]
        sc_t = [(scores_t[h] * decay_t[h]).astype(BF16) for h in heads]
        yield
        nd_t = [_dot(v_t[h].astype(BF16), sc_t[h]) + cross_t[h] * row(inter, h) for h in heads]
        yield
        hn_ts = []
        for h in heads:
            den = nd_t[h][MLSTM_HEAD_DIM:MLSTM_HEAD_DIM + 1, :]
            h_t = nd_t[h][0:MLSTM_HEAD_DIM, :] * (1.0 / jnp.maximum(jnp.abs(den), row(emr, h)))
            mu = jnp.sum(h_t, axis=0, keepdims=True) / MLSTM_HEAD_DIM
            hc = h_t - mu
            var = jnp.sum(hc * hc, axis=0, keepdims=True) / MLSTM_HEAD_DIM
            hn_ts.append(hc * lax.rsqrt(var + EPS))
        yield
        hn = jnp.concatenate(hn_ts, axis=0).T
        o_gate = _sigmoid(z_ref[rows, MO:MO + MLSTM_WIDTH])
        mix_ref[rows, RET_WIDTH + POOL_WIDTH:MIX_WIDTH] = (
            o_gate * (hn * vec(R_MLSTM_GN, MLSTM_WIDTH))).astype(BF16)

    for c in range(seq_tile // CHUNK):
        r0 = c * CHUNK
        rows = pl.ds(r0, CHUNK)
        streams = [mlstm_stages(r0, rows), retention_stages(rows), pool_stages(r0, rows)]
        while streams:
            for s in list(streams):
                if next(s, StopIteration) is StopIteration:
                    streams.remove(s)
                other_work.tick()
    other_work.finish()

    hbuf[0:HIST_ROWS, :] = hbuf[seq_tile:seq_tile + HIST_ROWS, :]


def _retention_tables():
    lg = np.log1p(-(2.0 ** (-5.0 - np.arange(RET_HEADS, dtype=np.float64))))
    idx = np.arange(CHUNK, dtype=np.float64)
    rel = idx[:, None] - idx[None, :]
    head_of_lane = np.arange(LANES) // RET_HEAD_DIM
    dtab, xi, zeta, cd = [], [], [], []
    same = (head_of_lane[:, None] == head_of_lane[None, :])
    key_scale = RET_HEAD_DIM ** -0.5
    for p in range(RET_PAIRS):
        hl = lg[2 * p + head_of_lane]
        dtab.append(key_scale * np.concatenate(
            [np.where(rel >= 0, np.exp(lg[2 * p + a] * np.maximum(rel, 0.0)), 0.0) for a in range(2)], axis=1))
        xi.append(np.exp(hl[None, :] * (idx[:, None] + 1.0)))
        zeta.append(key_scale * np.exp(hl[None, :] * (CHUNK - 1 - idx[:, None])))
        cd.append(np.where(same, np.exp(hl * CHUNK)[:, None], 0.0))
    f = lambda a: jnp.asarray(np.stack(a).astype(np.float32))
    avg = jnp.asarray((same / RET_HEAD_DIM).astype(np.float32)).astype(BF16)
    return f(dtab).astype(BF16), f(xi), f(zeta), f(cd), jnp.asarray(same.astype(np.float32)), avg


def _mixers(x, cos_t, sin_t, tables, rows, wpool, w_in, layer, seq):
    t = x.shape[0]
    n_tiles = t // SEQ_TILE
    ahead = lambda k: pl.BlockSpec((SEQ_TILE, D_MODEL), lambda j: (jnp.minimum(j + k, n_tiles - 1), 0))
    tile = lambda w: pl.BlockSpec((SEQ_TILE, w), lambda j: (jnp.maximum(j - 1, 0), 0))
    full = lambda a: pl.BlockSpec(a.shape, lambda j: (0,) * a.ndim)
    resident = lambda a: pl.BlockSpec((None,) + a.shape[1:], lambda j: (layer,) + (0,) * (a.ndim - 1),
                                      pipeline_mode=pl.Buffered(1))
    consts = tuple(tables) + (rows, wpool, w_in)
    return pl.pallas_call(
        functools.partial(_mixer_kernel, seq_tile=SEQ_TILE, tiles_per_seq=seq // SEQ_TILE),
        grid=(n_tiles + 1,),
        in_specs=[ahead(0), ahead(1), tile(LANES), tile(LANES)] + [full(a) for a in tables]
        + [_layer_block(rows, layer), _layer_block(wpool, layer), resident(w_in)],
        out_specs=tile(MIX_WIDTH),
        out_shape=jax.ShapeDtypeStruct((t, MIX_WIDTH), BF16),
        scratch_shapes=[
            pltpu.VMEM((D_MODEL, Z_WIDTH), BF16),
            pltpu.VMEM((SEQ_TILE, D_MODEL), BF16),
            pltpu.VMEM((SEQ_TILE, Z_WIDTH), F32),
            pltpu.VMEM((SEQ_TILE, Z_WIDTH), F32),
            pltpu.VMEM((SEQ_TILE + HIST_ROWS, HIST_COLS), F32),
            pltpu.VMEM((RET_PAIRS, LANES, LANES), F32),
            pltpu.VMEM((MLSTM_HEADS, LANES, LANES), F32),
            pltpu.VMEM((8, LANES), F32),
        ],
        compiler_params=pltpu.CompilerParams(
            dimension_semantics=("arbitrary",), vmem_limit_bytes=VMEM_LIMIT),
        name="token_mixers",
    )(x, x, cos_t, sin_t, *consts)


FF_CHUNK = 256


def _channel_kernel(x_ref, mix_ref, p_ref, rows_ref, wout_ref, wgu_ref, wdown_ref,
                    wpg_ref, wpp_ref, o_ref, *, final):
    vec = lambda r: rows_ref[r:r + 1, :]
    x1 = x_ref[...] + _dot(mix_ref[...], wout_ref[...])
    h = _rmsnorm(x1, vec(R_NORM_FFN)).astype(BF16)
    acc = x1
    for c0 in range(0, D_FF, FF_CHUNK):
        c1 = min(c0 + FF_CHUNK, D_FF)
        gate = _dot(h, wgu_ref[:, c0:c1])
        up = _dot(h, wgu_ref[:, D_FF + c0:D_FF + c1])
        act = (_silu(gate) * up).astype(BF16)
        acc = acc + _dot(act, wdown_ref[c0:c1, :])
    hp = _rmsnorm(acc, vec(R_NORM_PLE)).astype(BF16)
    emb = _dot(p_ref[...].astype(BF16), wpp_ref[...])
    x3 = acc + _sigmoid(_dot(hp, wpg_ref[...])) * emb
    if final:
        x3 = _rmsnorm(x3, vec(R_NORM_FINAL))
    o_ref[...] = x3


def _channel(x, mix, p, rows, wout, wgu, wdown, wpg, wpp, layer, final):
    t = x.shape[0]
    tile = lambda w: pl.BlockSpec((ROW_TILE, w), lambda i: (i, 0))
    resident = lambda a: pl.BlockSpec((None,) + a.shape[1:], lambda i: (layer,) + (0,) * (a.ndim - 1),
                                      pipeline_mode=pl.Buffered(1))
    consts = (rows, wout, wgu, wdown, wpg, wpp)
    return pl.pallas_call(
        functools.partial(_channel_kernel, final=final),
        grid=(t // ROW_TILE,),
        in_specs=[tile(D_MODEL), tile(MIX_WIDTH),
                  pl.BlockSpec((None, ROW_TILE, PLE_DIM), lambda i: (layer, i, 0))]
        + [resident(a) for a in consts],
        out_specs=tile(D_MODEL),
        out_shape=jax.ShapeDtypeStruct((t, D_MODEL), F32),
        compiler_params=pltpu.CompilerParams(
            dimension_semantics=("arbitrary",), vmem_limit_bytes=VMEM_LIMIT),
        name="channel_mixing",
    )(x, mix, p, *consts)


def _head_segments(v):
    zeros = jnp.zeros(v.shape[:-1] + (HEAD_PAD - MLSTM_HEAD_DIM,), v.dtype)
    out = []
    for h in range(v.shape[-1] // MLSTM_HEAD_DIM):
        out += [v[..., h * MLSTM_HEAD_DIM:(h + 1) * MLSTM_HEAD_DIM], zeros]
    return out


def _block_diag(w):
    depth, g, d, _ = w.shape
    eye = jnp.asarray(np.eye(g, dtype=np.float32))
    return (w[:, :, :, None, :] * eye[None, :, None, :, None]).reshape(depth, g * d, g * d)


def _pack_rows(depth, rows):
    segs = []
    for row in rows:
        width = sum(s.shape[-1] for s in row)
        segs += list(row) + [jnp.zeros((depth, D_MODEL - width), F32)]
    segs.append(jnp.zeros((depth, (N_ROWS - len(rows)) * D_MODEL), F32))
    return jnp.concatenate(segs, axis=-1).reshape(depth, N_ROWS, D_MODEL)


def kernel(x, p, positions, norm_mix, w_in, ret_gn, pool_w, pool_scale, conv_w, conv_b, b_igate, b_fgate,
           mlstm_gn, w_out, norm_ffn, w_gate_up, w_down, norm_ple, w_ple_gate, w_ple_proj, norm_final):
    batch, seq, d = x.shape
    depth = w_in.shape[0]
    t = batch * seq
    xf = x.reshape(t, d)
    cos_t, sin_t = _trig_tables(positions)
    tables = _retention_tables()

    rows = _pack_rows(depth, [
        [norm_mix], [ret_gn], [pool_scale], _head_segments(conv_b), [mlstm_gn],
        [b_igate, b_fgate], [norm_ffn], [norm_ple], [jnp.broadcast_to(norm_final, (depth, d))],
    ] + [_head_segments(conv_w[:, k]) for k in range(MLSTM_CONV)])
    wpool = _block_diag(pool_w).astype(BF16)
    w_in_t = jnp.swapaxes(w_in, 1, 2)
    wout = w_out.astype(BF16)
    wgu, wdown = w_gate_up.astype(BF16), w_down.astype(BF16)
    wpg, wpp = w_ple_gate.astype(BF16), w_ple_proj.astype(BF16)
    pf = p.reshape(depth, t, PLE_DIM)

    for i in range(depth):
        mix = _mixers(xf, cos_t, sin_t, tables, rows, wpool, w_in_t, i, seq)
        xf = _channel(xf, mix, pf, rows, wout, wgu, wdown, wpg, wpp, i, final=(i == depth - 1))
    return xf.reshape(batch, seq, d)
```

```python
import functools

import numpy as np
import jax
import jax.numpy as jnp
from jax import lax
from jax.experimental import pallas as pl
from jax.experimental.pallas import tpu as pltpu

F32 = jnp.float32
BF16 = jnp.bfloat16

D_MODEL = 1024
PLE_DIM = 256
RET_HEADS = 6
RET_HEAD_DIM = 64
RET_WIDTH = RET_HEADS * RET_HEAD_DIM
RET_PAIRS = RET_HEADS // 2
POOL_WINDOWS = (2, 4, 8, 16)
POOL_GROUP_DIM = 64
POOL_WIDTH = len(POOL_WINDOWS) * POOL_GROUP_DIM
MLSTM_HEADS = 4
MLSTM_HEAD_DIM = 96
MLSTM_WIDTH = MLSTM_HEADS * MLSTM_HEAD_DIM
MLSTM_CONV = 4
CHUNK = 128
D_FF = 2816
ROPE_BASE = 10000.0
EPS = 1e-6

LANES = 128
HEAD_PAD = LANES
MLSTM_WIDTH_P = MLSTM_HEADS * HEAD_PAD

RQ = 0
RK = RQ + RET_WIDTH
RV = RK + RET_WIDTH
RG = RV + RET_WIDTH
PU = RG + RET_WIDTH
MQ = PU + POOL_WIDTH
MK = MQ + MLSTM_WIDTH_P
MV = MK + MLSTM_WIDTH_P
MO = MV + MLSTM_WIDTH
GT = MO + MLSTM_WIDTH
Z_WIDTH = GT + LANES
HIST_COLS = MV - PU
HIST_ROWS = 16
MIX_WIDTH = RET_WIDTH + POOL_WIDTH + MLSTM_WIDTH
NEG = -1e30

ROW_TILE = 1024
SEQ_TILE = 512
VMEM_LIMIT = 56 * 1024 * 1024

(R_NORM_MIX, R_RET_GN, R_POOL_SCALE, R_CONV_B, R_MLSTM_GN, R_GATE_BIAS, R_NORM_FFN, R_NORM_PLE,
 R_NORM_FINAL, R_CONV_W) = range(10)
N_ROWS = 16


def _rmsnorm(x, g):
    return x * lax.rsqrt(jnp.mean(x * x, axis=-1, keepdims=True) + EPS) * g


def _sigmoid(x):
    return 1.0 / (1.0 + jnp.exp(-x))


def _silu(x):
    half = 0.5 * x
    return half + half * jnp.tanh(half)


def _dot(a, b):
    return jnp.dot(a, b, preferred_element_type=F32)


def _dot_nt(a, b):
    return lax.dot_general(a, b, (((1,), (1,)), ((), ())), preferred_element_type=F32)


TOKENS_PER_ROW = LANES // (RET_HEAD_DIM // 2)


def _trig_kernel(pos_ref, inv_ref, spread_ref, sign_ref, cos_ref, sin_ref):
    pos = pos_ref[...]
    ang = pos[:, 0:1] * inv_ref[0:1, :]
    for r in range(1, TOKENS_PER_ROW):
        ang = ang + pos[:, r:r + 1] * inv_ref[r:r + 1, :]
    n = ang.shape[0]
    for val, out_ref, sign in ((jnp.cos(ang), cos_ref, None), (jnp.sin(ang), sin_ref, sign_ref[...])):
        hi = val.astype(BF16)
        parts = jnp.concatenate([hi, (val - hi.astype(F32)).astype(BF16)], axis=1)
        for r in range(TOKENS_PER_ROW):
            wide = _dot(parts, spread_ref[r])
            out_ref[pl.ds(r, n, stride=TOKENS_PER_ROW), :] = wide if sign is None else wide * sign


def _trig_tables(positions):
    t = positions.size
    half = RET_HEAD_DIM // 2
    inv = ROPE_BASE ** (-jnp.arange(half, dtype=F32) / half)
    eye = jnp.asarray(np.eye(TOKENS_PER_ROW, dtype=np.float32))
    inv_rows = (eye[:, :, None] * inv[None, None, :]).reshape(TOKENS_PER_ROW, LANES)
    lanes = np.arange(LANES)
    spread = np.stack([(np.arange(LANES)[:, None] == half * r + lanes[None, :] % half) for r in range(TOKENS_PER_ROW)])
    spread = jnp.asarray(np.concatenate([spread, spread], axis=1).astype(np.float32)).astype(BF16)
    sign = np.where((lanes % RET_HEAD_DIM) < half, -1.0, 1.0).astype(np.float32).reshape(1, LANES)
    tile = min(2 * ROW_TILE, t)
    rows = tile // TOKENS_PER_ROW
    full = lambda a: pl.BlockSpec(a.shape, lambda i: (0,) * a.ndim)
    out = pl.BlockSpec((tile, LANES), lambda i: (i, 0))
    pos = positions.astype(F32).reshape(t // TOKENS_PER_ROW, TOKENS_PER_ROW)
    sign = jnp.asarray(sign)
    return pl.pallas_call(
        _trig_kernel,
        grid=(t // tile,),
        in_specs=[pl.BlockSpec((rows, TOKENS_PER_ROW), lambda i: (i, 0)), full(inv_rows), full(spread), full(sign)],
        out_specs=[out, out],
        out_shape=[jax.ShapeDtypeStruct((t, LANES), F32)] * 2,
        compiler_params=pltpu.CompilerParams(dimension_semantics=("arbitrary",)),
        name="rope_tables",
    )(pos, inv_rows, spread, sign)


def _layer_block(a, layer):
    return pl.BlockSpec((None,) + a.shape[1:], lambda *_: (layer,) + (0,) * (a.ndim - 1))


def _swap_halves(x, lo_half):
    return jnp.where(lo_half, pltpu.roll(x, LANES - RET_HEAD_DIM // 2, 1), pltpu.roll(x, RET_HEAD_DIM // 2, 1))


def _lane_scan(x, op, fill):
    lane = lax.broadcasted_iota(jnp.int32, x.shape, 1)
    sh = 1
    while sh < LANES:
        x = op(x, jnp.where(lane >= sh, pltpu.roll(x, sh, 1), fill))
        sh *= 2
    return x


def _mixer_kernel(x_ref, xnext_ref, cos_ref, sin_ref, dtab_ref, xi_ref, zeta_ref, cd_ref, bd_ref, avg_ref,
                  rows_ref, wpool_ref, wraw_ref,
                  mix_ref, win_ref, h_ref, z_even, z_odd, hbuf, rstate, cstate, mstate, *, seq_tile, tiles_per_seq):
    j = pl.program_id(0)
    tile_in_seq = jnp.maximum(j - 1, 0) % tiles_per_seq
    norm_in = lambda ref: _rmsnorm(ref[...], rows_ref[R_NORM_MIX:R_NORM_MIX + 1, :]).astype(BF16)

    @pl.when(j == 0)
    def _():
        h_ref[...] = norm_in(x_ref)
        def put(col, src, n):
            block = wraw_ref[src:src + n, :]
            if n < LANES:
                block = jnp.concatenate([block, jnp.zeros((LANES - n, D_MODEL), F32)], axis=0)
            win_ref[:, col:col + LANES] = block.T.astype(BF16)

        o = 4 * RET_WIDTH + POOL_WIDTH
        for c0 in range(0, o, LANES):
            put(c0, c0, LANES)
        for k in range(2 * MLSTM_HEADS):
            put(o + k * HEAD_PAD, o + k * MLSTM_HEAD_DIM, MLSTM_HEAD_DIM)
        src = o + 2 * MLSTM_WIDTH
        for c0 in range(0, 2 * MLSTM_WIDTH, LANES):
            put(MV + c0, src + c0, LANES)
        put(GT, src + 2 * MLSTM_WIDTH, 2 * MLSTM_HEADS)
        z_odd[...] = jnp.zeros_like(z_odd)

    @pl.when(tile_in_seq == 0)
    def _():
        hbuf[0:HIST_ROWS, :] = jnp.zeros((HIST_ROWS, HIST_COLS), F32)
        rstate[...] = jnp.zeros_like(rstate)
        cstate[...] = jnp.zeros_like(cstate)
        mstate[...] = jnp.zeros_like(mstate)

    def step(z_next, z_ref):
        def project(c0, c1):
            z_next[:, c0:c1] = _dot(h_ref[...], win_ref[:, c0:c1])

        slabs = [functools.partial(project, c0, min(c0 + PROJ_SLAB, Z_WIDTH)) for c0 in range(0, Z_WIDTH, PROJ_SLAB)]
        _mixer_tile(z_ref, cos_ref, sin_ref, dtab_ref, xi_ref, zeta_ref, cd_ref, bd_ref, avg_ref, rows_ref,
                    wpool_ref, mix_ref, hbuf, rstate, cstate, mstate, tile_in_seq, seq_tile,
                    _Interleave(slabs, (seq_tile // CHUNK) * STAGES_PER_CHUNK))
        h_ref[...] = norm_in(xnext_ref)

    @pl.when(j % 2 == 0)
    def _():
        step(z_even, z_odd)

    @pl.when(j % 2 == 1)
    def _():
        step(z_odd, z_even)


PROJ_SLAB = 512
STAGES_PER_CHUNK = 15


class _Interleave:
    def __init__(self, thunks, n_points):
        self.thunks, self.n_points, self.point, self.done = thunks, n_points, 0, 0

    def tick(self):
        self.point += 1
        due = len(self.thunks) if self.point >= self.n_points else (self.point * len(self.thunks)) // self.n_points
        while self.done < due:
            self.thunks[self.done]()
            self.done += 1

    def finish(self):
        self.point = self.n_points - 1
        self.tick()


def _mixer_tile(z_ref, cos_ref, sin_ref, dtab_ref, xi_ref, zeta_ref, cd_ref, bd_ref, avg_ref, rows_ref,
                wpool_ref, mix_ref, hbuf, rstate, cstate, mstate, j, seq_tile, other_work):
    vec = lambda r, n: rows_ref[r:r + 1, 0:n]
    hbuf[HIST_ROWS:, :] = z_ref[:, PU:MV]

    lane = lax.broadcasted_iota(jnp.int32, (CHUNK, LANES), 1)
    row_i = lax.broadcasted_iota(jnp.int32, (CHUNK, LANES), 0)
    lane_row = lax.broadcasted_iota(jnp.int32, (1, LANES), 1)
    in_a = (lane_row < RET_HEAD_DIM).astype(BF16)
    in_b = (lane_row >= RET_HEAD_DIM).astype(BF16)
    lo_half = (lane % RET_HEAD_DIM) < (RET_HEAD_DIM // 2)
    key_le_query = row_i <= lane
    lane_p = lax.broadcasted_iota(jnp.int32, (CHUNK, POOL_WIDTH), 1)
    row_p = lax.broadcasted_iota(jnp.int32, (CHUNK, POOL_WIDTH), 0)

    def pool_stages(r0, rows):
        ext = hbuf[r0:r0 + HIST_ROWS + CHUNK, 0:POOL_WIDTH]
        s2 = ext + pltpu.roll(ext, 1, 0)
        s4 = s2 + pltpu.roll(s2, 2, 0)
        s8 = s4 + pltpu.roll(s4, 4, 0)
        s16 = s8 + pltpu.roll(s8, 8, 0)
        u, s2, s4, s8, s16 = (a[HIST_ROWS:] for a in (ext, s2, s4, s8, s16))
        yield
        g0, g1, g2 = (lane_p < POOL_GROUP_DIM, lane_p < 2 * POOL_GROUP_DIM, lane_p < 3 * POOL_GROUP_DIM)
        wsum = jnp.where(g0, s2, jnp.where(g1, s4, jnp.where(g2, s8, s16)))
        width = jnp.where(g0, 2, jnp.where(g1, 4, jnp.where(g2, 8, 16)))
        tpos = row_p + (j * seq_tile + r0 + 1)
        count = jnp.minimum(tpos, width).astype(F32)
        pooled = wsum / count - u
        y_pool = _dot(pooled.astype(BF16), wpool_ref[...]) * vec(R_POOL_SCALE, POOL_WIDTH)
        mix_ref[rows, RET_WIDTH:RET_WIDTH + POOL_WIDTH] = y_pool.astype(BF16)

    def retention_stages(rows):
        pairs = range(RET_PAIRS)
        col = lambda base, p: z_ref[rows, base + p * LANES:base + (p + 1) * LANES]
        cosv = cos_ref[rows, :]
        sinv = sin_ref[rows, :]
        rope = lambda a: a * cosv + _swap_halves(a, lo_half) * sinv
        q = [rope(col(RQ, p)) for p in pairs]
        k = [rope(col(RK, p)) for p in pairs]
        yield
        kb = [k[p].astype(BF16) for p in pairs]
        vb = [col(RV, p).astype(BF16) for p in pairs]
        k2 = [jnp.concatenate([kb[p] * in_a, kb[p] * in_b], axis=0) for p in pairs]
        v2 = [jnp.concatenate([vb[p] * in_a, vb[p] * in_b], axis=0) for p in pairs]
        scores = [_dot_nt(q[p].astype(BF16), k2[p]) for p in pairs]
        kz_t = [(k[p] * zeta_ref[p]).T.astype(BF16) for p in pairs]
        yield
        lhs = [jnp.concatenate([scores[p].astype(BF16) * dtab_ref[p], (q[p] * xi_ref[p]).astype(BF16)], axis=1)
               for p in pairs]
        rhs = [jnp.concatenate([v2[p], rstate[p].astype(BF16)], axis=0) for p in pairs]
        ys = [_dot(lhs[p], rhs[p]) for p in pairs]
        kv = [_dot(kz_t[p], vb[p]) for p in pairs]
        for p in pairs:
            rstate[p] = rstate[p] * cd_ref[p] + kv[p] * bd_ref[...]
        yield
        y = jnp.concatenate(ys, axis=0)
        yc = y - _dot(y.astype(BF16), avg_ref[...])
        yield
        var = _dot((yc * yc).astype(BF16), avg_ref[...])
        yn = yc * lax.rsqrt(var + EPS)
        for p in pairs:
            cs = slice(p * LANES, (p + 1) * LANES)
            gain = rows_ref[R_RET_GN:R_RET_GN + 1, cs]
            mix_ref[rows, cs] = (_silu(col(RG, p)) * (yn[p * CHUNK:(p + 1) * CHUNK] * gain)).astype(BF16)

    def mlstm_stages(r0, rows):
        gates_t = (z_ref[rows, GT:GT + LANES] + vec(R_GATE_BIAS, LANES)).T
        li = gates_t[0:8]
        fpre = pltpu.roll(li, MLSTM_HEADS, 0)
        lf = jnp.minimum(fpre, 0.0) - jnp.log(1.0 + jnp.exp(-jnp.abs(fpre)))
        bcum = _lane_scan(lf, jnp.add, 0.0)
        g = li - bcum
        cmax = _lane_scan(g, jnp.maximum, NEG)
        m_prev = mstate[...]
        big_g = jnp.maximum(m_prev, cmax)
        inter = jnp.exp(m_prev - big_g)
        emr = jnp.exp(-(bcum + big_g))
        g_last = jnp.broadcast_to(big_g[:, LANES - 1:LANES], (8, LANES))
        b_last = jnp.broadcast_to(bcum[:, LANES - 1:LANES], (8, LANES))
        wk = jnp.exp(g - g_last)
        s_old = jnp.exp(m_prev - g_last)
        mstate[...] = b_last + g_last
        g_cols = jnp.concatenate([g, jnp.zeros((CHUNK - 8, LANES), F32)], axis=0).T
        yield
        ext = hbuf[r0 + HIST_ROWS - 8:r0 + HIST_ROWS + CHUNK, POOL_WIDTH:HIST_COLS]
        tap = lambda kk: vec(R_CONV_W + kk, 2 * MLSTM_WIDTH_P)
        prev = pltpu.roll(ext, 1, 0)
        older = pltpu.roll(tap(1) * ext + tap(0) * prev, 2, 0)
        yield
        conv = vec(R_CONV_B, 2 * MLSTM_WIDTH_P) + tap(3) * ext[8:] + tap(2) * prev[8:] + older[8:]
        qk = _silu(conv)
        yield
        v_all_t = z_ref[rows, MV:MV + MLSTM_WIDTH].T
        ones_row = jnp.where(lax.broadcasted_iota(jnp.int32, (8, LANES), 0) == 0, 1.0, 0.0)
        v_tail = jnp.concatenate([ones_row, jnp.zeros((HEAD_PAD - MLSTM_HEAD_DIM - 8, LANES), F32)], axis=0)
        heads = range(MLSTM_HEADS)
        row = lambda a, h: a[h:h + 1, :]
        qh = [(qk[:, h * HEAD_PAD:(h + 1) * HEAD_PAD] * (MLSTM_HEAD_DIM ** -0.5)).astype(BF16) for h in heads]
        kh = [qk[:, MLSTM_WIDTH_P + h * HEAD_PAD:MLSTM_WIDTH_P + (h + 1) * HEAD_PAD].astype(BF16) for h in heads]
        v_t = [jnp.concatenate([v_all_t[h * MLSTM_HEAD_DIM:(h + 1) * MLSTM_HEAD_DIM], v_tail], axis=0) for h in heads]
        c_t = [cstate[h] for h in heads]
        scores_t = [_dot_nt(kh[h], qh[h]) for h in heads]
        cross_t = [_dot_nt(c_t[h].astype(BF16), qh[h]) for h in heads]
        kv_t = [_dot((v_t[h] * row(wk, h)).astype(BF16), kh[h]) for h in heads]
        yield
        for h in heads:
            cstate[h] = c_t[h] * row(s_old, h) + kv_t[h]
        decay_t = [jnp.exp(jnp.where(key_le_query, g_cols[:, h:h + 1] - row(big_g, h), NEG)) for h in heads]
        sc_t = [(scores_t[h] * decay_t[h]).astype(BF16) for h in heads]
        yield
        nd_t = [_dot(v_t[h].astype(BF16), sc_t[h]) + cross_t[h] * row(inter, h) for h in heads]
        yield
        hn_ts = []
        for h in heads:
            den = nd_t[h][MLSTM_HEAD_DIM:MLSTM_HEAD_DIM + 1, :]
            h_t = nd_t[h][0:MLSTM_HEAD_DIM, :] * (1.0 / jnp.maximum(jnp.abs(den), row(emr, h)))
            mu = jnp.sum(h_t, axis=0, keepdims=True) / MLSTM_HEAD_DIM
            hc = h_t - mu
            var = jnp.sum(hc * hc, axis=0, keepdims=True) / MLSTM_HEAD_DIM
            hn_ts.append(hc * lax.rsqrt(var + EPS))
        yield
        hn = jnp.concatenate(hn_ts, axis=0).T
        o_gate = _sigmoid(z_ref[rows, MO:MO + MLSTM_WIDTH])
        mix_ref[rows, RET_WIDTH + POOL_WIDTH:MIX_WIDTH] = (
            o_gate * (hn * vec(R_MLSTM_GN, MLSTM_WIDTH))).astype(BF16)

    for c in range(seq_tile // CHUNK):
        r0 = c * CHUNK
        rows = pl.ds(r0, CHUNK)
        streams = [mlstm_stages(r0, rows), retention_stages(rows), pool_stages(r0, rows)]
        while streams:
            for s in list(streams):
                if next(s, StopIteration) is StopIteration:
                    streams.remove(s)
                other_work.tick()
    other_work.finish()

    hbuf[0:HIST_ROWS, :] = hbuf[seq_tile:seq_tile + HIST_ROWS, :]


def _retention_tables():
    lg = np.log1p(-(2.0 ** (-5.0 - np.arange(RET_HEADS, dtype=np.float64))))
    idx = np.arange(CHUNK, dtype=np.float64)
    rel = idx[:, None] - idx[None, :]
    head_of_lane = np.arange(LANES) // RET_HEAD_DIM
    dtab, xi, zeta, cd = [], [], [], []
    same = (head_of_lane[:, None] == head_of_lane[None, :])
    key_scale = RET_HEAD_DIM ** -0.5
    for p in range(RET_PAIRS):
        hl = lg[2 * p + head_of_lane]
        dtab.append(key_scale * np.concatenate(
            [np.where(rel >= 0, np.exp(lg[2 * p + a] * np.maximum(rel, 0.0)), 0.0) for a in range(2)], axis=1))
        xi.append(np.exp(hl[None, :] * (idx[:, None] + 1.0)))
        zeta.append(key_scale * np.exp(hl[None, :] * (CHUNK - 1 - idx[:, None])))
        cd.append(np.where(same, np.exp(hl * CHUNK)[:, None], 0.0))
    f = lambda a: jnp.asarray(np.stack(a).astype(np.float32))
    avg = jnp.asarray((same / RET_HEAD_DIM).astype(np.float32)).astype(BF16)
    return f(dtab).astype(BF16), f(xi), f(zeta), f(cd), jnp.asarray(same.astype(np.float32)), avg


def _mixers(x, cos_t, sin_t, tables, rows, wpool, w_in, layer, seq):
    t = x.shape[0]
    n_tiles = t // SEQ_TILE
    ahead = lambda k: pl.BlockSpec((SEQ_TILE, D_MODEL), lambda j: (jnp.minimum(j + k, n_tiles - 1), 0))
    tile = lambda w: pl.BlockSpec((SEQ_TILE, w), lambda j: (jnp.maximum(j - 1, 0), 0))
    full = lambda a: pl.BlockSpec(a.shape, lambda j: (0,) * a.ndim)
    resident = lambda a: pl.BlockSpec((None,) + a.shape[1:], lambda j: (layer,) + (0,) * (a.ndim - 1),
                                      pipeline_mode=pl.Buffered(1))
    consts = tuple(tables) + (rows, wpool, w_in)
    return pl.pallas_call(
        functools.partial(_mixer_kernel, seq_tile=SEQ_TILE, tiles_per_seq=seq // SEQ_TILE),
        grid=(n_tiles + 1,),
        in_specs=[ahead(0), ahead(1), tile(LANES), tile(LANES)] + [full(a) for a in tables]
        + [_layer_block(rows, layer), _layer_block(wpool, layer), resident(w_in)],
        out_specs=tile(MIX_WIDTH),
        out_shape=jax.ShapeDtypeStruct((t, MIX_WIDTH), BF16),
        scratch_shapes=[
            pltpu.VMEM((D_MODEL, Z_WIDTH), BF16),
            pltpu.VMEM((SEQ_TILE, D_MODEL), BF16),
            pltpu.VMEM((SEQ_TILE, Z_WIDTH), F32),
            pltpu.VMEM((SEQ_TILE, Z_WIDTH), F32),
            pltpu.VMEM((SEQ_TILE + HIST_ROWS, HIST_COLS), F32),
            pltpu.VMEM((RET_PAIRS, LANES, LANES), F32),
            pltpu.VMEM((MLSTM_HEADS, LANES, LANES), F32),
            pltpu.VMEM((8, LANES), F32),
        ],
        compiler_params=pltpu.CompilerParams(
            dimension_semantics=("arbitrary",), vmem_limit_bytes=VMEM_LIMIT),
        name="token_mixers",
    )(x, x, cos_t, sin_t, *consts)


FF_CHUNK = 256


def _channel_kernel(x_ref, mix_ref, p_ref, rows_ref, wout_ref, wgu_ref, wdown_ref,
                    wpg_ref, wpp_ref, o_ref, *, final):
    vec = lambda r: rows_ref[r:r + 1, :]
    x1 = x_ref[...] + _dot(mix_ref[...], wout_ref[...])
    h = _rmsnorm(x1, vec(R_NORM_FFN)).astype(BF16)
    acc = x1
    for c0 in range(0, D_FF, FF_CHUNK):
        c1 = min(c0 + FF_CHUNK, D_FF)
        gate_up = _dot(h, wgu_ref[:, 2 * c0:2 * c1])
        gate, up = gate_up[:, 0:c1 - c0], gate_up[:, c1 - c0:]
        act = (_silu(gate) * up).astype(BF16)
        acc = acc + _dot(act, wdown_ref[c0:c1, :])
    hp = _rmsnorm(acc, vec(R_NORM_PLE)).astype(BF16)
    emb = _dot(p_ref[...].astype(BF16), wpp_ref[...])
    x3 = acc + _sigmoid(_dot(hp, wpg_ref[...])) * emb
    if final:
        x3 = _rmsnorm(x3, vec(R_NORM_FINAL))
    o_ref[...] = x3


def _channel(x, mix, p, rows, wout, wgu, wdown, wpg, wpp, layer, final):
    t = x.shape[0]
    tile = lambda w: pl.BlockSpec((ROW_TILE, w), lambda i: (i, 0))
    resident = lambda a: pl.BlockSpec((None,) + a.shape[1:], lambda i: (layer,) + (0,) * (a.ndim - 1),
                                      pipeline_mode=pl.Buffered(1))
    consts = (rows, wout, wgu, wdown, wpg, wpp)
    return pl.pallas_call(
        functools.partial(_channel_kernel, final=final),
        grid=(t // ROW_TILE,),
        in_specs=[tile(D_MODEL), tile(MIX_WIDTH),
                  pl.BlockSpec((None, ROW_TILE, PLE_DIM), lambda i: (layer, i, 0))]
        + [resident(a) for a in consts],
        out_specs=tile(D_MODEL),
        out_shape=jax.ShapeDtypeStruct((t, D_MODEL), F32),
        compiler_params=pltpu.CompilerParams(
            dimension_semantics=("arbitrary",), vmem_limit_bytes=VMEM_LIMIT),
        name="channel_mixing",
    )(x, mix, p, *consts)


def _head_segments(v):
    zeros = jnp.zeros(v.shape[:-1] + (HEAD_PAD - MLSTM_HEAD_DIM,), v.dtype)
    out = []
    for h in range(v.shape[-1] // MLSTM_HEAD_DIM):
        out += [v[..., h * MLSTM_HEAD_DIM:(h + 1) * MLSTM_HEAD_DIM], zeros]
    return out


def _block_diag(w):
    depth, g, d, _ = w.shape
    eye = jnp.asarray(np.eye(g, dtype=np.float32))
    return (w[:, :, :, None, :] * eye[None, :, None, :, None]).reshape(depth, g * d, g * d)


def _pack_rows(depth, rows):
    segs = []
    for row in rows:
        width = sum(s.shape[-1] for s in row)
        segs += list(row) + [jnp.zeros((depth, D_MODEL - width), F32)]
    segs.append(jnp.zeros((depth, (N_ROWS - len(rows)) * D_MODEL), F32))
    return jnp.concatenate(segs, axis=-1).reshape(depth, N_ROWS, D_MODEL)


def kernel(x, p, positions, norm_mix, w_in, ret_gn, pool_w, pool_scale, conv_w, conv_b, b_igate, b_fgate,
           mlstm_gn, w_out, norm_ffn, w_gate_up, w_down, norm_ple, w_ple_gate, w_ple_proj, norm_final):
    batch, seq, d = x.shape
    depth = w_in.shape[0]
    t = batch * seq
    xf = x.reshape(t, d)
    cos_t, sin_t = _trig_tables(positions)
    tables = _retention_tables()

    rows = _pack_rows(depth, [
        [norm_mix], [ret_gn], [pool_scale], _head_segments(conv_b), [mlstm_gn],
        [b_igate, b_fgate], [norm_ffn], [norm_ple], [jnp.broadcast_to(norm_final, (depth, d))],
    ] + [_head_segments(conv_w[:, k]) for k in range(MLSTM_CONV)])
    wpool = _block_diag(pool_w).astype(BF16)
    w_in_t = jnp.swapaxes(w_in, 1, 2)
    wout = w_out.astype(BF16)
    wgu = w_gate_up.reshape(depth, d, 2, D_FF // FF_CHUNK, FF_CHUNK).swapaxes(2, 3).reshape(depth, d, 2 * D_FF)
    wgu, wdown = wgu.astype(BF16), w_down.astype(BF16)
    wpg, wpp = w_ple_gate.astype(BF16), w_ple_proj.astype(BF16)
    pf = p.reshape(depth, t, PLE_DIM)

    for i in range(depth):
        mix = _mixers(xf, cos_t, sin_t, tables, rows, wpool, w_in_t, i, seq)
        xf = _channel(xf, mix, pf, rows, wout, wgu, wdown, wpg, wpp, i, final=(i == depth - 1))
    return xf.reshape(batch, seq, d)
```
